```python
import jax, jax.numpy as jnp
from jax import lax
import numpy as np

D_MODEL = 1024
BATCH = 8
SEQ = 2048
DEPTH = 1

HEAD_DIM = 64
A_HEADS = 8
A_KV_HEADS = 2
B_HEADS = 8
B_KV_HEADS = 2
A_WIDTH = A_HEADS * HEAD_DIM
A_KV_WIDTH = A_KV_HEADS * HEAD_DIM
B_WIDTH = B_HEADS * HEAD_DIM
B_KV_WIDTH = B_KV_HEADS * HEAD_DIM
WINDOW = 128
BLOCK = 128
GRID_W = 64
ROPE_THETA = 10000.0
QK_EPS = 1e-6
LN_EPS = 1e-5
DN_ALPHA = (2.0 * DEPTH) ** 0.25
DN_BETA = (8.0 * DEPTH) ** -0.25

IN_SPLITS = (A_WIDTH, A_KV_WIDTH, A_KV_WIDTH, A_WIDTH,
             B_WIDTH, B_KV_WIDTH, B_KV_WIDTH, B_WIDTH,
             D_MODEL, D_MODEL)
IN_WIDTH = sum(IN_SPLITS)
IN_OFFSETS = tuple(int(v) for v in np.cumsum(IN_SPLITS)[:-1])
V_COLUMNS = (2, 6)

kernel_name = "hybrid_gated_window_axial_gqa_deepnorm"


def layer_norm(x, g, b):
    xf = x.astype(jnp.float32)
    mu = jnp.mean(xf, axis=-1, keepdims=True)
    var = jnp.mean(jnp.square(xf - mu), axis=-1, keepdims=True)
    y = (xf - mu) * lax.rsqrt(var + LN_EPS) * g.astype(jnp.float32) + b.astype(jnp.float32)
    return y.astype(x.dtype)


def rms_norm_heads(t, g):
    tf = t.astype(jnp.float32)
    y = tf * lax.rsqrt(jnp.mean(jnp.square(tf), axis=-1, keepdims=True) + QK_EPS) * g.astype(jnp.float32)
    return y.astype(t.dtype)


def axial_rope(t, row_idx, col_idx):
    half = HEAD_DIM // 2
    axis_pairs = half // 2
    freqs = ROPE_THETA ** (-jnp.arange(axis_pairs, dtype=jnp.float32) / axis_pairs)
    ang = jnp.concatenate([row_idx[:, None] * freqs, col_idx[:, None] * freqs], axis=-1)
    cos = jnp.cos(ang)[None, :, None, :]
    sin = jnp.sin(ang)[None, :, None, :]
    tf = t.astype(jnp.float32)
    t1, t2 = tf[..., :half], tf[..., half:]
    out = jnp.concatenate([t1 * cos - t2 * sin, t2 * cos + t1 * sin], axis=-1)
    return out.astype(t.dtype)


def windowed_sink_attention(q, k, v, sink):
    bsz, seq, _, dh = q.shape
    nblk = seq // BLOCK
    grp = A_HEADS // A_KV_HEADS
    qb = q.reshape(bsz, nblk, BLOCK, A_KV_HEADS, grp, dh)

    def band(t):
        tb = t.reshape(bsz, nblk, BLOCK, A_KV_HEADS, dh)
        tp = jnp.pad(tb, ((0, 0), (1, 1), (0, 0), (0, 0), (0, 0)))
        return jnp.concatenate([tp[:, :-2], tp[:, 1:-1], tp[:, 2:]], axis=2)

    kw, vw = band(k), band(v)
    s = jnp.einsum('bnqkgd,bnskd->bnkgqs', qb, kw).astype(jnp.float32) * (dh ** -0.5)

    blk = jnp.arange(nblk)
    q_pos = blk[:, None] * BLOCK + jnp.arange(BLOCK)[None, :]
    k_pos = (blk[:, None] - 1) * BLOCK + jnp.arange(3 * BLOCK)[None, :]
    dist = jnp.abs(q_pos[:, :, None] - k_pos[:, None, :])
    valid = (dist <= WINDOW) & (k_pos[:, None, :] >= 0) & (k_pos[:, None, :] < seq)

    slopes = jnp.exp2(-8.0 * (jnp.arange(A_HEADS, dtype=jnp.float32) + 1.0) / A_HEADS)
    slopes = slopes.reshape(A_KV_HEADS, grp)[None, None, :, :, None, None]
    s = s - slopes * dist.astype(jnp.float32)[None, :, None, None, :, :]
    s = jnp.where(valid[None, :, None, None, :, :], s, -jnp.inf)

    sink_l = sink.astype(jnp.float32).reshape(A_KV_HEADS, grp)[None, None, :, :, None, None]
    m = jnp.maximum(jnp.max(s, axis=-1, keepdims=True), sink_l)
    p = jnp.exp(s - m)
    denom = jnp.sum(p, axis=-1, keepdims=True) + jnp.exp(sink_l - m)
    p = (p / denom).astype(v.dtype)
    out = jnp.einsum('bnkgqs,bnskd->bnqkgd', p, vw)
    return out.reshape(bsz, seq, A_HEADS * dh)


def blockwise_global_attention(q, k, v):
    bsz, seq, _, dh = q.shape
    nblk = seq // BLOCK
    grp = B_HEADS // B_KV_HEADS
    qb = q.reshape(bsz, nblk, BLOCK, B_KV_HEADS, grp, dh).transpose(1, 0, 2, 3, 4, 5)
    scale = dh ** -0.5

    def one_block(q_blk):
        s = jnp.einsum('bqkgd,bskd->bkgqs', q_blk, k).astype(jnp.float32) * scale
        p = jax.nn.softmax(s, axis=-1).astype(v.dtype)
        return jnp.einsum('bkgqs,bskd->bqkgd', p, v)

    out = lax.map(one_block, qb)
    return out.transpose(1, 0, 2, 3, 4, 5).reshape(bsz, seq, B_HEADS * dh)


def hybrid_layer(x, w_in, b_gate, sink_a, qnorm_b, knorm_b, w_proj_a, w_proj_b, w_out, ln_g, ln_b,
                 row_idx, col_idx):
    bsz, seq, _ = x.shape
    h = x @ w_in
    qa, ka, va, za, qb, kb, vb, zb, ga, gb = jnp.split(h, IN_OFFSETS, axis=-1)

    ya = windowed_sink_attention(qa.reshape(bsz, seq, A_HEADS, HEAD_DIM),
                                 ka.reshape(bsz, seq, A_KV_HEADS, HEAD_DIM),
                                 va.reshape(bsz, seq, A_KV_HEADS, HEAD_DIM), sink_a)
    ya = (ya * jax.nn.silu(za)) @ w_proj_a

    qh = axial_rope(rms_norm_heads(qb.reshape(bsz, seq, B_HEADS, HEAD_DIM), qnorm_b), row_idx, col_idx)
    kh = axial_rope(rms_norm_heads(kb.reshape(bsz, seq, B_KV_HEADS, HEAD_DIM), knorm_b), row_idx, col_idx)
    yb = blockwise_global_attention(qh, kh, vb.reshape(bsz, seq, B_KV_HEADS, HEAD_DIM))
    yb = (yb * jax.nn.silu(zb)) @ w_proj_b

    gate_a = jax.nn.sigmoid(ga + b_gate[:D_MODEL])
    gate_b = jax.nn.sigmoid(gb + b_gate[D_MODEL:])
    y = (gate_a * ya + gate_b * yb) @ w_out

    return layer_norm(DN_ALPHA * x + y, ln_g, ln_b)


def setup_inputs(seed: int = 0) -> dict:
    key = jax.random.key(seed)
    ks = jax.random.split(key, 12)
    x = jax.random.normal(ks[0], (BATCH, SEQ, D_MODEL), jnp.float32)
    col_scale = jnp.concatenate([
        jnp.full((w,), DN_BETA if i in V_COLUMNS else 1.0, jnp.float32)
        for i, w in enumerate(IN_SPLITS)])
    w_in = jax.random.normal(ks[1], (DEPTH, D_MODEL, IN_WIDTH), jnp.float32) * (D_MODEL ** -0.5) * col_scale
    b_gate = 0.02 * jax.random.normal(ks[2], (DEPTH, 2 * D_MODEL), jnp.float32)
    sink_a = 0.5 * jax.random.normal(ks[3], (DEPTH, A_HEADS), jnp.float32)
    qnorm_b = 1.0 + 0.02 * jax.random.normal(ks[4], (DEPTH, HEAD_DIM), jnp.float32)
    knorm_b = 1.0 + 0.02 * jax.random.normal(ks[5], (DEPTH, HEAD_DIM), jnp.float32)
    w_proj_a = jax.random.normal(ks[6], (DEPTH, A_WIDTH, D_MODEL), jnp.float32) * (A_WIDTH ** -0.5) * DN_BETA
    w_proj_b = jax.random.normal(ks[7], (DEPTH, B_WIDTH, D_MODEL), jnp.float32) * (B_WIDTH ** -0.5) * DN_BETA
    w_out = jax.random.normal(ks[8], (DEPTH, D_MODEL, D_MODEL), jnp.float32) * (D_MODEL ** -0.5) * DN_BETA
    ln_g = 1.0 + 0.02 * jax.random.normal(ks[9], (DEPTH, D_MODEL), jnp.float32)
    ln_b = 0.02 * jax.random.normal(ks[10], (DEPTH, D_MODEL), jnp.float32)
    return {"x": x, "w_in": w_in, "b_gate": b_gate, "sink_a": sink_a, "qnorm_b": qnorm_b,
            "knorm_b": knorm_b, "w_proj_a": w_proj_a, "w_proj_b": w_proj_b, "w_out": w_out,
            "ln_g": ln_g, "ln_b": ln_b}


def reference(x, w_in, b_gate, sink_a, qnorm_b, knorm_b, w_proj_a, w_proj_b, w_out, ln_g, ln_b):
    seq = x.shape[1]
    rows = seq // GRID_W
    grid_r, grid_c = jnp.meshgrid(jnp.arange(rows), jnp.arange(GRID_W), indexing='ij')
    row_idx = grid_r.reshape(-1).astype(jnp.float32)
    col_idx = grid_c.reshape(-1).astype(jnp.float32)
    h = x
    for layer in range(DEPTH):
        h = hybrid_layer(h, w_in[layer], b_gate[layer], sink_a[layer], qnorm_b[layer], knorm_b[layer],
                         w_proj_a[layer], w_proj_b[layer], w_out[layer], ln_g[layer], ln_b[layer],
                         row_idx, col_idx)
    return h
```

```python
import functools

import jax
import jax.numpy as jnp
import numpy as np
from jax import lax
from jax.experimental import pallas as pl
from jax.experimental.pallas import tpu as pltpu

D_MODEL = 1024
HEAD_DIM = 64
HEADS = 8
KV_HEADS = 2
GROUP = HEADS // KV_HEADS
Q_WIDTH = HEADS * HEAD_DIM
KV_WIDTH = KV_HEADS * HEAD_DIM
WINDOW = 128
BLOCK = 128
GRID_W = 64
ROPE_THETA = 10000.0
QK_EPS = 1e-6
LN_EPS = 1e-5
DEPTH = 1
DN_ALPHA = (2.0 * DEPTH) ** 0.25
SCALE = HEAD_DIM ** -0.5

_SPLITS = (Q_WIDTH, KV_WIDTH, KV_WIDTH, Q_WIDTH, Q_WIDTH, KV_WIDTH, KV_WIDTH, Q_WIDTH, D_MODEL, D_MODEL)
_OFF = tuple(int(v) for v in np.cumsum((0,) + _SPLITS))
(QA, KA, VA, ZA, QB, KB, VB, ZB, GA, GB) = range(10)

V_AUG = HEAD_DIM + 16
VMEM_LIMIT = 56 * 1024 * 1024

BF16 = jnp.bfloat16
F32 = jnp.float32


def _cols(w, i):
    return w[:, _OFF[i]:_OFF[i + 1]]


def _qkv_kernel(x_ref, wt_ref, wk_ref, gq_ref, gk_ref, cosT_ref, sinT_ref, cosk_ref, sink_ref,
                qaT_ref, ka_ref, vaT_ref, qbT_ref, kb_ref, vbT_ref):
    tm = x_ref.shape[1]
    xb = x_ref[0].astype(BF16)
    t = lax.dot_general(wt_ref[...], xb, (((1,), (1,)), ((), ())), preferred_element_type=F32)
    kk = jnp.dot(xb, wk_ref[...], preferred_element_type=F32)

    ones = jnp.ones((V_AUG - HEAD_DIM, tm), BF16)

    def store_v(vT_ref, vT):
        for g in range(KV_HEADS):
            vT_ref[0, g, 0:HEAD_DIM, :] = vT[g * HEAD_DIM:(g + 1) * HEAD_DIM].astype(BF16)
            vT_ref[0, g, HEAD_DIM:V_AUG, :] = ones

    r0 = 0
    qaT_ref[0] = (t[r0:r0 + Q_WIDTH] * SCALE).astype(BF16)
    r0 += Q_WIDTH
    store_v(vaT_ref, t[r0:r0 + KV_WIDTH])
    r0 += KV_WIDTH
    qb = t[r0:r0 + Q_WIDTH].reshape(HEADS, HEAD_DIM, tm)
    r0 += Q_WIDTH
    store_v(vbT_ref, t[r0:r0 + KV_WIDTH])

    half = HEAD_DIM // 2
    ms = jnp.mean(qb * qb, axis=1, keepdims=True)
    qn = qb * lax.rsqrt(ms + QK_EPS) * gq_ref[...][None]
    q1, q2 = qn[:, :half], qn[:, half:]
    cosT, sinT = cosT_ref[...][None], sinT_ref[...][None]
    qr = jnp.concatenate([q1 * cosT - q2 * sinT, q2 * cosT + q1 * sinT], axis=1)
    qbT_ref[0] = (qr * SCALE).reshape(Q_WIDTH, tm).astype(BF16)

    ka_ref[0] = kk[:, :KV_WIDTH].astype(BF16)

    kb = kk[:, KV_WIDTH:]
    lane = lax.broadcasted_iota(jnp.int32, kb.shape, 1)
    k2 = kb * kb
    first = lane < HEAD_DIM
    ms0 = jnp.sum(jnp.where(first, k2, 0.0), axis=1, keepdims=True) * (1.0 / HEAD_DIM)
    ms1 = jnp.sum(jnp.where(first, 0.0, k2), axis=1, keepdims=True) * (1.0 / HEAD_DIM)
    kn = kb * lax.rsqrt(jnp.where(first, ms0, ms1) + QK_EPS) * gk_ref[...]
    swapped = jnp.where(lane % HEAD_DIM < half,
                        pltpu.roll(kn, KV_WIDTH - half, axis=1),
                        pltpu.roll(kn, half, axis=1))
    kb_ref[0] = (kn * cosk_ref[...] + swapped * sink_ref[...]).astype(BF16)


def _qkv_proj(x, wt, wk, gq, gk, cosT, sinT, cosk, sink, tm):
    bsz, seq, d = x.shape
    nt = seq // tm
    const = lambda shape: pl.BlockSpec(shape, lambda b, i: (0,) * len(shape))
    qT_spec = pl.BlockSpec((1, Q_WIDTH, tm), lambda b, i: (b, 0, i))
    k_spec = pl.BlockSpec((1, tm, KV_WIDTH), lambda b, i: (b, i, 0))
    vT_spec = pl.BlockSpec((1, KV_HEADS, V_AUG, tm), lambda b, i: (b, 0, 0, i))
    qT_shape = jax.ShapeDtypeStruct((bsz, Q_WIDTH, seq), BF16)
    k_shape = jax.ShapeDtypeStruct((bsz, seq, KV_WIDTH), BF16)
    vT_shape = jax.ShapeDtypeStruct((bsz, KV_HEADS, V_AUG, seq), BF16)
    return pl.pallas_call(
        _qkv_kernel,
        grid=(bsz, nt),
        in_specs=[
            pl.BlockSpec((1, tm, d), lambda b, i: (b, i, 0)),
            const(wt.shape), const(wk.shape), const(gq.shape), const(gk.shape),
            pl.BlockSpec((HEAD_DIM // 2, tm), lambda b, i: (0, i)),
            pl.BlockSpec((HEAD_DIM // 2, tm), lambda b, i: (0, i)),
            pl.BlockSpec((tm, KV_WIDTH), lambda b, i: (i, 0)),
            pl.BlockSpec((tm, KV_WIDTH), lambda b, i: (i, 0)),
        ],
        out_specs=[qT_spec, k_spec, vT_spec, qT_spec, k_spec, vT_spec],
        out_shape=[qT_shape, k_shape, vT_shape, qT_shape, k_shape, vT_shape],
        compiler_params=pltpu.CompilerParams(
            dimension_semantics=("parallel", "parallel"), vmem_limit_bytes=VMEM_LIMIT),
        name="qkv_proj",
    )(x, wt, wk, gq, gk, cosT, sinT, cosk, sink)


def _padded_q(qT_h, g):
    zeros = jnp.zeros_like(qT_h)
    parts = [zeros] * KV_HEADS
    parts[g] = qT_h
    return jnp.concatenate(parts, axis=0)


def _normalise(oT, extra=None):
    denom = oT[HEAD_DIM:HEAD_DIM + 1]
    if extra is not None:
        denom = denom + extra
    return oT[:HEAD_DIM] * (1.0 / denom)


def _attn_window_kernel(sink_ref, qT_ref, k_ref, vT_ref, o_ref, bias_ref):
    seq = k_ref.shape[1]
    nblk = seq // BLOCK
    span = 3 * BLOCK

    r = lax.broadcasted_iota(jnp.int32, (span, BLOCK), 0)
    c = lax.broadcasted_iota(jnp.int32, (span, BLOCK), 1)
    dist = jnp.abs(r - BLOCK - c)
    distf = dist.astype(F32)
    for h in range(HEADS):
        slope = 2.0 ** (-8.0 * (h + 1) / HEADS)
        bias_ref[h] = jnp.where(dist <= WINDOW, -slope * distf, -jnp.inf)

    def block(n, k_lo, nk, b_lo):
        q_lo = n * BLOCK
        if not isinstance(n, int):
            q_lo = pl.multiple_of(q_lo, BLOCK)
            k_lo = pl.multiple_of(k_lo, BLOCK)
        kwin = k_ref[0, pl.ds(k_lo, nk), :]
        qT = qT_ref[0, :, pl.ds(q_lo, BLOCK)]
        outs = []
        for h0 in range(0, HEADS, 2):
            g = h0 // GROUP
            w = jnp.concatenate(
                [_padded_q(qT[h * HEAD_DIM:(h + 1) * HEAD_DIM], g) for h in (h0, h0 + 1)], axis=1)
            sT = jnp.dot(kwin, w, preferred_element_type=F32)
            ps, sink_terms = [], []
            for i, h in enumerate((h0, h0 + 1)):
                s = sT[:, i * BLOCK:(i + 1) * BLOCK] + bias_ref[h, b_lo:b_lo + nk, :]
                sink = sink_ref[h]
                m = jnp.maximum(jnp.max(s, axis=0, keepdims=True), sink)
                ps.append(jnp.exp(s - m).astype(BF16))
                sink_terms.append(jnp.exp(sink - m))
            pT = jnp.concatenate(ps, axis=1)
            oT = jnp.dot(vT_ref[0, g, :, pl.ds(k_lo, nk)], pT, preferred_element_type=F32)
            o = _normalise(oT, jnp.concatenate(sink_terms, axis=1))
            outs += [o[:, :BLOCK], o[:, BLOCK:]]
        oT_all = jnp.concatenate(outs, axis=0)
        o_ref[0, pl.ds(q_lo, BLOCK), :] = oT_all.T.astype(o_ref.dtype)

    block(0, 0, 2 * BLOCK, BLOCK)

    def body(n, carry):
        block(n, (n - 1) * BLOCK, span, 0)
        return carry

    lax.fori_loop(1, nblk - 1, body, 0)
    block(nblk - 1, (nblk - 2) * BLOCK, 2 * BLOCK, 0)


def _attn_window(sink, qT, k, vT):
    bsz, _, seq = qT.shape
    return pl.pallas_call(
        _attn_window_kernel,
        grid=(bsz,),
        in_specs=[
            pl.BlockSpec(memory_space=pltpu.SMEM),
            pl.BlockSpec((1, Q_WIDTH, seq), lambda b: (b, 0, 0)),
            pl.BlockSpec((1, seq, KV_WIDTH), lambda b: (b, 0, 0)),
            pl.BlockSpec((1, KV_HEADS, V_AUG, seq), lambda b: (b, 0, 0, 0)),
        ],
        out_specs=pl.BlockSpec((1, seq, Q_WIDTH), lambda b: (b, 0, 0)),
        out_shape=jax.ShapeDtypeStruct((bsz, seq, Q_WIDTH), BF16),
        scratch_shapes=[pltpu.VMEM((HEADS, 3 * BLOCK, BLOCK), F32)],
        compiler_params=pltpu.CompilerParams(
            dimension_semantics=("parallel",), vmem_limit_bytes=VMEM_LIMIT),
        name="attn_window",
    )(sink, qT, k, vT)


def _attn_global_kernel(qT_ref, k_ref, vT_ref, o_ref, oT_ref, *, tq):
    seq = k_ref.shape[1]
    gw = GROUP * HEAD_DIM

    def q_tile(t, carry):
        q_lo = pl.multiple_of(t * tq, tq)
        for g in range(KV_HEADS):
            def head(j, c):
                row = pl.multiple_of(g * gw + j * HEAD_DIM, HEAD_DIM)
                qT_h = qT_ref[0, pl.ds(row, HEAD_DIM), pl.ds(q_lo, tq)]
                sT = jnp.dot(k_ref[0], _padded_q(qT_h, g), preferred_element_type=F32)
                m = jnp.max(sT, axis=0, keepdims=True)
                pT = jnp.exp(sT - m).astype(BF16)
                oT = jnp.dot(vT_ref[0, g], pT, preferred_element_type=F32)
                oT_ref[pl.ds(pl.multiple_of(j * HEAD_DIM, HEAD_DIM), HEAD_DIM), :] = _normalise(oT)
                return c

            lax.fori_loop(0, GROUP, head, 0)
            o_ref[0, pl.ds(q_lo, tq), g * gw:(g + 1) * gw] = oT_ref[...].T.astype(o_ref.dtype)
        return carry

    lax.fori_loop(0, seq // tq, q_tile, 0)


def _attn_global(qT, k, vT, tq):
    bsz, _, seq = qT.shape
    return pl.pallas_call(
        functools.partial(_attn_global_kernel, tq=tq),
        grid=(bsz,),
        in_specs=[
            pl.BlockSpec((1, Q_WIDTH, seq), lambda b: (b, 0, 0)),
            pl.BlockSpec((1, seq, KV_WIDTH), lambda b: (b, 0, 0)),
            pl.BlockSpec((1, KV_HEADS, V_AUG, seq), lambda b: (b, 0, 0, 0)),
        ],
        out_specs=pl.BlockSpec((1, seq, Q_WIDTH), lambda b: (b, 0, 0)),
        out_shape=jax.ShapeDtypeStruct((bsz, seq, Q_WIDTH), BF16),
        scratch_shapes=[pltpu.VMEM((GROUP * HEAD_DIM, tq), F32)],
        compiler_params=pltpu.CompilerParams(
            dimension_semantics=("parallel",), vmem_limit_bytes=VMEM_LIMIT),
        name="attn_global",
    )(qT, k, vT)


def _out_kernel(x_ref, ya_ref, yb_ref, wz_ref, wg_ref, bg_ref, wpa_ref, wpb_ref, wo_ref, lng_ref, lnb_ref,
                o_ref):
    x = x_ref[...]
    xb = x.astype(BF16)
    z = jnp.dot(xb, wz_ref[...], preferred_element_type=F32)
    sz = z * jax.nn.sigmoid(z)
    ua = (ya_ref[...].astype(F32) * sz[:, :Q_WIDTH]).astype(BF16)
    ub = (yb_ref[...].astype(F32) * sz[:, Q_WIDTH:]).astype(BF16)
    pa = jnp.dot(ua, wpa_ref[...], preferred_element_type=F32)
    pb = jnp.dot(ub, wpb_ref[...], preferred_element_type=F32)
    gate = jax.nn.sigmoid(jnp.dot(xb, wg_ref[...], preferred_element_type=F32) + bg_ref[...])
    mix = (gate[:, :D_MODEL] * pa + gate[:, D_MODEL:] * pb).astype(BF16)
    y = jnp.dot(mix, wo_ref[...], preferred_element_type=F32)
    r = DN_ALPHA * x + y
    mu = jnp.mean(r, axis=-1, keepdims=True)
    rc = r - mu
    var = jnp.mean(rc * rc, axis=-1, keepdims=True)
    o_ref[...] = rc * lax.rsqrt(var + LN_EPS) * lng_ref[...] + lnb_ref[...]


def _out_block(x2, ya2, yb2, wz, wg, bg, wpa, wpb, wo, lng, lnb, tm):
    n, d = x2.shape
    const = lambda a: pl.BlockSpec(a.shape, lambda i: (0, 0), pipeline_mode=pl.Buffered(1))
    row = lambda w: pl.BlockSpec((tm, w), lambda i: (i, 0))
    return pl.pallas_call(
        _out_kernel,
        grid=(n // tm,),
        in_specs=[row(d), row(Q_WIDTH), row(Q_WIDTH),
                  const(wz), const(wg), const(bg), const(wpa), const(wpb), const(wo), const(lng), const(lnb)],
        out_specs=row(d),
        out_shape=jax.ShapeDtypeStruct((n, d), x2.dtype),
        compiler_params=pltpu.CompilerParams(
            dimension_semantics=("parallel",), vmem_limit_bytes=VMEM_LIMIT),
        name="out_block",
    )(x2, ya2, yb2, wz, wg, bg, wpa, wpb, wo, lng, lnb)


def _rope_tables(seq):
    half = HEAD_DIM // 2
    axis_pairs = half // 2
    pos = jnp.arange(seq)
    row_idx = (pos // GRID_W).astype(F32)
    col_idx = (pos % GRID_W).astype(F32)
    freqs = ROPE_THETA ** (-jnp.arange(axis_pairs, dtype=F32) / axis_pairs)
    ang = jnp.concatenate([row_idx[:, None] * freqs, col_idx[:, None] * freqs], axis=-1)
    cos, sin = jnp.cos(ang), jnp.sin(ang)
    cosk = jnp.tile(cos, (1, KV_WIDTH // half))
    sink = jnp.tile(jnp.concatenate([-sin, sin], axis=-1), (1, KV_HEADS))
    return cos.T, sin.T, cosk, sink


def _layer(x, w_in, b_gate, sink_a, qnorm_b, knorm_b, w_proj_a, w_proj_b, w_out, ln_g, ln_b):
    bsz, seq, d = x.shape
    tm_qkv, tq, tm_out = 512, 256, 512

    wt = jnp.concatenate([_cols(w_in, i) for i in (QA, VA, QB, VB)], axis=1).T.astype(BF16)
    wk = jnp.concatenate([_cols(w_in, KA), _cols(w_in, KB)], axis=1).astype(BF16)
    wz = jnp.concatenate([_cols(w_in, ZA), _cols(w_in, ZB)], axis=1).astype(BF16)
    wg = jnp.concatenate([_cols(w_in, GA), _cols(w_in, GB)], axis=1).astype(BF16)
    gq = jnp.broadcast_to(qnorm_b.astype(F32)[:, None], (HEAD_DIM, tm_qkv))
    gk = jnp.tile(knorm_b.astype(F32), KV_HEADS)[None, :]
    cosT, sinT, cosk, sink = _rope_tables(seq)

    qaT, ka, vaT, qbT, kb, vbT = _qkv_proj(x, wt, wk, gq, gk, cosT, sinT, cosk, sink, tm_qkv)
    ya = _attn_window(sink_a.astype(F32), qaT, ka, vaT)
    yb = _attn_global(qbT, kb, vbT, tq)

    out = _out_block(
        x.reshape(bsz * seq, d), ya.reshape(bsz * seq, Q_WIDTH), yb.reshape(bsz * seq, Q_WIDTH),
        wz, wg, b_gate.astype(F32)[None, :], w_proj_a.astype(BF16), w_proj_b.astype(BF16), w_out.astype(BF16),
        ln_g.astype(F32)[None, :], ln_b.astype(F32)[None, :], tm_out)
    return out.reshape(bsz, seq, d)


def kernel(x, w_in, b_gate, sink_a, qnorm_b, knorm_b, w_proj_a, w_proj_b, w_out, ln_g, ln_b):
    h = x
    for layer in range(DEPTH):
        h = _layer(h, w_in[layer], b_gate[layer], sink_a[layer], qnorm_b[layer], knorm_b[layer],
                   w_proj_a[layer], w_proj_b[layer], w_out[layer], ln_g[layer], ln_b[layer])
    return h
```

```python
import functools

import jax
import jax.numpy as jnp
import numpy as np
from jax import lax
from jax.experimental import pallas as pl
from jax.experimental.pallas import tpu as pltpu

D_MODEL = 1024
HEAD_DIM = 64
HEADS = 8
KV_HEADS = 2
GROUP = HEADS // KV_HEADS
Q_WIDTH = HEADS * HEAD_DIM
KV_WIDTH = KV_HEADS * HEAD_DIM
WINDOW = 128
BLOCK = 128
GRID_W = 64
ROPE_THETA = 10000.0
QK_EPS = 1e-6
LN_EPS = 1e-5
DEPTH = 1
DN_ALPHA = (2.0 * DEPTH) ** 0.25
SCALE = HEAD_DIM ** -0.5
LOG2E = 1.4426950408889634

_SPLITS = (Q_WIDTH, KV_WIDTH, KV_WIDTH, Q_WIDTH, Q_WIDTH, KV_WIDTH, KV_WIDTH, Q_WIDTH, D_MODEL, D_MODEL)
_OFF = tuple(int(v) for v in np.cumsum((0,) + _SPLITS))
(QA, KA, VA, ZA, QB, KB, VB, ZB, GA, GB) = range(10)

V_AUG = HEAD_DIM + 16
VMEM_LIMIT = 56 * 1024 * 1024

BF16 = jnp.bfloat16
F32 = jnp.float32


def _cols(w, i):
    return w[:, _OFF[i]:_OFF[i + 1]]


def _qkv_kernel(x_ref, wt_ref, wk_ref, gq_ref, gk_ref, cosT_ref, sinT_ref, cosk_ref, sink_ref,
                qaT_ref, ka_ref, vaT_ref, qbT_ref, kb_ref, vbT_ref):
    tm = x_ref.shape[1]
    xb = x_ref[0].astype(BF16)
    t = lax.dot_general(wt_ref[...], xb, (((1,), (1,)), ((), ())), preferred_element_type=F32)
    kk = jnp.dot(xb, wk_ref[...], preferred_element_type=F32)

    ones = jnp.ones((V_AUG - HEAD_DIM, tm), BF16)

    def store_v(vT_ref, vT):
        for g in range(KV_HEADS):
            vT_ref[0, g, 0:HEAD_DIM, :] = vT[g * HEAD_DIM:(g + 1) * HEAD_DIM].astype(BF16)
            vT_ref[0, g, HEAD_DIM:V_AUG, :] = ones

    r0 = 0
    qaT_ref[0] = (t[r0:r0 + Q_WIDTH] * SCALE).astype(BF16)
    r0 += Q_WIDTH
    store_v(vaT_ref, t[r0:r0 + KV_WIDTH])
    r0 += KV_WIDTH
    qb = t[r0:r0 + Q_WIDTH].reshape(HEADS, HEAD_DIM, tm)
    r0 += Q_WIDTH
    store_v(vbT_ref, t[r0:r0 + KV_WIDTH])

    half = HEAD_DIM // 2
    ms = jnp.mean(qb * qb, axis=1, keepdims=True)
    qn = qb * lax.rsqrt(ms + QK_EPS) * gq_ref[...][None]
    q1, q2 = qn[:, :half], qn[:, half:]
    cosT, sinT = cosT_ref[...][None], sinT_ref[...][None]
    qr = jnp.concatenate([q1 * cosT - q2 * sinT, q2 * cosT + q1 * sinT], axis=1)
    qbT_ref[0] = (qr * (SCALE * LOG2E)).reshape(Q_WIDTH, tm).astype(BF16)

    ka_ref[0] = kk[:, :KV_WIDTH].astype(BF16)

    kb = kk[:, KV_WIDTH:]
    lane = lax.broadcasted_iota(jnp.int32, kb.shape, 1)
    k2 = kb * kb
    first = lane < HEAD_DIM
    ms0 = jnp.sum(jnp.where(first, k2, 0.0), axis=1, keepdims=True) * (1.0 / HEAD_DIM)
    ms1 = jnp.sum(jnp.where(first, 0.0, k2), axis=1, keepdims=True) * (1.0 / HEAD_DIM)
    kn = kb * lax.rsqrt(jnp.where(first, ms0, ms1) + QK_EPS) * gk_ref[...]
    swapped = jnp.where(lane % HEAD_DIM < half,
                        pltpu.roll(kn, KV_WIDTH - half, axis=1),
                        pltpu.roll(kn, half, axis=1))
    kb_ref[0] = (kn * cosk_ref[...] + swapped * sink_ref[...]).astype(BF16)


def _qkv_proj(x, wt, wk, gq, gk, cosT, sinT, cosk, sink, tm):
    bsz, seq, d = x.shape
    nt = seq // tm
    const = lambda shape: pl.BlockSpec(shape, lambda b, i: (0,) * len(shape))
    qT_spec = pl.BlockSpec((1, Q_WIDTH, tm), lambda b, i: (b, 0, i))
    k_spec = pl.BlockSpec((1, tm, KV_WIDTH), lambda b, i: (b, i, 0))
    vT_spec = pl.BlockSpec((1, KV_HEADS, V_AUG, tm), lambda b, i: (b, 0, 0, i))
    qT_shape = jax.ShapeDtypeStruct((bsz, Q_WIDTH, seq), BF16)
    k_shape = jax.ShapeDtypeStruct((bsz, seq, KV_WIDTH), BF16)
    vT_shape = jax.ShapeDtypeStruct((bsz, KV_HEADS, V_AUG, seq), BF16)
    return pl.pallas_call(
        _qkv_kernel,
        grid=(bsz, nt),
        in_specs=[
            pl.BlockSpec((1, tm, d), lambda b, i: (b, i, 0)),
            const(wt.shape), const(wk.shape), const(gq.shape), const(gk.shape),
            pl.BlockSpec((HEAD_DIM // 2, tm), lambda b, i: (0, i)),
            pl.BlockSpec((HEAD_DIM // 2, tm), lambda b, i: (0, i)),
            pl.BlockSpec((tm, KV_WIDTH), lambda b, i: (i, 0)),
            pl.BlockSpec((tm, KV_WIDTH), lambda b, i: (i, 0)),
        ],
        out_specs=[qT_spec, k_spec, vT_spec, qT_spec, k_spec, vT_spec],
        out_shape=[qT_shape, k_shape, vT_shape, qT_shape, k_shape, vT_shape],
        compiler_params=pltpu.CompilerParams(
            dimension_semantics=("parallel", "parallel"), vmem_limit_bytes=VMEM_LIMIT),
        name="qkv_proj",
    )(x, wt, wk, gq, gk, cosT, sinT, cosk, sink)


def _padded_q(qT_h, g):
    zeros = jnp.zeros_like(qT_h)
    parts = [zeros] * KV_HEADS
    parts[g] = qT_h
    return jnp.concatenate(parts, axis=0)


def _normalise(oT, extra=None):
    denom = oT[HEAD_DIM:HEAD_DIM + 1]
    if extra is not None:
        denom = denom + extra
    return oT[:HEAD_DIM] * (1.0 / denom)


def _attn_window_kernel(sink_ref, qT_ref, k_ref, vT_ref, o_ref, bias_ref):
    seq = k_ref.shape[1]
    nblk = seq // BLOCK
    span = 3 * BLOCK

    r = lax.broadcasted_iota(jnp.int32, (span, BLOCK), 0)
    c = lax.broadcasted_iota(jnp.int32, (span, BLOCK), 1)
    dist = jnp.abs(r - BLOCK - c)
    distf = dist.astype(F32)
    for h in range(HEADS):
        slope = 2.0 ** (-8.0 * (h + 1) / HEADS)
        bias_ref[h] = jnp.where(dist <= WINDOW, -slope * distf, -jnp.inf)

    def block(n, k_lo, nk, b_lo):
        q_lo = n * BLOCK
        if not isinstance(n, int):
            q_lo = pl.multiple_of(q_lo, BLOCK)
            k_lo = pl.multiple_of(k_lo, BLOCK)
        kwin = k_ref[0, pl.ds(k_lo, nk), :]
        qT = qT_ref[0, :, pl.ds(q_lo, BLOCK)]
        outs = []
        for h0 in range(0, HEADS, 2):
            g = h0 // GROUP
            w = jnp.concatenate(
                [_padded_q(qT[h * HEAD_DIM:(h + 1) * HEAD_DIM], g) for h in (h0, h0 + 1)], axis=1)
            sT = jnp.dot(kwin, w, preferred_element_type=F32)
            ps, sink_terms = [], []
            for i, h in enumerate((h0, h0 + 1)):
                s = sT[:, i * BLOCK:(i + 1) * BLOCK] + bias_ref[h, b_lo:b_lo + nk, :]
                sink = sink_ref[h]
                m = jnp.maximum(jnp.max(s, axis=0, keepdims=True), sink)
                ps.append(jnp.exp(s - m).astype(BF16))
                sink_terms.append(jnp.exp(sink - m))
            pT = jnp.concatenate(ps, axis=1)
            oT = jnp.dot(vT_ref[0, g, :, pl.ds(k_lo, nk)], pT, preferred_element_type=F32)
            o = _normalise(oT, jnp.concatenate(sink_terms, axis=1))
            outs += [o[:, :BLOCK], o[:, BLOCK:]]
        oT_all = jnp.concatenate(outs, axis=0)
        o_ref[0, pl.ds(q_lo, BLOCK), :] = oT_all.T.astype(o_ref.dtype)

    block(0, 0, 2 * BLOCK, BLOCK)

    def body(n, carry):
        block(n, (n - 1) * BLOCK, span, 0)
        return carry

    lax.fori_loop(1, nblk - 1, body, 0)
    block(nblk - 1, (nblk - 2) * BLOCK, 2 * BLOCK, 0)


def _attn_window(sink, qT, k, vT):
    bsz, _, seq = qT.shape
    return pl.pallas_call(
        _attn_window_kernel,
        grid=(bsz,),
        in_specs=[
            pl.BlockSpec(memory_space=pltpu.SMEM),
            pl.BlockSpec((1, Q_WIDTH, seq), lambda b: (b, 0, 0)),
            pl.BlockSpec((1, seq, KV_WIDTH), lambda b: (b, 0, 0)),
            pl.BlockSpec((1, KV_HEADS, V_AUG, seq), lambda b: (b, 0, 0, 0)),
        ],
        out_specs=pl.BlockSpec((1, seq, Q_WIDTH), lambda b: (b, 0, 0)),
        out_shape=jax.ShapeDtypeStruct((bsz, seq, Q_WIDTH), BF16),
        scratch_shapes=[pltpu.VMEM((HEADS, 3 * BLOCK, BLOCK), F32)],
        compiler_params=pltpu.CompilerParams(
            dimension_semantics=("parallel",), vmem_limit_bytes=VMEM_LIMIT),
        name="attn_window",
    )(sink, qT, k, vT)


def _fold_max(s):
    rows = s.shape[0]
    while rows > 8:
        rows //= 2
        s = jnp.maximum(s[:rows], s[rows:])
    return s


def _attn_global_kernel(qT_ref, k_ref, vT_ref, o_ref, s_ref, oT_ref, *, tq, kc):
    seq = k_ref.shape[1]
    gw = GROUP * HEAD_DIM
    n_items = (seq // tq) * HEADS
    chunks = range(0, seq, kc)

    def item(i):
        t, h = i // HEADS, i % HEADS
        return t, h // GROUP, h % GROUP, h

    def padded_q(i):
        t, g, _, h = item(i)
        qT_h = qT_ref[0, pl.ds(pl.multiple_of(h * HEAD_DIM, HEAD_DIM), HEAD_DIM),
                      pl.ds(pl.multiple_of(t * tq, tq), tq)]
        return jnp.where(g == 0, _padded_q(qT_h, 0), _padded_q(qT_h, 1))

    def scores_chunk(slot, w, k_lo, m8):
        sT = jnp.dot(k_ref[0, k_lo:k_lo + kc, :], w, preferred_element_type=F32)
        s_ref[slot, k_lo:k_lo + kc, :] = sT
        m8_c = _fold_max(sT)
        return m8_c if m8 is None else jnp.maximum(m8, m8_c)

    def values_chunk(i, slot, m, k_lo, acc):
        _, g, _, _ = item(i)
        pT = jnp.exp2(s_ref[slot, k_lo:k_lo + kc, :] - m).astype(BF16)
        oT = jnp.dot(vT_ref[0, g, :, k_lo:k_lo + kc], pT, preferred_element_type=F32)
        return oT if acc is None else acc + oT

    def finish(i, acc):
        t, g, j, _ = item(i)
        oT_ref[pl.ds(pl.multiple_of(j * HEAD_DIM, HEAD_DIM), HEAD_DIM), :] = _normalise(acc)

        @pl.when(j == GROUP - 1)
        def _():
            o_ref[0, pl.ds(pl.multiple_of(t * tq, tq), tq), pl.ds(pl.multiple_of(g * gw, gw), gw)] = (
                oT_ref[...].T.astype(o_ref.dtype))

    def step(i, slot, m_prev, do_scores=True, do_values=True):
        w = padded_q(i) if do_scores else None
        m8 = acc = None
        for k_lo in chunks:
            if do_scores:
                m8 = scores_chunk(slot, w, k_lo, m8)
            if do_values:
                acc = values_chunk(i - 1, 1 - slot, m_prev, k_lo, acc)
        if do_values:
            finish(i - 1, acc)
        return jnp.max(m8, axis=0, keepdims=True) if do_scores else None

    def pair(u, m):
        return step(2 * u + 2, 0, step(2 * u + 1, 1, m))

    m = step(0, 0, None, do_values=False)
    m = lax.fori_loop(0, n_items // 2 - 1, pair, m)
    m = step(n_items - 1, 1, m)
    step(n_items, 0, m, do_scores=False)


def _attn_global(qT, k, vT, tq, kc):
    bsz, _, seq = qT.shape
    return pl.pallas_call(
        functools.partial(_attn_global_kernel, tq=tq, kc=kc),
        grid=(bsz,),
        in_specs=[
            pl.BlockSpec((1, Q_WIDTH, seq), lambda b: (b, 0, 0)),
            pl.BlockSpec((1, seq, KV_WIDTH), lambda b: (b, 0, 0)),
            pl.BlockSpec((1, KV_HEADS, V_AUG, seq), lambda b: (b, 0, 0, 0)),
        ],
        out_specs=pl.BlockSpec((1, seq, Q_WIDTH), lambda b: (b, 0, 0)),
        out_shape=jax.ShapeDtypeStruct((bsz, seq, Q_WIDTH), BF16),
        scratch_shapes=[pltpu.VMEM((2, seq, tq), F32), pltpu.VMEM((GROUP * HEAD_DIM, tq), F32)],
        compiler_params=pltpu.CompilerParams(
            dimension_semantics=("parallel",), vmem_limit_bytes=VMEM_LIMIT),
        name="attn_global",
    )(qT, k, vT)


def _out_kernel(x_ref, ya_ref, yb_ref, wz_ref, wg_ref, bg_ref, wpa_ref, wpb_ref, wo_ref, lng_ref, lnb_ref,
                o_ref):
    x = x_ref[...]
    xb = x.astype(BF16)
    z = jnp.dot(xb, wz_ref[...], preferred_element_type=F32)
    sz = z * jax.nn.sigmoid(z)
    ua = (ya_ref[...].astype(F32) * sz[:, :Q_WIDTH]).astype(BF16)
    ub = (yb_ref[...].astype(F32) * sz[:, Q_WIDTH:]).astype(BF16)
    pa = jnp.dot(ua, wpa_ref[...], preferred_element_type=F32)
    pb = jnp.dot(ub, wpb_ref[...], preferred_element_type=F32)
    gate = jax.nn.sigmoid(jnp.dot(xb, wg_ref[...], preferred_element_type=F32) + bg_ref[...])
    mix = (gate[:, :D_MODEL] * pa + gate[:, D_MODEL:] * pb).astype(BF16)
    y = jnp.dot(mix, wo_ref[...], preferred_element_type=F32)
    r = DN_ALPHA * x + y
    mu = jnp.mean(r, axis=-1, keepdims=True)
    rc = r - mu
    var = jnp.mean(rc * rc, axis=-1, keepdims=True)
    o_ref[...] = rc * lax.rsqrt(var + LN_EPS) * lng_ref[...] + lnb_ref[...]


def _out_block(x2, ya2, yb2, wz, wg, bg, wpa, wpb, wo, lng, lnb, tm):
    n, d = x2.shape
    const = lambda a: pl.BlockSpec(a.shape, lambda i: (0, 0), pipeline_mode=pl.Buffered(1))
    row = lambda w: pl.BlockSpec((tm, w), lambda i: (i, 0))
    return pl.pallas_call(
        _out_kernel,
        grid=(n // tm,),
        in_specs=[row(d), row(Q_WIDTH), row(Q_WIDTH),
                  const(wz), const(wg), const(bg), const(wpa), const(wpb), const(wo), const(lng), const(lnb)],
        out_specs=row(d),
        out_shape=jax.ShapeDtypeStruct((n, d), x2.dtype),
        compiler_params=pltpu.CompilerParams(
            dimension_semantics=("parallel",), vmem_limit_bytes=VMEM_LIMIT),
        name="out_block",
    )(x2, ya2, yb2, wz, wg, bg, wpa, wpb, wo, lng, lnb)


def _rope_tables(seq):
    half = HEAD_DIM // 2
    axis_pairs = half // 2
    pos = jnp.arange(seq)
    row_idx = (pos // GRID_W).astype(F32)
    col_idx = (pos % GRID_W).astype(F32)
    freqs = ROPE_THETA ** (-jnp.arange(axis_pairs, dtype=F32) / axis_pairs)
    ang = jnp.concatenate([row_idx[:, None] * freqs, col_idx[:, None] * freqs], axis=-1)
    cos, sin = jnp.cos(ang), jnp.sin(ang)
    cosk = jnp.tile(cos, (1, KV_WIDTH // half))
    sink = jnp.tile(jnp.concatenate([-sin, sin], axis=-1), (1, KV_HEADS))
    return cos.T, sin.T, cosk, sink


def _layer(x, w_in, b_gate, sink_a, qnorm_b, knorm_b, w_proj_a, w_proj_b, w_out, ln_g, ln_b):
    bsz, seq, d = x.shape
    tm_qkv, tq, tm_out = 512, 256, 512

    wt = jnp.concatenate([_cols(w_in, i) for i in (QA, VA, QB, VB)], axis=1).T.astype(BF16)
    wk = jnp.concatenate([_cols(w_in, KA), _cols(w_in, KB)], axis=1).astype(BF16)
    wz = jnp.concatenate([_cols(w_in, ZA), _cols(w_in, ZB)], axis=1).astype(BF16)
    wg = jnp.concatenate([_cols(w_in, GA), _cols(w_in, GB)], axis=1).astype(BF16)
    gq = jnp.broadcast_to(qnorm_b.astype(F32)[:, None], (HEAD_DIM, tm_qkv))
    gk = jnp.tile(knorm_b.astype(F32), KV_HEADS)[None, :]
    cosT, sinT, cosk, sink = _rope_tables(seq)

    qaT, ka, vaT, qbT, kb, vbT = _qkv_proj(x, wt, wk, gq, gk, cosT, sinT, cosk, sink, tm_qkv)
    ya = _attn_window(sink_a.astype(F32), qaT, ka, vaT)
    yb = _attn_global(qbT, kb, vbT, tq, seq)

    out = _out_block(
        x.reshape(bsz * seq, d), ya.reshape(bsz * seq, Q_WIDTH), yb.reshape(bsz * seq, Q_WIDTH),
        wz, wg, b_gate.astype(F32)[None, :], w_proj_a.astype(BF16), w_proj_b.astype(BF16), w_out.astype(BF16),
        ln_g.astype(F32)[None, :], ln_b.astype(F32)[None, :], tm_out)
    return out.reshape(bsz, seq, d)


def kernel(x, w_in, b_gate, sink_a, qnorm_b, knorm_b, w_proj_a, w_proj_b, w_out, ln_g, ln_b):
    h = x
    for layer in range(DEPTH):
        h = _layer(h, w_in[layer], b_gate[layer], sink_a[layer], qnorm_b[layer], knorm_b[layer],
                   w_proj_a[layer], w_proj_b[layer], w_out[layer], ln_g[layer], ln_b[layer])
    return h
```

```python
import functools

import jax
import jax.numpy as jnp
import numpy as np
from jax import lax
from jax.experimental import pallas as pl
from jax.experimental.pallas import tpu as pltpu

D_MODEL = 1024
HEAD_DIM = 64
HEADS = 8
KV_HEADS = 2
GROUP = HEADS // KV_HEADS
Q_WIDTH = HEADS * HEAD_DIM
KV_WIDTH = KV_HEADS * HEAD_DIM
WINDOW = 128
BLOCK = 128
GRID_W = 64
ROPE_THETA = 10000.0
QK_EPS = 1e-6
LN_EPS = 1e-5
DEPTH = 1
DN_ALPHA = (2.0 * DEPTH) ** 0.25
SCALE = HEAD_DIM ** -0.5
LOG2E = 1.4426950408889634

_SPLITS = (Q_WIDTH, KV_WIDTH, KV_WIDTH, Q_WIDTH, Q_WIDTH, KV_WIDTH, KV_WIDTH, Q_WIDTH, D_MODEL, D_MODEL)
_OFF = tuple(int(v) for v in np.cumsum((0,) + _SPLITS))
(QA, KA, VA, ZA, QB, KB, VB, ZB, GA, GB) = range(10)

V_AUG = HEAD_DIM + 16
VMEM_LIMIT = 56 * 1024 * 1024

BF16 = jnp.bfloat16
F32 = jnp.float32


def _cols(w, i):
    return w[:, _OFF[i]:_OFF[i + 1]]


def _qkv_kernel(x_ref, wt_ref, wk_ref, gq_ref, gk_ref, cosT_ref, sinT_ref, cosk_ref, sink_ref,
                qaT_ref, ka_ref, vaT_ref, qbT_ref, kb_ref, vbT_ref):
    tm = x_ref.shape[1]
    xb = x_ref[0].astype(BF16)
    t = lax.dot_general(wt_ref[...], xb, (((1,), (1,)), ((), ())), preferred_element_type=F32)
    kk = jnp.dot(xb, wk_ref[...], preferred_element_type=F32)

    ones = jnp.ones((V_AUG - HEAD_DIM, tm), BF16)

    def store_v(vT_ref, vT):
        for g in range(KV_HEADS):
            vT_ref[0, g, 0:HEAD_DIM, :] = vT[g * HEAD_DIM:(g + 1) * HEAD_DIM].astype(BF16)
            vT_ref[0, g, HEAD_DIM:V_AUG, :] = ones

    r0 = 0
    qaT_ref[0] = (t[r0:r0 + Q_WIDTH] * (SCALE * LOG2E)).astype(BF16)
    r0 += Q_WIDTH
    store_v(vaT_ref, t[r0:r0 + KV_WIDTH])
    r0 += KV_WIDTH
    qb = t[r0:r0 + Q_WIDTH].reshape(HEADS, HEAD_DIM, tm)
    r0 += Q_WIDTH
    store_v(vbT_ref, t[r0:r0 + KV_WIDTH])

    half = HEAD_DIM // 2
    ms = jnp.mean(qb * qb, axis=1, keepdims=True)
    qn = qb * lax.rsqrt(ms + QK_EPS) * gq_ref[...][None]
    q1, q2 = qn[:, :half], qn[:, half:]
    cosT, sinT = cosT_ref[...][None], sinT_ref[...][None]
    qr = jnp.concatenate([q1 * cosT - q2 * sinT, q2 * cosT + q1 * sinT], axis=1)
    qbT_ref[0] = (qr * (SCALE * LOG2E)).reshape(Q_WIDTH, tm).astype(BF16)

    ka_ref[0] = kk[:, :KV_WIDTH].astype(BF16)

    kb = kk[:, KV_WIDTH:]
    lane = lax.broadcasted_iota(jnp.int32, kb.shape, 1)
    k2 = kb * kb
    first = lane < HEAD_DIM
    ms0 = jnp.sum(jnp.where(first, k2, 0.0), axis=1, keepdims=True) * (1.0 / HEAD_DIM)
    ms1 = jnp.sum(jnp.where(first, 0.0, k2), axis=1, keepdims=True) * (1.0 / HEAD_DIM)
    kn = kb * lax.rsqrt(jnp.where(first, ms0, ms1) + QK_EPS) * gk_ref[...]
    swapped = jnp.where(lane % HEAD_DIM < half,
                        pltpu.roll(kn, KV_WIDTH - half, axis=1),
                        pltpu.roll(kn, half, axis=1))
    kb_ref[0] = (kn * cosk_ref[...] + swapped * sink_ref[...]).astype(BF16)


def _qkv_proj(x, wt, wk, gq, gk, cosT, sinT, cosk, sink, tm):
    bsz, seq, d = x.shape
    nt = seq // tm
    const = lambda shape: pl.BlockSpec(shape, lambda b, i: (0,) * len(shape))
    qT_spec = pl.BlockSpec((1, Q_WIDTH, tm), lambda b, i: (b, 0, i))
    k_spec = pl.BlockSpec((1, tm, KV_WIDTH), lambda b, i: (b, i, 0))
    vT_spec = pl.BlockSpec((1, KV_HEADS, V_AUG, tm), lambda b, i: (b, 0, 0, i))
    qT_shape = jax.ShapeDtypeStruct((bsz, Q_WIDTH, seq), BF16)
    k_shape = jax.ShapeDtypeStruct((bsz, seq, KV_WIDTH), BF16)
    vT_shape = jax.ShapeDtypeStruct((bsz, KV_HEADS, V_AUG, seq), BF16)
    return pl.pallas_call(
        _qkv_kernel,
        grid=(bsz, nt),
        in_specs=[
            pl.BlockSpec((1, tm, d), lambda b, i: (b, i, 0)),
            const(wt.shape), const(wk.shape), const(gq.shape), const(gk.shape),
            pl.BlockSpec((HEAD_DIM // 2, tm), lambda b, i: (0, i)),
            pl.BlockSpec((HEAD_DIM // 2, tm), lambda b, i: (0, i)),
            pl.BlockSpec((tm, KV_WIDTH), lambda b, i: (i, 0)),
            pl.BlockSpec((tm, KV_WIDTH), lambda b, i: (i, 0)),
        ],
        out_specs=[qT_spec, k_spec, vT_spec, qT_spec, k_spec, vT_spec],
        out_shape=[qT_shape, k_shape, vT_shape, qT_shape, k_shape, vT_shape],
        compiler_params=pltpu.CompilerParams(
            dimension_semantics=("parallel", "parallel"), vmem_limit_bytes=VMEM_LIMIT),
        name="qkv_proj",
    )(x, wt, wk, gq, gk, cosT, sinT, cosk, sink)


def _padded_q(qT_h, g):
    zeros = jnp.zeros_like(qT_h)
    parts = [zeros] * KV_HEADS
    parts[g] = qT_h
    return jnp.concatenate(parts, axis=0)


def _fold_max(s):
    rows, n = s.shape
    while rows % 16 == 0:
        rows //= 2
        s = jnp.maximum(s[:rows], s[rows:])
    return s if rows == 8 else jnp.max(s.reshape(rows // 8, 8, n), axis=0)


def _normalise(oT, extra=None):
    denom = oT[HEAD_DIM:HEAD_DIM + 1]
    if extra is not None:
        denom = denom + extra
    return oT[:HEAD_DIM] * (1.0 / denom)


def _attn_window_kernel(sink_ref, qT_ref, k_ref, vT_ref, o_ref, bias_ref, s_ref):
    seq = k_ref.shape[1]
    nblk = seq // BLOCK
    span = 3 * BLOCK
    gn = GROUP * BLOCK

    r = lax.broadcasted_iota(jnp.int32, (span + 2 * BLOCK, BLOCK), 0)
    c = lax.broadcasted_iota(jnp.int32, (span + 2 * BLOCK, BLOCK), 1)
    dist = jnp.abs(r - 2 * BLOCK - c)
    distf = dist.astype(F32)
    for h in range(HEADS):
        slope = 2.0 ** (-8.0 * (h + 1) / HEADS) * LOG2E
        bias_ref[h] = jnp.where(dist <= WINDOW, -slope * distf, -jnp.inf)

    def sink_row(g):
        return jnp.concatenate(
            [jnp.full((1, BLOCK), sink_ref[g * GROUP + j] * LOG2E, F32) for j in range(GROUP)], axis=1)

    def window(n):
        if isinstance(n, int):
            q_lo, ws = n * BLOCK, min(max((n - 1) * BLOCK, 0), seq - span)
            return q_lo, ws, 2 * BLOCK - (q_lo - ws)
        q_lo = pl.multiple_of(n * BLOCK, BLOCK)
        ws = pl.multiple_of(jnp.clip((n - 1) * BLOCK, 0, seq - span), BLOCK)
        return q_lo, ws, pl.multiple_of(2 * BLOCK - (q_lo - ws), BLOCK)

    def scores_group(n, slot, g):
        q_lo, ws, b_lo = window(n)
        qT = qT_ref[0, g * GROUP * HEAD_DIM:(g + 1) * GROUP * HEAD_DIM, pl.ds(q_lo, BLOCK)]
        w = jnp.concatenate(
            [_padded_q(qT[j * HEAD_DIM:(j + 1) * HEAD_DIM], g) for j in range(GROUP)], axis=1)
        sT = jnp.dot(k_ref[0, pl.ds(ws, span), :], w, preferred_element_type=F32)
        ms = []
        for j in range(GROUP):
            s = sT[:, j * BLOCK:(j + 1) * BLOCK] + bias_ref[g * GROUP + j, pl.ds(b_lo, span), :]
            s_ref[slot, :, g * gn + j * BLOCK:g * gn + (j + 1) * BLOCK] = s
            ms.append(jnp.max(_fold_max(s), axis=0, keepdims=True))
        return jnp.maximum(jnp.concatenate(ms, axis=1), sink_row(g))

    def values_group(n, slot, g, m):
        _, ws, _ = window(n)
        pT = jnp.exp2(s_ref[slot, :, g * gn:(g + 1) * gn] - m).astype(BF16)
        oT = jnp.dot(vT_ref[0, g, :, pl.ds(ws, span)], pT, preferred_element_type=F32)
        o = _normalise(oT, jnp.exp2(sink_row(g) - m))
        return [o[:, j * BLOCK:(j + 1) * BLOCK] for j in range(GROUP)]

    def step(n, slot, m_prev, do_scores=True, do_values=True):
        ms, outs = [], []
        for g in range(KV_HEADS):
            if do_scores:
                ms.append(scores_group(n, slot, g))
            if do_values:
                outs += values_group(n - 1, 1 - slot, g, m_prev[g])
        if do_values:
            q_lo, _, _ = window(n - 1)
            o_ref[0, pl.ds(q_lo, BLOCK), :] = jnp.concatenate(outs, axis=0).T.astype(o_ref.dtype)
        return tuple(ms) if do_scores else None

    def pair(u, m):
        return step(2 * u + 2, 0, step(2 * u + 1, 1, m))

    m = step(0, 0, None, do_values=False)
    m = lax.fori_loop(0, nblk // 2 - 1, pair, m)
    m = step(nblk - 1, 1, m)
    step(nblk, 0, m, do_scores=False)


def _attn_window(sink, qT, k, vT):
    bsz, _, seq = qT.shape
    return pl.pallas_call(
        _attn_window_kernel,
        grid=(bsz,),
        in_specs=[
            pl.BlockSpec(memory_space=pltpu.SMEM),
            pl.BlockSpec((1, Q_WIDTH, seq), lambda b: (b, 0, 0)),
            pl.BlockSpec((1, seq, KV_WIDTH), lambda b: (b, 0, 0)),
            pl.BlockSpec((1, KV_HEADS, V_AUG, seq), lambda b: (b, 0, 0, 0)),
        ],
        out_specs=pl.BlockSpec((1, seq, Q_WIDTH), lambda b: (b, 0, 0)),
        out_shape=jax.ShapeDtypeStruct((bsz, seq, Q_WIDTH), BF16),
        scratch_shapes=[pltpu.VMEM((HEADS, 5 * BLOCK, BLOCK), F32),
                        pltpu.VMEM((2, 3 * BLOCK, HEADS * BLOCK), F32)],
        compiler_params=pltpu.CompilerParams(
            dimension_semantics=("parallel",), vmem_limit_bytes=VMEM_LIMIT),
        name="attn_window",
    )(sink, qT, k, vT)


def _attn_global_kernel(qT_ref, k_ref, vT_ref, o_ref, s_ref, oT_ref, *, tq, kc):
    seq = k_ref.shape[1]
    gw = GROUP * HEAD_DIM
    n_items = (seq // tq) * HEADS
    chunks = range(0, seq, kc)

    def item(i):
        t, h = i // HEADS, i % HEADS
        return t, h // GROUP, h % GROUP, h

    def padded_q(i):
        t, g, _, h = item(i)
        qT_h = qT_ref[0, pl.ds(pl.multiple_of(h * HEAD_DIM, HEAD_DIM), HEAD_DIM),
                      pl.ds(pl.multiple_of(t * tq, tq), tq)]
        return jnp.where(g == 0, _padded_q(qT_h, 0), _padded_q(qT_h, 1))

    def scores_chunk(slot, w, k_lo, m8):
        sT = jnp.dot(k_ref[0, k_lo:k_lo + kc, :], w, preferred_element_type=F32)
        s_ref[slot, k_lo:k_lo + kc, :] = sT
        m8_c = _fold_max(sT)
        return m8_c if m8 is None else jnp.maximum(m8, m8_c)

    def values_chunk(i, slot, m, k_lo, acc):
        _, g, _, _ = item(i)
        pT = jnp.exp2(s_ref[slot, k_lo:k_lo + kc, :] - m).astype(BF16)
        oT = jnp.dot(vT_ref[0, g, :, k_lo:k_lo + kc], pT, preferred_element_type=F32)
        return oT if acc is None else acc + oT

    def finish(i, acc):
        t, g, j, _ = item(i)
        oT_ref[pl.ds(pl.multiple_of(j * HEAD_DIM, HEAD_DIM), HEAD_DIM), :] = _normalise(acc)

        @pl.when(j == GROUP - 1)
        def _():
            o_ref[0, pl.ds(pl.multiple_of(t * tq, tq), tq), pl.ds(pl.multiple_of(g * gw, gw), gw)] = (
                oT_ref[...].T.astype(o_ref.dtype))

    def step(i, slot, m_prev, do_scores=True, do_values=True):
        w = padded_q(i) if do_scores else None
        m8 = acc = None
        for k_lo in chunks:
            if do_scores:
                m8 = scores_chunk(slot, w, k_lo, m8)
            if do_values:
                acc = values_chunk(i - 1, 1 - slot, m_prev, k_lo, acc)
        if do_values:
            finish(i - 1, acc)
        return jnp.max(m8, axis=0, keepdims=True) if do_scores else None

    def pair(u, m):
        return step(2 * u + 2, 0, step(2 * u + 1, 1, m))

    m = step(0, 0, None, do_values=False)
    m = lax.fori_loop(0, n_items // 2 - 1, pair, m)
    m = step(n_items - 1, 1, m)
    step(n_items, 0, m, do_scores=False)


def _attn_global(qT, k, vT, tq, kc):
    bsz, _, seq = qT.shape
    return pl.pallas_call(
        functools.partial(_attn_global_kernel, tq=tq, kc=kc),
        grid=(bsz,),
        in_specs=[
            pl.BlockSpec((1, Q_WIDTH, seq), lambda b: (b, 0, 0)),
            pl.BlockSpec((1, seq, KV_WIDTH), lambda b: (b, 0, 0)),
            pl.BlockSpec((1, KV_HEADS, V_AUG, seq), lambda b: (b, 0, 0, 0)),
        ],
        out_specs=pl.BlockSpec((1, seq, Q_WIDTH), lambda b: (b, 0, 0)),
        out_shape=jax.ShapeDtypeStruct((bsz, seq, Q_WIDTH), BF16),
        scratch_shapes=[pltpu.VMEM((2, seq, tq), F32), pltpu.VMEM((GROUP * HEAD_DIM, tq), F32)],
        compiler_params=pltpu.CompilerParams(
            dimension_semantics=("parallel",), vmem_limit_bytes=VMEM_LIMIT),
        name="attn_global",
    )(qT, k, vT)


def _out_kernel(x_ref, ya_ref, yb_ref, wz_ref, wg_ref, bg_ref, wpa_ref, wpb_ref, wo_ref, lng_ref, lnb_ref,
                o_ref):
    x = x_ref[...]
    xb = x.astype(BF16)
    z = jnp.dot(xb, wz_ref[...], preferred_element_type=F32)
    sz = z * jax.nn.sigmoid(z)
    ua = (ya_ref[...].astype(F32) * sz[:, :Q_WIDTH]).astype(BF16)
    ub = (yb_ref[...].astype(F32) * sz[:, Q_WIDTH:]).astype(BF16)
    pa = jnp.dot(ua, wpa_ref[...], preferred_element_type=F32)
    pb = jnp.dot(ub, wpb_ref[...], preferred_element_type=F32)
    gate = jax.nn.sigmoid(jnp.dot(xb, wg_ref[...], preferred_element_type=F32) + bg_ref[...])
    mix = (gate[:, :D_MODEL] * pa + gate[:, D_MODEL:] * pb).astype(BF16)
    y = jnp.dot(mix, wo_ref[...], preferred_element_type=F32)
    r = DN_ALPHA * x + y
    mu = jnp.mean(r, axis=-1, keepdims=True)
    rc = r - mu
    var = jnp.mean(rc * rc, axis=-1, keepdims=True)
    o_ref[...] = rc * lax.rsqrt(var + LN_EPS) * lng_ref[...] + lnb_ref[...]


def _out_block(x2, ya2, yb2, wz, wg, bg, wpa, wpb, wo, lng, lnb, tm):
    n, d = x2.shape
    const = lambda a: pl.BlockSpec(a.shape, lambda i: (0, 0), pipeline_mode=pl.Buffered(1))
    row = lambda w: pl.BlockSpec((tm, w), lambda i: (i, 0))
    return pl.pallas_call(
        _out_kernel,
        grid=(n // tm,),
        in_specs=[row(d), row(Q_WIDTH), row(Q_WIDTH),
                  const(wz), const(wg), const(bg), const(wpa), const(wpb), const(wo), const(lng), const(lnb)],
        out_specs=row(d),
        out_shape=jax.ShapeDtypeStruct((n, d), x2.dtype),
        compiler_params=pltpu.CompilerParams(
            dimension_semantics=("parallel",), vmem_limit_bytes=VMEM_LIMIT),
        name="out_block",
    )(x2, ya2, yb2, wz, wg, bg, wpa, wpb, wo, lng, lnb)


def _rope_tables(seq):
    half = HEAD_DIM // 2
    axis_pairs = half // 2
    pos = jnp.arange(seq)
    row_idx = (pos // GRID_W).astype(F32)
    col_idx = (pos % GRID_W).astype(F32)
    freqs = ROPE_THETA ** (-jnp.arange(axis_pairs, dtype=F32) / axis_pairs)
    ang = jnp.concatenate([row_idx[:, None] * freqs, col_idx[:, None] * freqs], axis=-1)
    cos, sin = jnp.cos(ang), jnp.sin(ang)
    cosk = jnp.tile(cos, (1, KV_WIDTH // half))
    sink = jnp.tile(jnp.concatenate([-sin, sin], axis=-1), (1, KV_HEADS))
    return cos.T, sin.T, cosk, sink


def _layer(x, w_in, b_gate, sink_a, qnorm_b, knorm_b, w_proj_a, w_proj_b, w_out, ln_g, ln_b):
    bsz, seq, d = x.shape
    tm_qkv, tq, tm_out = 512, 512, 512

    wt = jnp.concatenate([_cols(w_in, i) for i in (QA, VA, QB, VB)], axis=1).T.astype(BF16)
    wk = jnp.concatenate([_cols(w_in, KA), _cols(w_in, KB)], axis=1).astype(BF16)
    wz = jnp.concatenate([_cols(w_in, ZA), _cols(w_in, ZB)], axis=1).astype(BF16)
    wg = jnp.concatenate([_cols(w_in, GA), _cols(w_in, GB)], axis=1).astype(BF16)
    gq = jnp.broadcast_to(qnorm_b.astype(F32)[:, None], (HEAD_DIM, tm_qkv))
    gk = jnp.tile(knorm_b.astype(F32), KV_HEADS)[None, :]
    cosT, sinT, cosk, sink = _rope_tables(seq)

    qaT, ka, vaT, qbT, kb, vbT = _qkv_proj(x, wt, wk, gq, gk, cosT, sinT, cosk, sink, tm_qkv)
    ya = _attn_window(sink_a.astype(F32), qaT, ka, vaT)
    yb = _attn_global(qbT, kb, vbT, tq, 256)

    out = _out_block(
        x.reshape(bsz * seq, d), ya.reshape(bsz * seq, Q_WIDTH), yb.reshape(bsz * seq, Q_WIDTH),
        wz, wg, b_gate.astype(F32)[None, :], w_proj_a.astype(BF16), w_proj_b.astype(BF16), w_out.astype(BF16),
        ln_g.astype(F32)[None, :], ln_b.astype(F32)[None, :], tm_out)
    return out.reshape(bsz, seq, d)


def kernel(x, w_in, b_gate, sink_a, qnorm_b, knorm_b, w_proj_a, w_proj_b, w_out, ln_g, ln_b):
    h = x
    for layer in range(DEPTH):
        h = _layer(h, w_in[layer], b_gate[layer], sink_a[layer], qnorm_b[layer], knorm_b[layer],
                   w_proj_a[layer], w_proj_b[layer], w_out[layer], ln_g[layer], ln_b[layer])
    return h
```

```python
import functools

import jax
import jax.numpy as jnp
import numpy as np
from jax import lax
from jax.experimental import pallas as pl
from jax.experimental.pallas import tpu as pltpu

D_MODEL = 1024
HEAD_DIM = 64
HEADS = 8
KV_HEADS = 2
GROUP = HEADS // KV_HEADS
Q_WIDTH = HEADS * HEAD_DIM
KV_WIDTH = KV_HEADS * HEAD_DIM
WINDOW = 128
BLOCK = 128
GRID_W = 64
ROPE_THETA = 10000.0
QK_EPS = 1e-6
LN_EPS = 1e-5
DEPTH = 1
DN_ALPHA = (2.0 * DEPTH) ** 0.25
SCALE = HEAD_DIM ** -0.5
LOG2E = 1.4426950408889634

_SPLITS = (Q_WIDTH, KV_WIDTH, KV_WIDTH, Q_WIDTH, Q_WIDTH, KV_WIDTH, KV_WIDTH, Q_WIDTH, D_MODEL, D_MODEL)
_OFF = tuple(int(v) for v in np.cumsum((0,) + _SPLITS))
(QA, KA, VA, ZA, QB, KB, VB, ZB, GA, GB) = range(10)

V_AUG = HEAD_DIM + 16
VMEM_LIMIT = 56 * 1024 * 1024

BF16 = jnp.bfloat16
F32 = jnp.float32


def _cols(w, i):
    return w[:, _OFF[i]:_OFF[i + 1]]


def _qkv_kernel(x_ref, wt_ref, wk_ref, gq_ref, gk_ref, cosT_ref, sinT_ref, cosk_ref, sink_ref,
                qaT_ref, ka_ref, vaT_ref, qbT_ref, kb_ref, vbT_ref):
    sub = gq_ref.shape[1]
    half = HEAD_DIM // 2
    ones = jnp.ones((V_AUG - HEAD_DIM, sub), BF16)

    for lo in range(0, x_ref.shape[1], sub):
        tok = slice(lo, lo + sub)
        xb = x_ref[0, tok, :].astype(BF16)
        t = lax.dot_general(wt_ref[...], xb, (((1,), (1,)), ((), ())), preferred_element_type=F32)
        kk = jnp.dot(xb, wk_ref[...], preferred_element_type=F32)

        def store_v(vT_ref, vT):
            for g in range(KV_HEADS):
                vT_ref[0, g, 0:HEAD_DIM, tok] = vT[g * HEAD_DIM:(g + 1) * HEAD_DIM].astype(BF16)
                vT_ref[0, g, HEAD_DIM:V_AUG, tok] = ones

        r0 = 0
        qaT_ref[0, :, tok] = (t[r0:r0 + Q_WIDTH] * (SCALE * LOG2E)).astype(BF16)
        r0 += Q_WIDTH
        store_v(vaT_ref, t[r0:r0 + KV_WIDTH])
        r0 += KV_WIDTH
        qb = t[r0:r0 + Q_WIDTH].reshape(HEADS, HEAD_DIM, sub)
        r0 += Q_WIDTH
        store_v(vbT_ref, t[r0:r0 + KV_WIDTH])

        ms = jnp.mean(qb * qb, axis=1, keepdims=True)
        qn = qb * lax.rsqrt(ms + QK_EPS) * gq_ref[...][None]
        q1, q2 = qn[:, :half], qn[:, half:]
        cosT, sinT = cosT_ref[:, tok][None], sinT_ref[:, tok][None]
        qr = jnp.concatenate([q1 * cosT - q2 * sinT, q2 * cosT + q1 * sinT], axis=1)
        qbT_ref[0, :, tok] = (qr * (SCALE * LOG2E)).reshape(Q_WIDTH, sub).astype(BF16)

        ka_ref[0, tok, :] = kk[:, :KV_WIDTH].astype(BF16)

        kb = kk[:, KV_WIDTH:]
        lane = lax.broadcasted_iota(jnp.int32, kb.shape, 1)
        k2 = kb * kb
        first = lane < HEAD_DIM
        ms0 = jnp.sum(jnp.where(first, k2, 0.0), axis=1, keepdims=True) * (1.0 / HEAD_DIM)
        ms1 = jnp.sum(jnp.where(first, 0.0, k2), axis=1, keepdims=True) * (1.0 / HEAD_DIM)
        kn = kb * lax.rsqrt(jnp.where(first, ms0, ms1) + QK_EPS) * gk_ref[...]
        swapped = jnp.where(lane % HEAD_DIM < half,
                            pltpu.roll(kn, KV_WIDTH - half, axis=1),
                            pltpu.roll(kn, half, axis=1))
        kb_ref[0, tok, :] = (kn * cosk_ref[tok, :] + swapped * sink_ref[tok, :]).astype(BF16)


def _qkv_proj(x, wt, wk, gq, gk, cosT, sinT, cosk, sink, tm):
    bsz, seq, d = x.shape
    nt = seq // tm
    const = lambda shape: pl.BlockSpec(shape, lambda b, i: (0,) * len(shape))
    qT_spec = pl.BlockSpec((1, Q_WIDTH, tm), lambda b, i: (b, 0, i))
    k_spec = pl.BlockSpec((1, tm, KV_WIDTH), lambda b, i: (b, i, 0))
    vT_spec = pl.BlockSpec((1, KV_HEADS, V_AUG, tm), lambda b, i: (b, 0, 0, i))
    qT_shape = jax.ShapeDtypeStruct((bsz, Q_WIDTH, seq), BF16)
    k_shape = jax.ShapeDtypeStruct((bsz, seq, KV_WIDTH), BF16)
    vT_shape = jax.ShapeDtypeStruct((bsz, KV_HEADS, V_AUG, seq), BF16)
    return pl.pallas_call(
        _qkv_kernel,
        grid=(bsz, nt),
        in_specs=[
            pl.BlockSpec((1, tm, d), lambda b, i: (b, i, 0)),
            const(wt.shape), const(wk.shape), const(gq.shape), const(gk.shape),
            pl.BlockSpec((HEAD_DIM // 2, tm), lambda b, i: (0, i)),
            pl.BlockSpec((HEAD_DIM // 2, tm), lambda b, i: (0, i)),
            pl.BlockSpec((tm, KV_WIDTH), lambda b, i: (i, 0)),
            pl.BlockSpec((tm, KV_WIDTH), lambda b, i: (i, 0)),
        ],
        out_specs=[qT_spec, k_spec, vT_spec, qT_spec, k_spec, vT_spec],
        out_shape=[qT_shape, k_shape, vT_shape, qT_shape, k_shape, vT_shape],
        compiler_params=pltpu.CompilerParams(
            dimension_semantics=("parallel", "parallel"), vmem_limit_bytes=VMEM_LIMIT),
        name="qkv_proj",
    )(x, wt, wk, gq, gk, cosT, sinT, cosk, sink)


def _padded_q(qT_h, g):
    zeros = jnp.zeros_like(qT_h)
    parts = [zeros] * KV_HEADS
    parts[g] = qT_h
    return jnp.concatenate(parts, axis=0)


def _fold_max(s):
    rows, n = s.shape
    while rows % 16 == 0:
        rows //= 2
        s = jnp.maximum(s[:rows], s[rows:])
    return s if rows == 8 else jnp.max(s.reshape(rows // 8, 8, n), axis=0)


def _normalise(oT, extra=None):
    denom = oT[HEAD_DIM:HEAD_DIM + 1]
    if extra is not None:
        denom = denom + extra
    return oT[:HEAD_DIM] * (1.0 / denom)


def _attn_window_kernel(sink_ref, qT_ref, k_ref, vT_ref, o_ref, bias_ref, s_ref):
    seq = k_ref.shape[1]
    nblk = seq // BLOCK
    span = 3 * BLOCK
    gn = GROUP * BLOCK

    r = lax.broadcasted_iota(jnp.int32, (span + 2 * BLOCK, BLOCK), 0)
    c = lax.broadcasted_iota(jnp.int32, (span + 2 * BLOCK, BLOCK), 1)
    dist = jnp.abs(r - 2 * BLOCK - c)
    distf = dist.astype(F32)
    for h in range(HEADS):
        slope = 2.0 ** (-8.0 * (h + 1) / HEADS) * LOG2E
        bias_ref[h] = jnp.where(dist <= WINDOW, -slope * distf, -jnp.inf)

    def sink_row(g):
        return jnp.concatenate(
            [jnp.full((1, BLOCK), sink_ref[g * GROUP + j] * LOG2E, F32) for j in range(GROUP)], axis=1)

    def window(n):
        if isinstance(n, int):
            q_lo, ws = n * BLOCK, min(max((n - 1) * BLOCK, 0), seq - span)
            return q_lo, ws, 2 * BLOCK - (q_lo - ws)
        q_lo = pl.multiple_of(n * BLOCK, BLOCK)
        ws = pl.multiple_of(jnp.clip((n - 1) * BLOCK, 0, seq - span), BLOCK)
        return q_lo, ws, pl.multiple_of(2 * BLOCK - (q_lo - ws), BLOCK)

    def scores_group(n, slot, g):
        q_lo, ws, b_lo = window(n)
        qT = qT_ref[0, g * GROUP * HEAD_DIM:(g + 1) * GROUP * HEAD_DIM, pl.ds(q_lo, BLOCK)]
        w = jnp.concatenate(
            [_padded_q(qT[j * HEAD_DIM:(j + 1) * HEAD_DIM], g) for j in range(GROUP)], axis=1)
        sT = jnp.dot(k_ref[0, pl.ds(ws, span), :], w, preferred_element_type=F32)
        ms = []
        for j in range(GROUP):
            s = sT[:, j * BLOCK:(j + 1) * BLOCK] + bias_ref[g * GROUP + j, pl.ds(b_lo, span), :]
            s_ref[slot, :, g * gn + j * BLOCK:g * gn + (j + 1) * BLOCK] = s
            ms.append(jnp.max(_fold_max(s), axis=0, keepdims=True))
        return jnp.maximum(jnp.concatenate(ms, axis=1), sink_row(g))

    def values_group(n, slot, g, m):
        _, ws, _ = window(n)
        pT = jnp.exp2(s_ref[slot, :, g * gn:(g + 1) * gn] - m).astype(BF16)
        oT = jnp.dot(vT_ref[0, g, :, pl.ds(ws, span)], pT, preferred_element_type=F32)
        o = _normalise(oT, jnp.exp2(sink_row(g) - m))
        return [o[:, j * BLOCK:(j + 1) * BLOCK] for j in range(GROUP)]

    def step(n, slot, m_prev, do_scores=True, do_values=True):
        ms, outs = [], []
        for g in range(KV_HEADS):
            if do_scores:
                ms.append(scores_group(n, slot, g))
            if do_values:
                outs += values_group(n - 1, 1 - slot, g, m_prev[g])
        if do_values:
            q_lo, _, _ = window(n - 1)
            o_ref[0, pl.ds(q_lo, BLOCK), :] = jnp.concatenate(outs, axis=0).T.astype(o_ref.dtype)
        return tuple(ms) if do_scores else None

    def pair(u, m):
        return step(2 * u + 2, 0, step(2 * u + 1, 1, m))

    m = step(0, 0, None, do_values=False)
    m = lax.fori_loop(0, nblk // 2 - 1, pair, m)
    m = step(nblk - 1, 1, m)
    step(nblk, 0, m, do_scores=False)


def _attn_window(sink, qT, k, vT):
    bsz, _, seq = qT.shape
    return pl.pallas_call(
        _attn_window_kernel,
        grid=(bsz,),
        in_specs=[
            pl.BlockSpec(memory_space=pltpu.SMEM),
            pl.BlockSpec((1, Q_WIDTH, seq), lambda b: (b, 0, 0)),
            pl.BlockSpec((1, seq, KV_WIDTH), lambda b: (b, 0, 0)),
            pl.BlockSpec((1, KV_HEADS, V_AUG, seq), lambda b: (b, 0, 0, 0)),
        ],
        out_specs=pl.BlockSpec((1, seq, Q_WIDTH), lambda b: (b, 0, 0)),
        out_shape=jax.ShapeDtypeStruct((bsz, seq, Q_WIDTH), BF16),
        scratch_shapes=[pltpu.VMEM((HEADS, 5 * BLOCK, BLOCK), F32),
                        pltpu.VMEM((2, 3 * BLOCK, HEADS * BLOCK), F32)],
        compiler_params=pltpu.CompilerParams(
            dimension_semantics=("parallel",), vmem_limit_bytes=VMEM_LIMIT),
        name="attn_window",
    )(sink, qT, k, vT)


def _attn_global_kernel(qT_ref, k_ref, vT_ref, o_ref, s_ref, oT_ref, *, tq, kc):
    seq = k_ref.shape[1]
    gw = GROUP * HEAD_DIM
    n_items = (seq // tq) * HEADS
    chunks = range(0, seq, kc)

    def item(i):
        t, h = i // HEADS, i % HEADS
        return t, h // GROUP, h % GROUP, h

    def padded_q(i):
        t, g, _, h = item(i)
        qT_h = qT_ref[0, pl.ds(pl.multiple_of(h * HEAD_DIM, HEAD_DIM), HEAD_DIM),
                      pl.ds(pl.multiple_of(t * tq, tq), tq)]
        return jnp.where(g == 0, _padded_q(qT_h, 0), _padded_q(qT_h, 1))

    def scores_chunk(slot, w, k_lo, m8):
        sT = jnp.dot(k_ref[0, k_lo:k_lo + kc, :], w, preferred_element_type=F32)
        s_ref[slot, k_lo:k_lo + kc, :] = sT
        m8_c = _fold_max(sT)
        return m8_c if m8 is None else jnp.maximum(m8, m8_c)

    def values_chunk(i, slot, m, k_lo, acc):
        _, g, _, _ = item(i)
        pT = jnp.exp2(s_ref[slot, k_lo:k_lo + kc, :] - m).astype(BF16)
        oT = jnp.dot(vT_ref[0, g, :, k_lo:k_lo + kc], pT, preferred_element_type=F32)
        return oT if acc is None else acc + oT

    def finish(i, acc):
        t, g, j, _ = item(i)
        oT_ref[pl.ds(pl.multiple_of(j * HEAD_DIM, HEAD_DIM), HEAD_DIM), :] = _normalise(acc)

        @pl.when(j == GROUP - 1)
        def _():
            o_ref[0, pl.ds(pl.multiple_of(t * tq, tq), tq), pl.ds(pl.multiple_of(g * gw, gw), gw)] = (
                oT_ref[...].T.astype(o_ref.dtype))

    def step(i, slot, m_prev, do_scores=True, do_values=True):
        w = padded_q(i) if do_scores else None
        m8 = acc = None
        for k_lo in chunks:
            if do_scores:
                m8 = scores_chunk(slot, w, k_lo, m8)
            if do_values:
                acc = values_chunk(i - 1, 1 - slot, m_prev, k_lo, acc)
        if do_values:
            finish(i - 1, acc)
        return jnp.max(m8, axis=0, keepdims=True) if do_scores else None

    def pair(u, m):
        return step(2 * u + 2, 0, step(2 * u + 1, 1, m))

    m = step(0, 0, None, do_values=False)
    m = lax.fori_loop(0, n_items // 2 - 1, pair, m)
    m = step(n_items - 1, 1, m)
    step(n_items, 0, m, do_scores=False)


def _attn_global(qT, k, vT, tq, kc):
    bsz, _, seq = qT.shape
    return pl.pallas_call(
        functools.partial(_attn_global_kernel, tq=tq, kc=kc),
        grid=(bsz,),
        in_specs=[
            pl.BlockSpec((1, Q_WIDTH, seq), lambda b: (b, 0, 0)),
            pl.BlockSpec((1, seq, KV_WIDTH), lambda b: (b, 0, 0)),
            pl.BlockSpec((1, KV_HEADS, V_AUG, seq), lambda b: (b, 0, 0, 0)),
        ],
        out_specs=pl.BlockSpec((1, seq, Q_WIDTH), lambda b: (b, 0, 0)),
        out_shape=jax.ShapeDtypeStruct((bsz, seq, Q_WIDTH), BF16),
        scratch_shapes=[pltpu.VMEM((2, seq, tq), F32), pltpu.VMEM((GROUP * HEAD_DIM, tq), F32)],
        compiler_params=pltpu.CompilerParams(
            dimension_semantics=("parallel",), vmem_limit_bytes=VMEM_LIMIT),
        name="attn_global",
    )(qT, k, vT)


def _out_kernel(x_ref, ya_ref, yb_ref, wz_ref, wg_ref, bg_ref, wpa_ref, wpb_ref, wo_ref, lng_ref, lnb_ref,
                o_ref, *, sub):
    for lo in range(0, x_ref.shape[0], sub):
        rows = slice(lo, lo + sub)
        x = x_ref[rows, :]
        xb = x.astype(BF16)
        z = jnp.dot(xb, wz_ref[...], preferred_element_type=F32)
        gl = jnp.dot(xb, wg_ref[...], preferred_element_type=F32)
        sz = z * jax.nn.sigmoid(z)
        ua = (ya_ref[rows, :].astype(F32) * sz[:, :Q_WIDTH]).astype(BF16)
        ub = (yb_ref[rows, :].astype(F32) * sz[:, Q_WIDTH:]).astype(BF16)
        pa = jnp.dot(ua, wpa_ref[...], preferred_element_type=F32)
        pb = jnp.dot(ub, wpb_ref[...], preferred_element_type=F32)
        gate = jax.nn.sigmoid(gl + bg_ref[...])
        mix = (gate[:, :D_MODEL] * pa + gate[:, D_MODEL:] * pb).astype(BF16)
        y = jnp.dot(mix, wo_ref[...], preferred_element_type=F32)
        r = DN_ALPHA * x + y
        mu = jnp.mean(r, axis=-1, keepdims=True)
        rc = r - mu
        var = jnp.mean(rc * rc, axis=-1, keepdims=True)
        o_ref[rows, :] = rc * lax.rsqrt(var + LN_EPS) * lng_ref[...] + lnb_ref[...]


def _out_block(x2, ya2, yb2, wz, wg, bg, wpa, wpb, wo, lng, lnb, tm, sub):
    n, d = x2.shape
    const = lambda a: pl.BlockSpec(a.shape, lambda i: (0, 0), pipeline_mode=pl.Buffered(1))
    row = lambda w: pl.BlockSpec((tm, w), lambda i: (i, 0))
    return pl.pallas_call(
        functools.partial(_out_kernel, sub=sub),
        grid=(n // tm,),
        in_specs=[row(d), row(Q_WIDTH), row(Q_WIDTH),
                  const(wz), const(wg), const(bg), const(wpa), const(wpb), const(wo), const(lng), const(lnb)],
        out_specs=row(d),
        out_shape=jax.ShapeDtypeStruct((n, d), x2.dtype),
        compiler_params=pltpu.CompilerParams(
            dimension_semantics=("parallel",), vmem_limit_bytes=VMEM_LIMIT),
        name="out_block",
    )(x2, ya2, yb2, wz, wg, bg, wpa, wpb, wo, lng, lnb)


def _rope_tables(seq):
    half = HEAD_DIM // 2
    axis_pairs = half // 2
    pos = jnp.arange(seq)
    row_idx = (pos // GRID_W).astype(F32)
    col_idx = (pos % GRID_W).astype(F32)
    freqs = ROPE_THETA ** (-jnp.arange(axis_pairs, dtype=F32) / axis_pairs)
    ang = jnp.concatenate([row_idx[:, None] * freqs, col_idx[:, None] * freqs], axis=-1)
    cos, sin = jnp.cos(ang), jnp.sin(ang)
    cosk = jnp.tile(cos, (1, KV_WIDTH // half))
    sink = jnp.tile(jnp.concatenate([-sin, sin], axis=-1), (1, KV_HEADS))
    return cos.T, sin.T, cosk, sink


def _layer(x, w_in, b_gate, sink_a, qnorm_b, knorm_b, w_proj_a, w_proj_b, w_out, ln_g, ln_b):
    bsz, seq, d = x.shape
    tm_qkv, sub_qkv, tq, tm_out = 1024, 256, 512, 1024

    wt = jnp.concatenate([_cols(w_in, i) for i in (QA, VA, QB, VB)], axis=1).T.astype(BF16)
    wk = jnp.concatenate([_cols(w_in, KA), _cols(w_in, KB)], axis=1).astype(BF16)
    wz = jnp.concatenate([_cols(w_in, ZA), _cols(w_in, ZB)], axis=1).astype(BF16)
    wg = jnp.concatenate([_cols(w_in, GA), _cols(w_in, GB)], axis=1).astype(BF16)
    gq = jnp.broadcast_to(qnorm_b.astype(F32)[:, None], (HEAD_DIM, sub_qkv))
    gk = jnp.tile(knorm_b.astype(F32), KV_HEADS)[None, :]
    cosT, sinT, cosk, sink = _rope_tables(seq)

    qaT, ka, vaT, qbT, kb, vbT = _qkv_proj(x, wt, wk, gq, gk, cosT, sinT, cosk, sink, tm_qkv)
    ya = _attn_window(sink_a.astype(F32), qaT, ka, vaT)
    yb = _attn_global(qbT, kb, vbT, tq, 256)

    out = _out_block(
        x.reshape(bsz * seq, d), ya.reshape(bsz * seq, Q_WIDTH), yb.reshape(bsz * seq, Q_WIDTH),
        wz, wg, b_gate.astype(F32)[None, :], w_proj_a.astype(BF16), w_proj_b.astype(BF16), w_out.astype(BF16),
        ln_g.astype(F32)[None, :], ln_b.astype(F32)[None, :], tm_out, 256)
    return out.reshape(bsz, seq, d)


def kernel(x, w_in, b_gate, sink_a, qnorm_b, knorm_b, w_proj_a, w_proj_b, w_out, ln_g, ln_b):
    h = x
    for layer in range(DEPTH):
        h = _layer(h, w_in[layer], b_gate[layer], sink_a[layer], qnorm_b[layer], knorm_b[layer],
                   w_proj_a[layer], w_proj_b[layer], w_out[layer], ln_g[layer], ln_b[layer])
    return h
```

```python
import functools

import jax
import jax.numpy as jnp
import numpy as np
from jax import lax
from jax.experimental import pallas as pl
from jax.experimental.pallas import tpu as pltpu

D_MODEL = 1024
HEAD_DIM = 64
HEADS = 8
KV_HEADS = 2
GROUP = HEADS // KV_HEADS
Q_WIDTH = HEADS * HEAD_DIM
KV_WIDTH = KV_HEADS * HEAD_DIM
WINDOW = 128
BLOCK = 128
GRID_W = 64
ROPE_THETA = 10000.0
QK_EPS = 1e-6
LN_EPS = 1e-5
DEPTH = 1
DN_ALPHA = (2.0 * DEPTH) ** 0.25
SCALE = HEAD_DIM ** -0.5
LOG2E = 1.4426950408889634

_SPLITS = (Q_WIDTH, KV_WIDTH, KV_WIDTH, Q_WIDTH, Q_WIDTH, KV_WIDTH, KV_WIDTH, Q_WIDTH, D_MODEL, D_MODEL)
_OFF = tuple(int(v) for v in np.cumsum((0,) + _SPLITS))
(QA, KA, VA, ZA, QB, KB, VB, ZB, GA, GB) = range(10)

V_AUG = HEAD_DIM + 16
VMEM_LIMIT = 56 * 1024 * 1024

BF16 = jnp.bfloat16
F32 = jnp.float32


def _cols(w, i):
    return w[:, _OFF[i]:_OFF[i + 1]]


def _qkv_kernel(x_ref, wt_ref, wk_ref, gq_ref, gk_ref, cosT_ref, sinT_ref, cosk_ref, sink_ref,
                qaT_ref, ka_ref, vaT_ref, qbT_ref, kb_ref, vbT_ref):
    sub = gq_ref.shape[1]
    half = HEAD_DIM // 2
    ones = jnp.ones((V_AUG - HEAD_DIM, sub), BF16)

    for lo in range(0, x_ref.shape[1], sub):
        tok = slice(lo, lo + sub)
        xb = x_ref[0, tok, :].astype(BF16)
        t = lax.dot_general(wt_ref[...], xb, (((1,), (1,)), ((), ())), preferred_element_type=F32)
        kk = jnp.dot(xb, wk_ref[...], preferred_element_type=F32)

        def store_v(vT_ref, vT):
            for g in range(KV_HEADS):
                vT_ref[0, g, 0:HEAD_DIM, tok] = vT[g * HEAD_DIM:(g + 1) * HEAD_DIM].astype(BF16)
                vT_ref[0, g, HEAD_DIM:V_AUG, tok] = ones

        r0 = 0
        qaT_ref[0, :, tok] = (t[r0:r0 + Q_WIDTH] * (SCALE * LOG2E)).astype(BF16)
        r0 += Q_WIDTH
        store_v(vaT_ref, t[r0:r0 + KV_WIDTH])
        r0 += KV_WIDTH
        qb = t[r0:r0 + Q_WIDTH].reshape(HEADS, HEAD_DIM, sub)
        r0 += Q_WIDTH
        store_v(vbT_ref, t[r0:r0 + KV_WIDTH])

        ms = jnp.mean(qb * qb, axis=1, keepdims=True)
        qn = qb * lax.rsqrt(ms + QK_EPS) * gq_ref[...][None]
        q1, q2 = qn[:, :half], qn[:, half:]
        cosT, sinT = cosT_ref[:, tok][None], sinT_ref[:, tok][None]
        qr = jnp.concatenate([q1 * cosT - q2 * sinT, q2 * cosT + q1 * sinT], axis=1)
        qbT_ref[0, :, tok] = (qr * (SCALE * LOG2E)).reshape(Q_WIDTH, sub).astype(BF16)

        ka_ref[0, tok, :] = kk[:, :KV_WIDTH].astype(BF16)

        kb = kk[:, KV_WIDTH:]
        lane = lax.broadcasted_iota(jnp.int32, kb.shape, 1)
        k2 = kb * kb
        first = lane < HEAD_DIM
        ms0 = jnp.sum(jnp.where(first, k2, 0.0), axis=1, keepdims=True) * (1.0 / HEAD_DIM)
        ms1 = jnp.sum(jnp.where(first, 0.0, k2), axis=1, keepdims=True) * (1.0 / HEAD_DIM)
        kn = kb * lax.rsqrt(jnp.where(first, ms0, ms1) + QK_EPS) * gk_ref[...]
        swapped = jnp.where(lane % HEAD_DIM < half,
                            pltpu.roll(kn, KV_WIDTH - half, axis=1),
                            pltpu.roll(kn, half, axis=1))
        kb_ref[0, tok, :] = (kn * cosk_ref[tok, :] + swapped * sink_ref[tok, :]).astype(BF16)


def _qkv_proj(x, wt, wk, gq, gk, cosT, sinT, cosk, sink, tm):
    bsz, seq, d = x.shape
    nt = seq // tm
    const = lambda shape: pl.BlockSpec(shape, lambda b, i: (0,) * len(shape))
    qT_spec = pl.BlockSpec((1, Q_WIDTH, tm), lambda b, i: (b, 0, i))
    k_spec = pl.BlockSpec((1, tm, KV_WIDTH), lambda b, i: (b, i, 0))
    vT_spec = pl.BlockSpec((1, KV_HEADS, V_AUG, tm), lambda b, i: (b, 0, 0, i))
    qT_shape = jax.ShapeDtypeStruct((bsz, Q_WIDTH, seq), BF16)
    k_shape = jax.ShapeDtypeStruct((bsz, seq, KV_WIDTH), BF16)
    vT_shape = jax.ShapeDtypeStruct((bsz, KV_HEADS, V_AUG, seq), BF16)
    return pl.pallas_call(
        _qkv_kernel,
        grid=(bsz, nt),
        in_specs=[
            pl.BlockSpec((1, tm, d), lambda b, i: (b, i, 0)),
            const(wt.shape), const(wk.shape), const(gq.shape), const(gk.shape),
            pl.BlockSpec((HEAD_DIM // 2, tm), lambda b, i: (0, i)),
            pl.BlockSpec((HEAD_DIM // 2, tm), lambda b, i: (0, i)),
            pl.BlockSpec((tm, KV_WIDTH), lambda b, i: (i, 0)),
            pl.BlockSpec((tm, KV_WIDTH), lambda b, i: (i, 0)),
        ],
        out_specs=[qT_spec, k_spec, vT_spec, qT_spec, k_spec, vT_spec],
        out_shape=[qT_shape, k_shape, vT_shape, qT_shape, k_shape, vT_shape],
        compiler_params=pltpu.CompilerParams(
            dimension_semantics=("parallel", "parallel"), vmem_limit_bytes=VMEM_LIMIT),
        name="qkv_proj",
    )(x, wt, wk, gq, gk, cosT, sinT, cosk, sink)


def _padded_q(qT_h, g):
    zeros = jnp.zeros_like(qT_h)
    parts = [zeros] * KV_HEADS
    parts[g] = qT_h
    return jnp.concatenate(parts, axis=0)


def _aligned(v, m):
    return v if isinstance(v, int) else pl.multiple_of(v, m)


def _fold_max(s):
    rows, n = s.shape
    while rows % 16 == 0:
        rows //= 2
        s = jnp.maximum(s[:rows], s[rows:])
    return s if rows == 8 else jnp.max(s.reshape(rows // 8, 8, n), axis=0)


def _normalise(oT, extra=None):
    denom = oT[HEAD_DIM:HEAD_DIM + 1]
    if extra is not None:
        denom = denom + extra
    return oT[:HEAD_DIM] * (1.0 / denom)


def _attn_window_kernel(sink_ref, qT_ref, k_ref, vT_ref, o_ref, bias_ref, s_ref):
    seq = k_ref.shape[1]
    nblk = seq // BLOCK
    span = 3 * BLOCK
    gn = GROUP * BLOCK

    r = lax.broadcasted_iota(jnp.int32, (span + 2 * BLOCK, BLOCK), 0)
    c = lax.broadcasted_iota(jnp.int32, (span + 2 * BLOCK, BLOCK), 1)
    dist = jnp.abs(r - 2 * BLOCK - c)
    distf = dist.astype(F32)
    for h in range(HEADS):
        slope = 2.0 ** (-8.0 * (h + 1) / HEADS) * LOG2E
        bias_ref[h] = jnp.where(dist <= WINDOW, -slope * distf, -jnp.inf)

    def sink_row(g):
        return jnp.concatenate(
            [jnp.full((1, BLOCK), sink_ref[g * GROUP + j] * LOG2E, F32) for j in range(GROUP)], axis=1)

    def window(n):
        if isinstance(n, int):
            q_lo, ws = n * BLOCK, min(max((n - 1) * BLOCK, 0), seq - span)
            return q_lo, ws, 2 * BLOCK - (q_lo - ws)
        q_lo = pl.multiple_of(n * BLOCK, BLOCK)
        ws = pl.multiple_of(jnp.clip((n - 1) * BLOCK, 0, seq - span), BLOCK)
        return q_lo, ws, pl.multiple_of(2 * BLOCK - (q_lo - ws), BLOCK)

    def scores_group(n, slot, g):
        q_lo, ws, b_lo = window(n)
        qT = qT_ref[0, g * GROUP * HEAD_DIM:(g + 1) * GROUP * HEAD_DIM, pl.ds(q_lo, BLOCK)]
        w = jnp.concatenate(
            [_padded_q(qT[j * HEAD_DIM:(j + 1) * HEAD_DIM], g) for j in range(GROUP)], axis=1)
        sT = jnp.dot(k_ref[0, pl.ds(ws, span), :], w, preferred_element_type=F32)
        ms = []
        for j in range(GROUP):
            s = sT[:, j * BLOCK:(j + 1) * BLOCK] + bias_ref[g * GROUP + j, pl.ds(b_lo, span), :]
            s_ref[slot, :, g * gn + j * BLOCK:g * gn + (j + 1) * BLOCK] = s
            ms.append(jnp.max(_fold_max(s), axis=0, keepdims=True))
        return jnp.maximum(jnp.concatenate(ms, axis=1), sink_row(g))

    def values_group(n, slot, g, m):
        _, ws, _ = window(n)
        pT = jnp.exp2(s_ref[slot, :, g * gn:(g + 1) * gn] - m).astype(BF16)
        oT = jnp.dot(vT_ref[0, g, :, pl.ds(ws, span)], pT, preferred_element_type=F32)
        o = _normalise(oT, jnp.exp2(sink_row(g) - m))
        return [o[:, j * BLOCK:(j + 1) * BLOCK] for j in range(GROUP)]

    def step(n, slot, m_prev, do_scores=True, do_values=True):
        ms, outs = [], []
        for g in range(KV_HEADS):
            if do_scores:
                ms.append(scores_group(n, slot, g))
            if do_values:
                outs += values_group(n - 1, 1 - slot, g, m_prev[g])
        if do_values:
            q_lo, _, _ = window(n - 1)
            o_ref[0, pl.ds(q_lo, BLOCK), :] = jnp.concatenate(outs, axis=0).T.astype(o_ref.dtype)
        return tuple(ms) if do_scores else None

    unroll = 4

    def body(u, m):
        for i in range(1, unroll + 1):
            m = step(unroll * u + i, i % 2, m)
        return m

    m = step(0, 0, None, do_values=False)
    n_loop = (nblk - 1) // unroll
    m = lax.fori_loop(0, n_loop, body, m)
    for n in range(n_loop * unroll + 1, nblk):
        m = step(n, n % 2, m)
    step(nblk, nblk % 2, m, do_scores=False)


def _attn_window(sink, qT, k, vT):
    bsz, _, seq = qT.shape
    return pl.pallas_call(
        _attn_window_kernel,
        grid=(bsz,),
        in_specs=[
            pl.BlockSpec(memory_space=pltpu.SMEM),
            pl.BlockSpec((1, Q_WIDTH, seq), lambda b: (b, 0, 0)),
            pl.BlockSpec((1, seq, KV_WIDTH), lambda b: (b, 0, 0)),
            pl.BlockSpec((1, KV_HEADS, V_AUG, seq), lambda b: (b, 0, 0, 0)),
        ],
        out_specs=pl.BlockSpec((1, seq, Q_WIDTH), lambda b: (b, 0, 0)),
        out_shape=jax.ShapeDtypeStruct((bsz, seq, Q_WIDTH), BF16),
        scratch_shapes=[pltpu.VMEM((HEADS, 5 * BLOCK, BLOCK), F32),
                        pltpu.VMEM((2, 3 * BLOCK, HEADS * BLOCK), F32)],
        compiler_params=pltpu.CompilerParams(
            dimension_semantics=("parallel",), vmem_limit_bytes=VMEM_LIMIT),
        name="attn_window",
    )(sink, qT, k, vT)


def _attn_global_kernel(qT_ref, k_ref, vT_ref, o_ref, s_ref, *, tq, kc):
    seq = k_ref.shape[1]
    gw = GROUP * HEAD_DIM
    n_groups = (seq // tq) * KV_HEADS
    chunks = range(0, seq, kc)

    def padded_q(grp, j):
        t, g = grp // KV_HEADS, grp % KV_HEADS
        qT_h = qT_ref[0, pl.ds(_aligned(g * gw + j * HEAD_DIM, HEAD_DIM), HEAD_DIM),
                      pl.ds(_aligned(t * tq, tq), tq)]
        if isinstance(g, int):
            return _padded_q(qT_h, g)
        return jnp.where(g == 0, _padded_q(qT_h, 0), _padded_q(qT_h, 1))

    def scores_chunk(slot, w, k_lo, m8):
        sT = jnp.dot(k_ref[0, k_lo:k_lo + kc, :], w, preferred_element_type=F32)
        s_ref[slot, k_lo:k_lo + kc, :] = sT
        m8_c = _fold_max(sT)
        return m8_c if m8 is None else jnp.maximum(m8, m8_c)

    def values_chunk(g, slot, m, k_lo, acc):
        pT = jnp.exp2(s_ref[slot, k_lo:k_lo + kc, :] - m).astype(BF16)
        oT = jnp.dot(vT_ref[0, g, :, k_lo:k_lo + kc], pT, preferred_element_type=F32)
        return oT if acc is None else acc + oT

    def scores(grp, j):
        w, m8 = padded_q(grp, j), None
        for k_lo in chunks:
            m8 = scores_chunk(j % 2, w, k_lo, m8)
        return jnp.max(m8, axis=0, keepdims=True)

    def group(grp, m, last=False):
        t, g = grp // KV_HEADS, grp % KV_HEADS
        outs = []
        for j in range(GROUP):
            nxt = (grp, j + 1) if j + 1 < GROUP else (None if last else (grp + 1, 0))
            w = padded_q(*nxt) if nxt else None
            m8 = acc = None
            for k_lo in chunks:
                if nxt:
                    m8 = scores_chunk((j + 1) % 2, w, k_lo, m8)
                acc = values_chunk(g, j % 2, m, k_lo, acc)
            outs.append(_normalise(acc))
            m = jnp.max(m8, axis=0, keepdims=True) if nxt else None
        o_ref[0, pl.ds(_aligned(t * tq, tq), tq), pl.ds(_aligned(g * gw, gw), gw)] = (
            jnp.concatenate(outs, axis=0).T.astype(o_ref.dtype))
        return m

    m = scores(0, 0)
    m = lax.fori_loop(0, n_groups - 1, group, m)
    group(n_groups - 1, m, last=True)


def _attn_global(qT, k, vT, tq, kc):
    bsz, _, seq = qT.shape
    return pl.pallas_call(
        functools.partial(_attn_global_kernel, tq=tq, kc=kc),
        grid=(bsz,),
        in_specs=[
            pl.BlockSpec((1, Q_WIDTH, seq), lambda b: (b, 0, 0)),
            pl.BlockSpec((1, seq, KV_WIDTH), lambda b: (b, 0, 0)),
            pl.BlockSpec((1, KV_HEADS, V_AUG, seq), lambda b: (b, 0, 0, 0)),
        ],
        out_specs=pl.BlockSpec((1, seq, Q_WIDTH), lambda b: (b, 0, 0)),
        out_shape=jax.ShapeDtypeStruct((bsz, seq, Q_WIDTH), BF16),
        scratch_shapes=[pltpu.VMEM((2, seq, tq), F32)],
        compiler_params=pltpu.CompilerParams(
            dimension_semantics=("parallel",), vmem_limit_bytes=VMEM_LIMIT),
        name="attn_global",
    )(qT, k, vT)


def _out_kernel(x_ref, ya_ref, yb_ref, wz_ref, wg_ref, bg_ref, wpa_ref, wpb_ref, wo_ref, lng_ref, lnb_ref,
                o_ref, *, sub):
    for lo in range(0, x_ref.shape[0], sub):
        rows = slice(lo, lo + sub)
        x = x_ref[rows, :]
        xb = x.astype(BF16)
        z = jnp.dot(xb, wz_ref[...], preferred_element_type=F32)
        gl = jnp.dot(xb, wg_ref[...], preferred_element_type=F32)
        sz = z * jax.nn.sigmoid(z)
        ua = (ya_ref[rows, :].astype(F32) * sz[:, :Q_WIDTH]).astype(BF16)
        ub = (yb_ref[rows, :].astype(F32) * sz[:, Q_WIDTH:]).astype(BF16)
        pa = jnp.dot(ua, wpa_ref[...], preferred_element_type=F32)
        pb = jnp.dot(ub, wpb_ref[...], preferred_element_type=F32)
        gate = jax.nn.sigmoid(gl + bg_ref[...])
        mix = (gate[:, :D_MODEL] * pa + gate[:, D_MODEL:] * pb).astype(BF16)
        y = jnp.dot(mix, wo_ref[...], preferred_element_type=F32)
        r = DN_ALPHA * x + y
        mu = jnp.mean(r, axis=-1, keepdims=True)
        rc = r - mu
        var = jnp.mean(rc * rc, axis=-1, keepdims=True)
        o_ref[rows, :] = rc * lax.rsqrt(var + LN_EPS) * lng_ref[...] + lnb_ref[...]


def _out_block(x2, ya2, yb2, wz, wg, bg, wpa, wpb, wo, lng, lnb, tm, sub):
    n, d = x2.shape
    const = lambda a: pl.BlockSpec(a.shape, lambda i: (0, 0), pipeline_mode=pl.Buffered(1))
    row = lambda w: pl.BlockSpec((tm, w), lambda i: (i, 0))
    return pl.pallas_call(
        functools.partial(_out_kernel, sub=sub),
        grid=(n // tm,),
        in_specs=[row(d), row(Q_WIDTH), row(Q_WIDTH),
                  const(wz), const(wg), const(bg), const(wpa), const(wpb), const(wo), const(lng), const(lnb)],
        out_specs=row(d),
        out_shape=jax.ShapeDtypeStruct((n, d), x2.dtype),
        compiler_params=pltpu.CompilerParams(
            dimension_semantics=("parallel",), vmem_limit_bytes=VMEM_LIMIT),
        name="out_block",
    )(x2, ya2, yb2, wz, wg, bg, wpa, wpb, wo, lng, lnb)


def _rope_tables(seq):
    half = HEAD_DIM // 2
    axis_pairs = half // 2
    pos = jnp.arange(seq)
    row_idx = (pos // GRID_W).astype(F32)
    col_idx = (pos % GRID_W).astype(F32)
    freqs = ROPE_THETA ** (-jnp.arange(axis_pairs, dtype=F32) / axis_pairs)
    ang = jnp.concatenate([row_idx[:, None] * freqs, col_idx[:, None] * freqs], axis=-1)
    cos, sin = jnp.cos(ang), jnp.sin(ang)
    cosk = jnp.tile(cos, (1, KV_WIDTH // half))
    sink = jnp.tile(jnp.concatenate([-sin, sin], axis=-1), (1, KV_HEADS))
    return cos.T, sin.T, cosk, sink


def _layer(x, w_in, b_gate, sink_a, qnorm_b, knorm_b, w_proj_a, w_proj_b, w_out, ln_g, ln_b):
    bsz, seq, d = x.shape
    tm_qkv, sub_qkv, tq, tm_out = 1024, 256, 512, 1024

    wt = jnp.concatenate([_cols(w_in, i) for i in (QA, VA, QB, VB)], axis=1).T.astype(BF16)
    wk = jnp.concatenate([_cols(w_in, KA), _cols(w_in, KB)], axis=1).astype(BF16)
    wz = jnp.concatenate([_cols(w_in, ZA), _cols(w_in, ZB)], axis=1).astype(BF16)
    wg = jnp.concatenate([_cols(w_in, GA), _cols(w_in, GB)], axis=1).astype(BF16)
    gq = jnp.broadcast_to(qnorm_b.astype(F32)[:, None], (HEAD_DIM, sub_qkv))
    gk = jnp.tile(knorm_b.astype(F32), KV_HEADS)[None, :]
    cosT, sinT, cosk, sink = _rope_tables(seq)

    qaT, ka, vaT, qbT, kb, vbT = _qkv_proj(x, wt, wk, gq, gk, cosT, sinT, cosk, sink, tm_qkv)
    ya = _attn_window(sink_a.astype(F32), qaT, ka, vaT)
    yb = _attn_global(qbT, kb, vbT, tq, 256)

    out = _out_block(
        x.reshape(bsz * seq, d), ya.reshape(bsz * seq, Q_WIDTH), yb.reshape(bsz * seq, Q_WIDTH),
        wz, wg, b_gate.astype(F32)[None, :], w_proj_a.astype(BF16), w_proj_b.astype(BF16), w_out.astype(BF16),
        ln_g.astype(F32)[None, :], ln_b.astype(F32)[None, :], tm_out, 256)
    return out.reshape(bsz, seq, d)


def kernel(x, w_in, b_gate, sink_a, qnorm_b, knorm_b, w_proj_a, w_proj_b, w_out, ln_g, ln_b):
    h = x
    for layer in range(DEPTH):
        h = _layer(h, w_in[layer], b_gate[layer], sink_a[layer], qnorm_b[layer], knorm_b[layer],
                   w_proj_a[layer], w_proj_b[layer], w_out[layer], ln_g[layer], ln_b[layer])
    return h
```

```python
import functools

import jax
import jax.numpy as jnp
import numpy as np
from jax import lax
from jax.experimental import pallas as pl
from jax.experimental.pallas import tpu as pltpu

D_MODEL = 1024
HEAD_DIM = 64
HEADS = 8
KV_HEADS = 2
GROUP = HEADS // KV_HEADS
Q_WIDTH = HEADS * HEAD_DIM
KV_WIDTH = KV_HEADS * HEAD_DIM
WINDOW = 128
BLOCK = 128
GRID_W = 64
ROPE_THETA = 10000.0
QK_EPS = 1e-6
LN_EPS = 1e-5
DEPTH = 1
DN_ALPHA = (2.0 * DEPTH) ** 0.25
SCALE = HEAD_DIM ** -0.5
LOG2E = 1.4426950408889634

_SPLITS = (Q_WIDTH, KV_WIDTH, KV_WIDTH, Q_WIDTH, Q_WIDTH, KV_WIDTH, KV_WIDTH, Q_WIDTH, D_MODEL, D_MODEL)
_OFF = tuple(int(v) for v in np.cumsum((0,) + _SPLITS))
(QA, KA, VA, ZA, QB, KB, VB, ZB, GA, GB) = range(10)

V_AUG = HEAD_DIM + 16
VMEM_LIMIT = 56 * 1024 * 1024

BF16 = jnp.bfloat16
F32 = jnp.float32


def _prep_kernel(w_ref, wpa_ref, wpb_ref, wo_ref, wt_out, wk_out, wz_out, wg_out, wpa_out, wpb_out, wo_out):
    w = w_ref[0]
    cols = lambda *ids: jnp.concatenate([w[:, _OFF[i]:_OFF[i + 1]] for i in ids], axis=1)
    wt_out[...] = cols(QA, VA, QB, VB).T.astype(BF16)
    wk_out[...] = cols(KA, KB).astype(BF16)
    wz_out[...] = cols(ZA, ZB).astype(BF16)
    wg_out[...] = cols(GA, GB).astype(BF16)
    wpa_out[...] = wpa_ref[0].astype(BF16)
    wpb_out[...] = wpb_ref[0].astype(BF16)
    wo_out[...] = wo_ref[0].astype(BF16)


def _prep_weights(w_in, w_proj_a, w_proj_b, w_out, layer, steps=4):
    _, d, in_width = w_in.shape
    rows = lambda a: pl.BlockSpec((1, a.shape[1] // steps, a.shape[2]), lambda i: (layer, i, 0))
    row_out = lambda n, width: pl.BlockSpec((n // steps, width), lambda i: (i, 0))
    t_width = 2 * (Q_WIDTH + KV_WIDTH)
    shapes = [(t_width, d), (d, 2 * KV_WIDTH), (d, 2 * Q_WIDTH), (d, 2 * D_MODEL),
              w_proj_a.shape[1:], w_proj_b.shape[1:], w_out.shape[1:]]
    return pl.pallas_call(
        _prep_kernel,
        grid=(steps,),
        in_specs=[rows(w_in), rows(w_proj_a), rows(w_proj_b), rows(w_out)],
        out_specs=[pl.BlockSpec((t_width, d // steps), lambda i: (0, i))]
        + [row_out(*s) for s in shapes[1:]],
        out_shape=[jax.ShapeDtypeStruct(s, BF16) for s in shapes],
        compiler_params=pltpu.CompilerParams(
            dimension_semantics=("parallel",), vmem_limit_bytes=VMEM_LIMIT),
        name="prep_weights",
    )(w_in, w_proj_a, w_proj_b, w_out)


def _qkv_kernel(x_ref, wt_ref, wk_ref, gq_ref, gk_ref, cosT_ref, sinT_ref, cosk_ref, sink_ref,
                qaT_ref, ka_ref, vaT_ref, qbT_ref, kb_ref, vbT_ref):
    sub = gq_ref.shape[1]
    half = HEAD_DIM // 2
    ones = jnp.ones((V_AUG - HEAD_DIM, sub), BF16)

    for lo in range(0, x_ref.shape[1], sub):
        tok = slice(lo, lo + sub)
        xb = x_ref[0, tok, :].astype(BF16)
        t = lax.dot_general(wt_ref[...], xb, (((1,), (1,)), ((), ())), preferred_element_type=F32)
        kk = jnp.dot(xb, wk_ref[...], preferred_element_type=F32)

        def store_v(vT_ref, vT):
            for g in range(KV_HEADS):
                vT_ref[0, g, 0:HEAD_DIM, tok] = vT[g * HEAD_DIM:(g + 1) * HEAD_DIM].astype(BF16)
                vT_ref[0, g, HEAD_DIM:V_AUG, tok] = ones

        r0 = 0
        qaT_ref[0, :, tok] = (t[r0:r0 + Q_WIDTH] * (SCALE * LOG2E)).astype(BF16)
        r0 += Q_WIDTH
        store_v(vaT_ref, t[r0:r0 + KV_WIDTH])
        r0 += KV_WIDTH
        qb = t[r0:r0 + Q_WIDTH].reshape(HEADS, HEAD_DIM, sub)
        r0 += Q_WIDTH
        store_v(vbT_ref, t[r0:r0 + KV_WIDTH])

        ms = jnp.mean(qb * qb, axis=1, keepdims=True)
        qn = qb * lax.rsqrt(ms + QK_EPS) * gq_ref[...][None]
        q1, q2 = qn[:, :half], qn[:, half:]
        cosT, sinT = cosT_ref[:, tok][None], sinT_ref[:, tok][None]
        qr = jnp.concatenate([q1 * cosT - q2 * sinT, q2 * cosT + q1 * sinT], axis=1)
        qbT_ref[0, :, tok] = (qr * (SCALE * LOG2E)).reshape(Q_WIDTH, sub).astype(BF16)

        ka_ref[0, tok, :] = kk[:, :KV_WIDTH].astype(BF16)

        kb = kk[:, KV_WIDTH:]
        lane = lax.broadcasted_iota(jnp.int32, kb.shape, 1)
        k2 = kb * kb
        first = lane < HEAD_DIM
        ms0 = jnp.sum(jnp.where(first, k2, 0.0), axis=1, keepdims=True) * (1.0 / HEAD_DIM)
        ms1 = jnp.sum(jnp.where(first, 0.0, k2), axis=1, keepdims=True) * (1.0 / HEAD_DIM)
        kn = kb * lax.rsqrt(jnp.where(first, ms0, ms1) + QK_EPS) * gk_ref[...]
        swapped = jnp.where(lane % HEAD_DIM < half,
                            pltpu.roll(kn, KV_WIDTH - half, axis=1),
                            pltpu.roll(kn, half, axis=1))
        kb_ref[0, tok, :] = (kn * cosk_ref[tok, :] + swapped * sink_ref[tok, :]).astype(BF16)


def _qkv_proj(x, wt, wk, gq, gk, cosT, sinT, cosk, sink, tm):
    bsz, seq, d = x.shape
    nt = seq // tm
    const = lambda shape: pl.BlockSpec(shape, lambda b, i: (0,) * len(shape))
    qT_spec = pl.BlockSpec((1, Q_WIDTH, tm), lambda b, i: (b, 0, i))
    k_spec = pl.BlockSpec((1, tm, KV_WIDTH), lambda b, i: (b, i, 0))
    vT_spec = pl.BlockSpec((1, KV_HEADS, V_AUG, tm), lambda b, i: (b, 0, 0, i))
    qT_shape = jax.ShapeDtypeStruct((bsz, Q_WIDTH, seq), BF16)
    k_shape = jax.ShapeDtypeStruct((bsz, seq, KV_WIDTH), BF16)
    vT_shape = jax.ShapeDtypeStruct((bsz, KV_HEADS, V_AUG, seq), BF16)
    return pl.pallas_call(
        _qkv_kernel,
        grid=(bsz, nt),
        in_specs=[
            pl.BlockSpec((1, tm, d), lambda b, i: (b, i, 0)),
            const(wt.shape), const(wk.shape), const(gq.shape), const(gk.shape),
            pl.BlockSpec((HEAD_DIM // 2, tm), lambda b, i: (0, i)),
            pl.BlockSpec((HEAD_DIM // 2, tm), lambda b, i: (0, i)),
            pl.BlockSpec((tm, KV_WIDTH), lambda b, i: (i, 0)),
            pl.BlockSpec((tm, KV_WIDTH), lambda b, i: (i, 0)),
        ],
        out_specs=[qT_spec, k_spec, vT_spec, qT_spec, k_spec, vT_spec],
        out_shape=[qT_shape, k_shape, vT_shape, qT_shape, k_shape, vT_shape],
        compiler_params=pltpu.CompilerParams(
            dimension_semantics=("parallel", "parallel"), vmem_limit_bytes=VMEM_LIMIT),
        name="qkv_proj",
    )(x, wt, wk, gq, gk, cosT, sinT, cosk, sink)


def _padded_q(qT_h, g):
    zeros = jnp.zeros_like(qT_h)
    parts = [zeros] * KV_HEADS
    parts[g] = qT_h
    return jnp.concatenate(parts, axis=0)


def _aligned(v, m):
    return v if isinstance(v, int) else pl.multiple_of(v, m)


def _fold_max(s):
    rows, n = s.shape
    while rows % 16 == 0:
        rows //= 2
        s = jnp.maximum(s[:rows], s[rows:])
    return s if rows == 8 else jnp.max(s.reshape(rows // 8, 8, n), axis=0)


def _normalise(oT, extra=None):
    denom = oT[HEAD_DIM:HEAD_DIM + 1]
    if extra is not None:
        denom = denom + extra
    return oT[:HEAD_DIM] * (1.0 / denom)


def _attn_window_kernel(sink_ref, qT_ref, k_ref, vT_ref, o_ref, bias_ref, s_ref):
    seq = k_ref.shape[1]
    nblk = seq // BLOCK
    span = 3 * BLOCK
    gn = GROUP * BLOCK

    r = lax.broadcasted_iota(jnp.int32, (span + 2 * BLOCK, BLOCK), 0)
    c = lax.broadcasted_iota(jnp.int32, (span + 2 * BLOCK, BLOCK), 1)
    dist = jnp.abs(r - 2 * BLOCK - c)
    distf = dist.astype(F32)
    for h in range(HEADS):
        slope = 2.0 ** (-8.0 * (h + 1) / HEADS) * LOG2E
        bias_ref[h] = jnp.where(dist <= WINDOW, -slope * distf, -jnp.inf)

    def sink_row(g):
        return jnp.concatenate(
            [jnp.full((1, BLOCK), sink_ref[g * GROUP + j] * LOG2E, F32) for j in range(GROUP)], axis=1)

    def window(n):
        if isinstance(n, int):
            q_lo, ws = n * BLOCK, min(max((n - 1) * BLOCK, 0), seq - span)
            return q_lo, ws, 2 * BLOCK - (q_lo - ws)
        q_lo = pl.multiple_of(n * BLOCK, BLOCK)
        ws = pl.multiple_of(jnp.clip((n - 1) * BLOCK, 0, seq - span), BLOCK)
        return q_lo, ws, pl.multiple_of(2 * BLOCK - (q_lo - ws), BLOCK)

    def scores_group(n, slot, g):
        q_lo, ws, b_lo = window(n)
        qT = qT_ref[0, g * GROUP * HEAD_DIM:(g + 1) * GROUP * HEAD_DIM, pl.ds(q_lo, BLOCK)]
        w = jnp.concatenate(
            [_padded_q(qT[j * HEAD_DIM:(j + 1) * HEAD_DIM], g) for j in range(GROUP)], axis=1)
        sT = jnp.dot(k_ref[0, pl.ds(ws, span), :], w, preferred_element_type=F32)
        ms = []
        for j in range(GROUP):
            s = sT[:, j * BLOCK:(j + 1) * BLOCK] + bias_ref[g * GROUP + j, pl.ds(b_lo, span), :]
            s_ref[slot, :, g * gn + j * BLOCK:g * gn + (j + 1) * BLOCK] = s
            ms.append(jnp.max(_fold_max(s), axis=0, keepdims=True))
        return jnp.maximum(jnp.concatenate(ms, axis=1), sink_row(g))

    def values_group(n, slot, g, m):
        _, ws, _ = window(n)
        pT = jnp.exp2(s_ref[slot, :, g * gn:(g + 1) * gn] - m).astype(BF16)
        oT = jnp.dot(vT_ref[0, g, :, pl.ds(ws, span)], pT, preferred_element_type=F32)
        o = _normalise(oT, jnp.exp2(sink_row(g) - m))
        return [o[:, j * BLOCK:(j + 1) * BLOCK] for j in range(GROUP)]

    def step(n, slot, m_prev, do_scores=True, do_values=True):
        ms, outs = [], []
        for g in range(KV_HEADS):
            if do_scores:
                ms.append(scores_group(n, slot, g))
            if do_values:
                outs += values_group(n - 1, 1 - slot, g, m_prev[g])
        if do_values:
            q_lo, _, _ = window(n - 1)
            o_ref[0, pl.ds(q_lo, BLOCK), :] = jnp.concatenate(outs, axis=0).T.astype(o_ref.dtype)
        return tuple(ms) if do_scores else None

    unroll = 4

    def body(u, m):
        for i in range(1, unroll + 1):
            m = step(unroll * u + i, i % 2, m)
        return m

    m = step(0, 0, None, do_values=False)
    n_loop = (nblk - 1) // unroll
    m = lax.fori_loop(0, n_loop, body, m)
    for n in range(n_loop * unroll + 1, nblk):
        m = step(n, n % 2, m)
    step(nblk, nblk % 2, m, do_scores=False)


def _attn_window(sink, qT, k, vT):
    bsz, _, seq = qT.shape
    return pl.pallas_call(
        _attn_window_kernel,
        grid=(bsz,),
        in_specs=[
            pl.BlockSpec(memory_space=pltpu.SMEM),
            pl.BlockSpec((1, Q_WIDTH, seq), lambda b: (b, 0, 0)),
            pl.BlockSpec((1, seq, KV_WIDTH), lambda b: (b, 0, 0)),
            pl.BlockSpec((1, KV_HEADS, V_AUG, seq), lambda b: (b, 0, 0, 0)),
        ],
        out_specs=pl.BlockSpec((1, seq, Q_WIDTH), lambda b: (b, 0, 0)),
        out_shape=jax.ShapeDtypeStruct((bsz, seq, Q_WIDTH), BF16),
        scratch_shapes=[pltpu.VMEM((HEADS, 5 * BLOCK, BLOCK), F32),
                        pltpu.VMEM((2, 3 * BLOCK, HEADS * BLOCK), F32)],
        compiler_params=pltpu.CompilerParams(
            dimension_semantics=("parallel",), vmem_limit_bytes=VMEM_LIMIT),
        name="attn_window",
    )(sink, qT, k, vT)


def _attn_global_kernel(qT_ref, k_ref, vT_ref, o_ref, s_ref, *, tq, kc):
    seq = k_ref.shape[1]
    gw = GROUP * HEAD_DIM
    n_groups = (seq // tq) * KV_HEADS
    chunks = range(0, seq, kc)

    def padded_q(grp, j):
        t, g = grp // KV_HEADS, grp % KV_HEADS
        qT_h = qT_ref[0, pl.ds(_aligned(g * gw + j * HEAD_DIM, HEAD_DIM), HEAD_DIM),
                      pl.ds(_aligned(t * tq, tq), tq)]
        if isinstance(g, int):
            return _padded_q(qT_h, g)
        return jnp.where(g == 0, _padded_q(qT_h, 0), _padded_q(qT_h, 1))

    def scores_chunk(slot, w, k_lo, m8):
        sT = jnp.dot(k_ref[0, k_lo:k_lo + kc, :], w, preferred_element_type=F32)
        s_ref[slot, k_lo:k_lo + kc, :] = sT
        m8_c = _fold_max(sT)
        return m8_c if m8 is None else jnp.maximum(m8, m8_c)

    def values_chunk(g, slot, m, k_lo, acc):
        pT = jnp.exp2(s_ref[slot, k_lo:k_lo + kc, :] - m).astype(BF16)
        oT = jnp.dot(vT_ref[0, g, :, k_lo:k_lo + kc], pT, preferred_element_type=F32)
        return oT if acc is None else acc + oT

    def scores(grp, j):
        w, m8 = padded_q(grp, j), None
        for k_lo in chunks:
            m8 = scores_chunk(j % 2, w, k_lo, m8)
        return jnp.max(m8, axis=0, keepdims=True)

    def group(grp, m, last=False):
        t, g = grp // KV_HEADS, grp % KV_HEADS
        outs = []
        for j in range(GROUP):
            nxt = (grp, j + 1) if j + 1 < GROUP else (None if last else (grp + 1, 0))
            w = padded_q(*nxt) if nxt else None
            m8 = acc = None
            for k_lo in chunks:
                if nxt:
                    m8 = scores_chunk((j + 1) % 2, w, k_lo, m8)
                acc = values_chunk(g, j % 2, m, k_lo, acc)
            outs.append(_normalise(acc))
            m = jnp.max(m8, axis=0, keepdims=True) if nxt else None
        o_ref[0, pl.ds(_aligned(t * tq, tq), tq), pl.ds(_aligned(g * gw, gw), gw)] = (
            jnp.concatenate(outs, axis=0).T.astype(o_ref.dtype))
        return m

    m = scores(0, 0)
    m = lax.fori_loop(0, n_groups - 1, group, m)
    group(n_groups - 1, m, last=True)


def _attn_global(qT, k, vT, tq, kc):
    bsz, _, seq = qT.shape
    return pl.pallas_call(
        functools.partial(_attn_global_kernel, tq=tq, kc=kc),
        grid=(bsz,),
        in_specs=[
            pl.BlockSpec((1, Q_WIDTH, seq), lambda b: (b, 0, 0)),
            pl.BlockSpec((1, seq, KV_WIDTH), lambda b: (b, 0, 0)),
            pl.BlockSpec((1, KV_HEADS, V_AUG, seq), lambda b: (b, 0, 0, 0)),
        ],
        out_specs=pl.BlockSpec((1, seq, Q_WIDTH), lambda b: (b, 0, 0)),
        out_shape=jax.ShapeDtypeStruct((bsz, seq, Q_WIDTH), BF16),
        scratch_shapes=[pltpu.VMEM((2, seq, tq), F32)],
        compiler_params=pltpu.CompilerParams(
            dimension_semantics=("parallel",), vmem_limit_bytes=VMEM_LIMIT),
        name="attn_global",
    )(qT, k, vT)


def _out_kernel(x_ref, ya_ref, yb_ref, wz_ref, wg_ref, bg_ref, wpa_ref, wpb_ref, wo_ref, lng_ref, lnb_ref,
                o_ref, *, sub):
    for lo in range(0, x_ref.shape[0], sub):
        rows = slice(lo, lo + sub)
        x = x_ref[rows, :]
        xb = x.astype(BF16)
        z = jnp.dot(xb, wz_ref[...], preferred_element_type=F32)
        gl = jnp.dot(xb, wg_ref[...], preferred_element_type=F32)
        sz = z * jax.nn.sigmoid(z)
        ua = (ya_ref[rows, :].astype(F32) * sz[:, :Q_WIDTH]).astype(BF16)
        ub = (yb_ref[rows, :].astype(F32) * sz[:, Q_WIDTH:]).astype(BF16)
        pa = jnp.dot(ua, wpa_ref[...], preferred_element_type=F32)
        pb = jnp.dot(ub, wpb_ref[...], preferred_element_type=F32)
        gate = jax.nn.sigmoid(gl + bg_ref[...])
        mix = (gate[:, :D_MODEL] * pa + gate[:, D_MODEL:] * pb).astype(BF16)
        y = jnp.dot(mix, wo_ref[...], preferred_element_type=F32)
        r = DN_ALPHA * x + y
        mu = jnp.mean(r, axis=-1, keepdims=True)
        rc = r - mu
        var = jnp.mean(rc * rc, axis=-1, keepdims=True)
        o_ref[rows, :] = rc * lax.rsqrt(var + LN_EPS) * lng_ref[...] + lnb_ref[...]


def _out_block(x2, ya2, yb2, wz, wg, bg, wpa, wpb, wo, lng, lnb, tm, sub):
    n, d = x2.shape
    const = lambda a: pl.BlockSpec(a.shape, lambda i: (0, 0), pipeline_mode=pl.Buffered(1))
    row = lambda w: pl.BlockSpec((tm, w), lambda i: (i, 0))
    return pl.pallas_call(
        functools.partial(_out_kernel, sub=sub),
        grid=(n // tm,),
        in_specs=[row(d), row(Q_WIDTH), row(Q_WIDTH),
                  const(wz), const(wg), const(bg), const(wpa), const(wpb), const(wo), const(lng), const(lnb)],
        out_specs=row(d),
        out_shape=jax.ShapeDtypeStruct((n, d), x2.dtype),
        compiler_params=pltpu.CompilerParams(
            dimension_semantics=("parallel",), vmem_limit_bytes=VMEM_LIMIT),
        name="out_block",
    )(x2, ya2, yb2, wz, wg, bg, wpa, wpb, wo, lng, lnb)


def _rope_tables(seq):
    half = HEAD_DIM // 2
    axis_pairs = half // 2
    pos = np.arange(seq)
    freqs = ROPE_THETA ** (-np.arange(axis_pairs, dtype=np.float64) / axis_pairs)
    ang = np.concatenate([(pos // GRID_W)[:, None] * freqs, (pos % GRID_W)[:, None] * freqs], axis=-1)
    cos, sin = np.cos(ang), np.sin(ang)
    cosk = np.tile(cos, (1, KV_WIDTH // half))
    sink = np.tile(np.concatenate([-sin, sin], axis=-1), (1, KV_HEADS))
    return tuple(jnp.asarray(a, F32) for a in (cos.T, sin.T, cosk, sink))


def _layer(x, layer, w_in, b_gate, sink_a, qnorm_b, knorm_b, w_proj_a, w_proj_b, w_out, ln_g, ln_b):
    bsz, seq, d = x.shape
    tm_qkv, sub_qkv, tq, tm_out = 1024, 256, 512, 1024

    wt, wk, wz, wg, wpa, wpb, wo = _prep_weights(w_in, w_proj_a, w_proj_b, w_out, layer)
    gq = jnp.broadcast_to(qnorm_b[layer].astype(F32)[:, None], (HEAD_DIM, sub_qkv))
    gk = jnp.tile(knorm_b[layer].astype(F32), KV_HEADS)[None, :]
    cosT, sinT, cosk, sink = _rope_tables(seq)

    qaT, ka, vaT, qbT, kb, vbT = _qkv_proj(x, wt, wk, gq, gk, cosT, sinT, cosk, sink, tm_qkv)
    ya = _attn_window(sink_a[layer].astype(F32), qaT, ka, vaT)
    yb = _attn_global(qbT, kb, vbT, tq, 256)

    out = _out_block(
        x.reshape(bsz * seq, d), ya.reshape(bsz * seq, Q_WIDTH), yb.reshape(bsz * seq, Q_WIDTH),
        wz, wg, b_gate[layer].astype(F32)[None, :], wpa, wpb, wo,
        ln_g[layer].astype(F32)[None, :], ln_b[layer].astype(F32)[None, :], tm_out, 256)
    return out.reshape(bsz, seq, d)


def kernel(x, w_in, b_gate, sink_a, qnorm_b, knorm_b, w_proj_a, w_proj_b, w_out, ln_g, ln_b):
    h = x
    for layer in range(DEPTH):
        h = _layer(h, layer, w_in, b_gate, sink_a, qnorm_b, knorm_b, w_proj_a, w_proj_b, w_out, ln_g, ln_b)
    return h
```

```python
import functools

import jax
import jax.numpy as jnp
import numpy as np
from jax import lax
from jax.experimental import pallas as pl
from jax.experimental.pallas import tpu as pltpu

D_MODEL = 1024
HEAD_DIM = 64
HEADS = 8
KV_HEADS = 2
GROUP = HEADS // KV_HEADS
Q_WIDTH = HEADS * HEAD_DIM
KV_WIDTH = KV_HEADS * HEAD_DIM
WINDOW = 128
BLOCK = 128
GRID_W = 64
ROPE_THETA = 10000.0
QK_EPS = 1e-6
LN_EPS = 1e-5
DEPTH = 1
DN_ALPHA = (2.0 * DEPTH) ** 0.25
SCALE = HEAD_DIM ** -0.5
LOG2E = 1.4426950408889634

_SPLITS = (Q_WIDTH, KV_WIDTH, KV_WIDTH, Q_WIDTH, Q_WIDTH, KV_WIDTH, KV_WIDTH, Q_WIDTH, D_MODEL, D_MODEL)
_OFF = tuple(int(v) for v in np.cumsum((0,) + _SPLITS))
(QA, KA, VA, ZA, QB, KB, VB, ZB, GA, GB) = range(10)

V_AUG = HEAD_DIM + 16
VMEM_LIMIT = 56 * 1024 * 1024

BF16 = jnp.bfloat16
F32 = jnp.float32


def _cols(w, *ids):
    return jnp.concatenate([w[:, _OFF[i]:_OFF[i + 1]] for i in ids], axis=1)


def _prep_kernel(w_ref, wt_out, wk_out):
    w = w_ref[0]
    wt_out[...] = _cols(w, QA, VA, QB, VB).T.astype(BF16)
    wk_out[...] = _cols(w, KA, KB).astype(BF16)


def _prep_qkv_weights(w_in, layer, steps=4):
    d = w_in.shape[1]
    t_width = 2 * (Q_WIDTH + KV_WIDTH)
    return pl.pallas_call(
        _prep_kernel,
        grid=(steps,),
        in_specs=[pl.BlockSpec((1, d // steps, _OFF[ZB]), lambda i: (layer, i, 0))],
        out_specs=[pl.BlockSpec((t_width, d // steps), lambda i: (0, i)),
                   pl.BlockSpec((d // steps, 2 * KV_WIDTH), lambda i: (i, 0))],
        out_shape=[jax.ShapeDtypeStruct((t_width, d), BF16), jax.ShapeDtypeStruct((d, 2 * KV_WIDTH), BF16)],
        compiler_params=pltpu.CompilerParams(
            dimension_semantics=("parallel",), vmem_limit_bytes=VMEM_LIMIT),
        name="prep_qkv_weights",
    )(w_in)


def _qkv_kernel(x_ref, wt_ref, wk_ref, gq_ref, gk_ref, cosT_ref, sinT_ref, cosk_ref, sink_ref,
                w_ref, wpa_ref, wpb_ref, wo_ref,
                qaT_ref, ka_ref, vaT_ref, qbT_ref, kb_ref, vbT_ref,
                wz_out, wg_out, wpa_out, wpb_out, wo_out, *, sub):
    half = HEAD_DIM // 2
    ones = jnp.ones((V_AUG - HEAD_DIM, sub), BF16)
    gq = gq_ref[...][None]
    gk = jnp.concatenate([gk_ref[...]] * KV_HEADS, axis=1)

    w = w_ref[0]
    wz_out[...] = _cols(w, ZA, ZB).astype(BF16)
    wg_out[...] = _cols(w, GA, GB).astype(BF16)
    wpa_out[...] = wpa_ref[0].astype(BF16)
    wpb_out[...] = wpb_ref[0].astype(BF16)
    wo_out[...] = wo_ref[0].astype(BF16)

    for lo in range(0, x_ref.shape[1], sub):
        tok = slice(lo, lo + sub)
        xb = x_ref[0, tok, :].astype(BF16)
        t = lax.dot_general(wt_ref[...], xb, (((1,), (1,)), ((), ())), preferred_element_type=F32)
        kk = jnp.dot(xb, wk_ref[...], preferred_element_type=F32)

        def store_v(vT_ref, vT):
            for g in range(KV_HEADS):
                vT_ref[0, g, 0:HEAD_DIM, tok] = vT[g * HEAD_DIM:(g + 1) * HEAD_DIM].astype(BF16)
                vT_ref[0, g, HEAD_DIM:V_AUG, tok] = ones

        r0 = 0
        qaT_ref[0, :, tok] = (t[r0:r0 + Q_WIDTH] * (SCALE * LOG2E)).astype(BF16)
        r0 += Q_WIDTH
        store_v(vaT_ref, t[r0:r0 + KV_WIDTH])
        r0 += KV_WIDTH
        qb = t[r0:r0 + Q_WIDTH].reshape(HEADS, HEAD_DIM, sub)
        r0 += Q_WIDTH
        store_v(vbT_ref, t[r0:r0 + KV_WIDTH])

        ms = jnp.mean(qb * qb, axis=1, keepdims=True)
        qn = qb * lax.rsqrt(ms + QK_EPS) * gq
        q1, q2 = qn[:, :half], qn[:, half:]
        cosT, sinT = cosT_ref[:, tok][None], sinT_ref[:, tok][None]
        qr = jnp.concatenate([q1 * cosT - q2 * sinT, q2 * cosT + q1 * sinT], axis=1)
        qbT_ref[0, :, tok] = (qr * (SCALE * LOG2E)).reshape(Q_WIDTH, sub).astype(BF16)

        ka_ref[0, tok, :] = kk[:, :KV_WIDTH].astype(BF16)

        kb = kk[:, KV_WIDTH:]
        lane = lax.broadcasted_iota(jnp.int32, kb.shape, 1)
        k2 = kb * kb
        first = lane < HEAD_DIM
        ms0 = jnp.sum(jnp.where(first, k2, 0.0), axis=1, keepdims=True) * (1.0 / HEAD_DIM)
        ms1 = jnp.sum(jnp.where(first, 0.0, k2), axis=1, keepdims=True) * (1.0 / HEAD_DIM)
        kn = kb * lax.rsqrt(jnp.where(first, ms0, ms1) + QK_EPS) * gk
        swapped = jnp.where(lane % HEAD_DIM < half,
                            pltpu.roll(kn, KV_WIDTH - half, axis=1),
                            pltpu.roll(kn, half, axis=1))
        kb_ref[0, tok, :] = (kn * cosk_ref[tok, :] + swapped * sink_ref[tok, :]).astype(BF16)


def _qkv_proj(x, wt, wk, gq, gk, cosT, sinT, cosk, sink, w_in, w_proj_a, w_proj_b, w_out, layer, tm, sub):
    bsz, seq, d = x.shape
    nt = seq // tm
    steps = bsz * nt
    const = lambda shape: pl.BlockSpec(shape, lambda b, i: (0,) * len(shape))
    qT_spec = pl.BlockSpec((1, Q_WIDTH, tm), lambda b, i: (b, 0, i))
    k_spec = pl.BlockSpec((1, tm, KV_WIDTH), lambda b, i: (b, i, 0))
    vT_spec = pl.BlockSpec((1, KV_HEADS, V_AUG, tm), lambda b, i: (b, 0, 0, i))
    qT_shape = jax.ShapeDtypeStruct((bsz, Q_WIDTH, seq), BF16)
    k_shape = jax.ShapeDtypeStruct((bsz, seq, KV_WIDTH), BF16)
    vT_shape = jax.ShapeDtypeStruct((bsz, KV_HEADS, V_AUG, seq), BF16)
    slab_in = lambda a: pl.BlockSpec((1, a.shape[1] // steps, a.shape[2]), lambda b, i: (layer, b * nt + i, 0))
    slab_out = lambda rows, width: pl.BlockSpec((rows // steps, width), lambda b, i: (b * nt + i, 0))
    w_shapes = [(d, 2 * Q_WIDTH), (d, 2 * D_MODEL), w_proj_a.shape[1:], w_proj_b.shape[1:], w_out.shape[1:]]
    outs = pl.pallas_call(
        functools.partial(_qkv_kernel, sub=sub),
        grid=(bsz, nt),
        in_specs=[
            pl.BlockSpec((1, tm, d), lambda b, i: (b, i, 0)),
            const(wt.shape), const(wk.shape), const(gq.shape), const(gk.shape),
            pl.BlockSpec((HEAD_DIM // 2, tm), lambda b, i: (0, i)),
            pl.BlockSpec((HEAD_DIM // 2, tm), lambda b, i: (0, i)),
            pl.BlockSpec((tm, KV_WIDTH), lambda b, i: (i, 0)),
            pl.BlockSpec((tm, KV_WIDTH), lambda b, i: (i, 0)),
            slab_in(w_in), slab_in(w_proj_a), slab_in(w_proj_b), slab_in(w_out),
        ],
        out_specs=[qT_spec, k_spec, vT_spec, qT_spec, k_spec, vT_spec] + [slab_out(*s) for s in w_shapes],
        out_shape=[qT_shape, k_shape, vT_shape, qT_shape, k_shape, vT_shape]
        + [jax.ShapeDtypeStruct(s, BF16) for s in w_shapes],
        compiler_params=pltpu.CompilerParams(
            dimension_semantics=("parallel", "parallel"), vmem_limit_bytes=VMEM_LIMIT),
        name="qkv_proj",
    )(x, wt, wk, gq, gk, cosT, sinT, cosk, sink, w_in, w_proj_a, w_proj_b, w_out)
    return outs[:6], outs[6:]


def _padded_q(qT_h, g):
    zeros = jnp.zeros_like(qT_h)
    parts = [zeros] * KV_HEADS
    parts[g] = qT_h
    return jnp.concatenate(parts, axis=0)


def _aligned(v, m):
    return v if isinstance(v, int) else pl.multiple_of(v, m)


def _fold_max(s):
    rows, n = s.shape
    while rows % 16 == 0:
        rows //= 2
        s = jnp.maximum(s[:rows], s[rows:])
    return s if rows == 8 else jnp.max(s.reshape(rows // 8, 8, n), axis=0)


def _normalise(oT, extra=None):
    denom = oT[HEAD_DIM:HEAD_DIM + 1]
    if extra is not None:
        denom = denom + extra
    return oT[:HEAD_DIM] * (1.0 / denom)


def _alibi_table():
    r = np.arange(5 * BLOCK)[:, None]
    c = np.arange(BLOCK)[None, :]
    dist = np.abs(r - 2 * BLOCK - c).astype(np.float64)
    slopes = 2.0 ** (-8.0 * (np.arange(HEADS) + 1.0) / HEADS) * LOG2E
    bias = np.where(dist <= WINDOW, -slopes[:, None, None] * dist, -np.inf)
    return jnp.asarray(bias, F32)


def _attn_window_kernel(sink_ref, bias_ref, qT_ref, k_ref, vT_ref, o_ref, s_ref):
    seq = k_ref.shape[1]
    nblk = seq // BLOCK
    span = 3 * BLOCK
    gn = GROUP * BLOCK

    def sink_row(g):
        return jnp.concatenate(
            [jnp.full((1, BLOCK), sink_ref[g * GROUP + j] * LOG2E, F32) for j in range(GROUP)], axis=1)

    def window(n):
        if isinstance(n, int):
            q_lo, ws = n * BLOCK, min(max((n - 1) * BLOCK, 0), seq - span)
            return q_lo, ws, 2 * BLOCK - (q_lo - ws)
        q_lo = pl.multiple_of(n * BLOCK, BLOCK)
        ws = pl.multiple_of(jnp.clip((n - 1) * BLOCK, 0, seq - span), BLOCK)
        return q_lo, ws, pl.multiple_of(2 * BLOCK - (q_lo - ws), BLOCK)

    def scores_group(n, slot, g):
        q_lo, ws, b_lo = window(n)
        qT = qT_ref[0, g * GROUP * HEAD_DIM:(g + 1) * GROUP * HEAD_DIM, pl.ds(q_lo, BLOCK)]
        w = jnp.concatenate(
            [_padded_q(qT[j * HEAD_DIM:(j + 1) * HEAD_DIM], g) for j in range(GROUP)], axis=1)
        sT = jnp.dot(k_ref[0, pl.ds(ws, span), :], w, preferred_element_type=F32)
        ms = []
        for j in range(GROUP):
            s = sT[:, j * BLOCK:(j + 1) * BLOCK] + bias_ref[g * GROUP + j, pl.ds(b_lo, span), :]
            s_ref[slot, :, g * gn + j * BLOCK:g * gn + (j + 1) * BLOCK] = s
            ms.append(jnp.max(_fold_max(s), axis=0, keepdims=True))
        return jnp.maximum(jnp.concatenate(ms, axis=1), sink_row(g))

    def values_group(n, slot, g, m):
        _, ws, _ = window(n)
        pT = jnp.exp2(s_ref[slot, :, g * gn:(g + 1) * gn] - m).astype(BF16)
        oT = jnp.dot(vT_ref[0, g, :, pl.ds(ws, span)], pT, preferred_element_type=F32)
        o = _normalise(oT, jnp.exp2(sink_row(g) - m))
        return [o[:, j * BLOCK:(j + 1) * BLOCK] for j in range(GROUP)]

    def step(n, slot, m_prev, do_scores=True, do_values=True):
        ms, outs = [], []
        for g in range(KV_HEADS):
            if do_scores:
                ms.append(scores_group(n, slot, g))
            if do_values:
                outs += values_group(n - 1, 1 - slot, g, m_prev[g])
        if do_values:
            q_lo, _, _ = window(n - 1)
            o_ref[0, pl.ds(q_lo, BLOCK), :] = jnp.concatenate(outs, axis=0).T.astype(o_ref.dtype)
        return tuple(ms) if do_scores else None

    unroll = 4

    def body(u, m):
        for i in range(1, unroll + 1):
            m = step(unroll * u + i, i % 2, m)
        return m

    m = step(0, 0, None, do_values=False)
    n_loop = (nblk - 1) // unroll
    m = lax.fori_loop(0, n_loop, body, m)
    for n in range(n_loop * unroll + 1, nblk):
        m = step(n, n % 2, m)
    step(nblk, nblk % 2, m, do_scores=False)


def _attn_window(sink, qT, k, vT):
    bsz, _, seq = qT.shape
    bias = _alibi_table()
    return pl.pallas_call(
        _attn_window_kernel,
        grid=(bsz,),
        in_specs=[
            pl.BlockSpec(memory_space=pltpu.SMEM),
            pl.BlockSpec(bias.shape, lambda b: (0, 0, 0)),
            pl.BlockSpec((1, Q_WIDTH, seq), lambda b: (b, 0, 0)),
            pl.BlockSpec((1, seq, KV_WIDTH), lambda b: (b, 0, 0)),
            pl.BlockSpec((1, KV_HEADS, V_AUG, seq), lambda b: (b, 0, 0, 0)),
        ],
        out_specs=pl.BlockSpec((1, seq, Q_WIDTH), lambda b: (b, 0, 0)),
        out_shape=jax.ShapeDtypeStruct((bsz, seq, Q_WIDTH), BF16),
        scratch_shapes=[pltpu.VMEM((2, 3 * BLOCK, HEADS * BLOCK), F32)],
        compiler_params=pltpu.CompilerParams(
            dimension_semantics=("parallel",), vmem_limit_bytes=VMEM_LIMIT),
        name="attn_window",
    )(sink, bias, qT, k, vT)


def _attn_global_kernel(qT_ref, k_ref, vT_ref, o_ref, s_ref, *, tq, kc):
    seq = k_ref.shape[1]
    gw = GROUP * HEAD_DIM
    n_groups = (seq // tq) * KV_HEADS
    chunks = range(0, seq, kc)

    def padded_q(grp, j):
        t, g = grp // KV_HEADS, grp % KV_HEADS
        qT_h = qT_ref[0, pl.ds(_aligned(g * gw + j * HEAD_DIM, HEAD_DIM), HEAD_DIM),
                      pl.ds(_aligned(t * tq, tq), tq)]
        if isinstance(g, int):
            return _padded_q(qT_h, g)
        return jnp.where(g == 0, _padded_q(qT_h, 0), _padded_q(qT_h, 1))

    def scores_chunk(slot, w, k_lo, m8):
        sT = jnp.dot(k_ref[0, k_lo:k_lo + kc, :], w, preferred_element_type=F32)
        s_ref[slot, k_lo:k_lo + kc, :] = sT
        m8_c = _fold_max(sT)
        return m8_c if m8 is None else jnp.maximum(m8, m8_c)

    def values_chunk(g, slot, m, k_lo, acc):
        pT = jnp.exp2(s_ref[slot, k_lo:k_lo + kc, :] - m).astype(BF16)
        oT = jnp.dot(vT_ref[0, g, :, k_lo:k_lo + kc], pT, preferred_element_type=F32)
        return oT if acc is None else acc + oT

    def scores(grp, j):
        w, m8 = padded_q(grp, j), None
        for k_lo in chunks:
            m8 = scores_chunk(j % 2, w, k_lo, m8)
        return jnp.max(m8, axis=0, keepdims=True)

    def group(grp, m, last=False):
        t, g = grp // KV_HEADS, grp % KV_HEADS
        outs = []
        for j in range(GROUP):
            nxt = (grp, j + 1) if j + 1 < GROUP else (None if last else (grp + 1, 0))
            w = padded_q(*nxt) if nxt else None
            m8 = acc = None
            for k_lo in chunks:
                if nxt:
                    m8 = scores_chunk((j + 1) % 2, w, k_lo, m8)
                acc = values_chunk(g, j % 2, m, k_lo, acc)
            outs.append(_normalise(acc))
            m = jnp.max(m8, axis=0, keepdims=True) if nxt else None
        o_ref[0, pl.ds(_aligned(t * tq, tq), tq), pl.ds(_aligned(g * gw, gw), gw)] = (
            jnp.concatenate(outs, axis=0).T.astype(o_ref.dtype))
        return m

    m = scores(0, 0)
    m = lax.fori_loop(0, n_groups - 1, group, m)
    group(n_groups - 1, m, last=True)


def _attn_global(qT, k, vT, tq, kc):
    bsz, _, seq = qT.shape
    return pl.pallas_call(
        functools.partial(_attn_global_kernel, tq=tq, kc=kc),
        grid=(bsz,),
        in_specs=[
            pl.BlockSpec((1, Q_WIDTH, seq), lambda b: (b, 0, 0)),
            pl.BlockSpec((1, seq, KV_WIDTH), lambda b: (b, 0, 0)),
            pl.BlockSpec((1, KV_HEADS, V_AUG, seq), lambda b: (b, 0, 0, 0)),
        ],
        out_specs=pl.BlockSpec((1, seq, Q_WIDTH), lambda b: (b, 0, 0)),
        out_shape=jax.ShapeDtypeStruct((bsz, seq, Q_WIDTH), BF16),
        scratch_shapes=[pltpu.VMEM((2, seq, tq), F32)],
        compiler_params=pltpu.CompilerParams(
            dimension_semantics=("parallel",), vmem_limit_bytes=VMEM_LIMIT),
        name="attn_global",
    )(qT, k, vT)


def _out_kernel(x_ref, ya_ref, yb_ref, wz_ref, wg_ref, bg_ref, wpa_ref, wpb_ref, wo_ref, lng_ref, lnb_ref,
                o_ref, *, sub):
    for lo in range(0, x_ref.shape[0], sub):
        rows = slice(lo, lo + sub)
        x = x_ref[rows, :]
        xb = x.astype(BF16)
        z = jnp.dot(xb, wz_ref[...], preferred_element_type=F32)
        gl = jnp.dot(xb, wg_ref[...], preferred_element_type=F32)
        sz = z * jax.nn.sigmoid(z)
        ua = (ya_ref[rows, :].astype(F32) * sz[:, :Q_WIDTH]).astype(BF16)
        ub = (yb_ref[rows, :].astype(F32) * sz[:, Q_WIDTH:]).astype(BF16)
        pa = jnp.dot(ua, wpa_ref[...], preferred_element_type=F32)
        pb = jnp.dot(ub, wpb_ref[...], preferred_element_type=F32)
        gate = jax.nn.sigmoid(gl + bg_ref[...])
        mix = (gate[:, :D_MODEL] * pa + gate[:, D_MODEL:] * pb).astype(BF16)
        y = jnp.dot(mix, wo_ref[...], preferred_element_type=F32)
        r = DN_ALPHA * x + y
        mu = jnp.mean(r, axis=-1, keepdims=True)
        rc = r - mu
        var = jnp.mean(rc * rc, axis=-1, keepdims=True)
        o_ref[rows, :] = rc * lax.rsqrt(var + LN_EPS) * lng_ref[...] + lnb_ref[...]


def _out_block(x2, ya2, yb2, wz, wg, bg, wpa, wpb, wo, lng, lnb, tm, sub):
    n, d = x2.shape
    const = lambda a: pl.BlockSpec(a.shape, lambda i: (0, 0), pipeline_mode=pl.Buffered(1))
    row = lambda w: pl.BlockSpec((tm, w), lambda i: (i, 0))
    return pl.pallas_call(
        functools.partial(_out_kernel, sub=sub),
        grid=(n // tm,),
        in_specs=[row(d), row(Q_WIDTH), row(Q_WIDTH),
                  const(wz), const(wg), const(bg), const(wpa), const(wpb), const(wo), const(lng), const(lnb)],
        out_specs=row(d),
        out_shape=jax.ShapeDtypeStruct((n, d), x2.dtype),
        compiler_params=pltpu.CompilerParams(
            dimension_semantics=("parallel",), vmem_limit_bytes=VMEM_LIMIT),
        name="out_block",
    )(x2, ya2, yb2, wz, wg, bg, wpa, wpb, wo, lng, lnb)


def _rope_tables(seq):
    half = HEAD_DIM // 2
    axis_pairs = half // 2
    pos = np.arange(seq)
    freqs = ROPE_THETA ** (-np.arange(axis_pairs, dtype=np.float64) / axis_pairs)
    ang = np.concatenate([(pos // GRID_W)[:, None] * freqs, (pos % GRID_W)[:, None] * freqs], axis=-1)
    cos, sin = np.cos(ang), np.sin(ang)
    cosk = np.tile(cos, (1, KV_WIDTH // half))
    sink = np.tile(np.concatenate([-sin, sin], axis=-1), (1, KV_HEADS))
    return tuple(jnp.asarray(a, F32) for a in (cos.T, sin.T, cosk, sink))


def _layer(x, layer, w_in, b_gate, sink_a, qnorm_b, knorm_b, w_proj_a, w_proj_b, w_out, ln_g, ln_b):
    bsz, seq, d = x.shape
    tm_qkv, sub_qkv, tq, tm_out = 1024, 256, 512, 1024

    wt, wk = _prep_qkv_weights(w_in, layer)
    gq = qnorm_b[layer].astype(F32)[:, None]
    gk = knorm_b[layer].astype(F32)[None, :]
    cosT, sinT, cosk, sink = _rope_tables(seq)

    (qaT, ka, vaT, qbT, kb, vbT), (wz, wg, wpa, wpb, wo) = _qkv_proj(
        x, wt, wk, gq, gk, cosT, sinT, cosk, sink, w_in, w_proj_a, w_proj_b, w_out, layer, tm_qkv, sub_qkv)
    ya = _attn_window(sink_a[layer].astype(F32), qaT, ka, vaT)
    yb = _attn_global(qbT, kb, vbT, tq, 256)

    out = _out_block(
        x.reshape(bsz * seq, d), ya.reshape(bsz * seq, Q_WIDTH), yb.reshape(bsz * seq, Q_WIDTH),
        wz, wg, b_gate[layer].astype(F32)[None, :], wpa, wpb, wo,
        ln_g[layer].astype(F32)[None, :], ln_b[layer].astype(F32)[None, :], tm_out, 256)
    return out.reshape(bsz, seq, d)


def kernel(x, w_in, b_gate, sink_a, qnorm_b, knorm_b, w_proj_a, w_proj_b, w_out, ln_g, ln_b):
    h = x
    for layer in range(DEPTH):
        h = _layer(h, layer, w_in, b_gate, sink_a, qnorm_b, knorm_b, w_proj_a, w_proj_b, w_out, ln_g, ln_b)
    return h
```

```python
import functools

import jax
import jax.numpy as jnp
import numpy as np
from jax import lax
from jax.experimental import pallas as pl
from jax.experimental.pallas import tpu as pltpu

D_MODEL = 1024
HEAD_DIM = 64
HEADS = 8
KV_HEADS = 2
GROUP = HEADS // KV_HEADS
Q_WIDTH = HEADS * HEAD_DIM
KV_WIDTH = KV_HEADS * HEAD_DIM
WINDOW = 128
BLOCK = 128
GRID_W = 64
ROPE_THETA = 10000.0
QK_EPS = 1e-6
LN_EPS = 1e-5
DEPTH = 1
DN_ALPHA = (2.0 * DEPTH) ** 0.25
SCALE = HEAD_DIM ** -0.5
LOG2E = 1.4426950408889634

_SPLITS = (Q_WIDTH, KV_WIDTH, KV_WIDTH, Q_WIDTH, Q_WIDTH, KV_WIDTH, KV_WIDTH, Q_WIDTH, D_MODEL, D_MODEL)
_OFF = tuple(int(v) for v in np.cumsum((0,) + _SPLITS))
(QA, KA, VA, ZA, QB, KB, VB, ZB, GA, GB) = range(10)

V_AUG = HEAD_DIM + 16
VMEM_LIMIT = 56 * 1024 * 1024

BF16 = jnp.bfloat16
F32 = jnp.float32


def _cols(w, *ids):
    return jnp.concatenate([w[:, _OFF[i]:_OFF[i + 1]] for i in ids], axis=1)


def _prep_kernel(w_ref, wt_out, wk_out):
    w = w_ref[0]
    wt_out[...] = _cols(w, QA, VA, QB, VB).T.astype(BF16)
    wk_out[...] = _cols(w, KA, KB).astype(BF16)


def _prep_qkv_weights(w_in, layer, steps=4):
    d = w_in.shape[1]
    t_width = 2 * (Q_WIDTH + KV_WIDTH)
    return pl.pallas_call(
        _prep_kernel,
        grid=(steps,),
        in_specs=[pl.BlockSpec((1, d // steps, _OFF[ZB]), lambda i: (layer, i, 0))],
        out_specs=[pl.BlockSpec((t_width, d // steps), lambda i: (0, i)),
                   pl.BlockSpec((d // steps, 2 * KV_WIDTH), lambda i: (i, 0))],
        out_shape=[jax.ShapeDtypeStruct((t_width, d), BF16), jax.ShapeDtypeStruct((d, 2 * KV_WIDTH), BF16)],
        compiler_params=pltpu.CompilerParams(
            dimension_semantics=("parallel",), vmem_limit_bytes=VMEM_LIMIT),
        name="prep_qkv_weights",
    )(w_in)


def _qkv_kernel(x_ref, wt_ref, wk_ref, gq_ref, gk_ref, cosT_ref, sinT_ref, cosk_ref, sink_ref,
                w_ref, wpa_ref, wpb_ref, wo_ref,
                qaT_ref, ka_ref, vaT_ref, qbT_ref, kb_ref, vbT_ref,
                wz_out, wg_out, wpa_out, wpb_out, wo_out, *, sub):
    half = HEAD_DIM // 2
    ones = jnp.ones((V_AUG - HEAD_DIM, sub), BF16)
    gq = gq_ref[...][None]
    gk = jnp.concatenate([gk_ref[...]] * KV_HEADS, axis=1)

    w = w_ref[0]
    wz_out[...] = _cols(w, ZA, ZB).astype(BF16)
    wg_out[...] = _cols(w, GA, GB).astype(BF16)
    wpa_out[...] = wpa_ref[0].astype(BF16)
    wpb_out[...] = wpb_ref[0].astype(BF16)
    wo_out[...] = wo_ref[0].astype(BF16)

    for lo in range(0, x_ref.shape[1], sub):
        tok = slice(lo, lo + sub)
        xb = x_ref[0, tok, :].astype(BF16)
        t = lax.dot_general(wt_ref[...], xb, (((1,), (1,)), ((), ())), preferred_element_type=F32)
        kk = jnp.dot(xb, wk_ref[...], preferred_element_type=F32)

        def store_v(vT_ref, vT):
            for g in range(KV_HEADS):
                vT_ref[0, g, 0:HEAD_DIM, tok] = vT[g * HEAD_DIM:(g + 1) * HEAD_DIM].astype(BF16)
                vT_ref[0, g, HEAD_DIM:V_AUG, tok] = ones

        r0 = 0
        qaT_ref[0, :, tok] = (t[r0:r0 + Q_WIDTH] * (SCALE * LOG2E)).astype(BF16)
        r0 += Q_WIDTH
        store_v(vaT_ref, t[r0:r0 + KV_WIDTH])
        r0 += KV_WIDTH
        qb = t[r0:r0 + Q_WIDTH].reshape(HEADS, HEAD_DIM, sub)
        r0 += Q_WIDTH
        store_v(vbT_ref, t[r0:r0 + KV_WIDTH])

        ms = jnp.mean(qb * qb, axis=1, keepdims=True)
        qn = qb * lax.rsqrt(ms + QK_EPS) * gq
        q1, q2 = qn[:, :half], qn[:, half:]
        cosT, sinT = cosT_ref[:, tok][None], sinT_ref[:, tok][None]
        qr = jnp.concatenate([q1 * cosT - q2 * sinT, q2 * cosT + q1 * sinT], axis=1)
        qbT_ref[0, :, tok] = (qr * (SCALE * LOG2E)).reshape(Q_WIDTH, sub).astype(BF16)

        ka_ref[0, tok, :] = kk[:, :KV_WIDTH].astype(BF16)

        kb = kk[:, KV_WIDTH:]
        lane = lax.broadcasted_iota(jnp.int32, kb.shape, 1)
        k2 = kb * kb
        first = lane < HEAD_DIM
        ms0 = jnp.sum(jnp.where(first, k2, 0.0), axis=1, keepdims=True) * (1.0 / HEAD_DIM)
        ms1 = jnp.sum(jnp.where(first, 0.0, k2), axis=1, keepdims=True) * (1.0 / HEAD_DIM)
        kn = kb * lax.rsqrt(jnp.where(first, ms0, ms1) + QK_EPS) * gk
        swapped = jnp.where(lane % HEAD_DIM < half,
                            pltpu.roll(kn, KV_WIDTH - half, axis=1),
                            pltpu.roll(kn, half, axis=1))
        kb_ref[0, tok, :] = (kn * cosk_ref[tok, :] + swapped * sink_ref[tok, :]).astype(BF16)


def _qkv_proj(x, wt, wk, gq, gk, cosT, sinT, cosk, sink, w_in, w_proj_a, w_proj_b, w_out, layer, tm, sub):
    bsz, seq, d = x.shape
    nt = seq // tm
    steps = bsz * nt
    const = lambda shape: pl.BlockSpec(shape, lambda b, i: (0,) * len(shape))
    qT_spec = pl.BlockSpec((1, Q_WIDTH, tm), lambda b, i: (b, 0, i))
    k_spec = pl.BlockSpec((1, tm, KV_WIDTH), lambda b, i: (b, i, 0))
    vT_spec = pl.BlockSpec((1, KV_HEADS, V_AUG, tm), lambda b, i: (b, 0, 0, i))
    qT_shape = jax.ShapeDtypeStruct((bsz, Q_WIDTH, seq), BF16)
    k_shape = jax.ShapeDtypeStruct((bsz, seq, KV_WIDTH), BF16)
    vT_shape = jax.ShapeDtypeStruct((bsz, KV_HEADS, V_AUG, seq), BF16)
    slab_in = lambda a: pl.BlockSpec((1, a.shape[1] // steps, a.shape[2]), lambda b, i: (layer, b * nt + i, 0))
    slab_out = lambda rows, width: pl.BlockSpec((rows // steps, width), lambda b, i: (b * nt + i, 0))
    w_shapes = [(d, 2 * Q_WIDTH), (d, 2 * D_MODEL), w_proj_a.shape[1:], w_proj_b.shape[1:], w_out.shape[1:]]
    outs = pl.pallas_call(
        functools.partial(_qkv_kernel, sub=sub),
        grid=(bsz, nt),
        in_specs=[
            pl.BlockSpec((1, tm, d), lambda b, i: (b, i, 0)),
            const(wt.shape), const(wk.shape), const(gq.shape), const(gk.shape),
            pl.BlockSpec((HEAD_DIM // 2, tm), lambda b, i: (0, i)),
            pl.BlockSpec((HEAD_DIM // 2, tm), lambda b, i: (0, i)),
            pl.BlockSpec((tm, KV_WIDTH), lambda b, i: (i, 0)),
            pl.BlockSpec((tm, KV_WIDTH), lambda b, i: (i, 0)),
            slab_in(w_in), slab_in(w_proj_a), slab_in(w_proj_b), slab_in(w_out),
        ],
        out_specs=[qT_spec, k_spec, vT_spec, qT_spec, k_spec, vT_spec] + [slab_out(*s) for s in w_shapes],
        out_shape=[qT_shape, k_shape, vT_shape, qT_shape, k_shape, vT_shape]
        + [jax.ShapeDtypeStruct(s, BF16) for s in w_shapes],
        compiler_params=pltpu.CompilerParams(
            dimension_semantics=("parallel", "parallel"), vmem_limit_bytes=VMEM_LIMIT),
        name="qkv_proj",
    )(x, wt, wk, gq, gk, cosT, sinT, cosk, sink, w_in, w_proj_a, w_proj_b, w_out)
    return outs[:6], outs[6:]


def _padded_q(qT_h, g):
    zeros = jnp.zeros_like(qT_h)
    parts = [zeros] * KV_HEADS
    parts[g] = qT_h
    return jnp.concatenate(parts, axis=0)


def _aligned(v, m):
    return v if isinstance(v, int) else pl.multiple_of(v, m)


def _fold_max(s):
    rows, n = s.shape
    while rows % 16 == 0:
        rows //= 2
        s = jnp.maximum(s[:rows], s[rows:])
    return s if rows == 8 else jnp.max(s.reshape(rows // 8, 8, n), axis=0)


def _normalise(oT, extra=None):
    denom = oT[HEAD_DIM:HEAD_DIM + 1]
    if extra is not None:
        denom = denom + extra
    return oT[:HEAD_DIM] * (1.0 / denom)


def _alibi_table():
    r = np.arange(5 * BLOCK)[:, None]
    c = np.arange(BLOCK)[None, :]
    dist = np.abs(r - 2 * BLOCK - c).astype(np.float64)
    slopes = 2.0 ** (-8.0 * (np.arange(HEADS) + 1.0) / HEADS) * LOG2E
    bias = np.where(dist <= WINDOW, -slopes[:, None, None] * dist, -np.inf)
    return jnp.asarray(bias, F32)


def _attn_window_kernel(sink_ref, bias_ref, qT_ref, k_ref, vT_ref, o_ref, s_ref):
    seq = k_ref.shape[1]
    nblk = seq // BLOCK
    span = 3 * BLOCK
    gn = GROUP * BLOCK

    def sink_row(g):
        return jnp.concatenate(
            [jnp.full((1, BLOCK), sink_ref[g * GROUP + j] * LOG2E, F32) for j in range(GROUP)], axis=1)

    def window(n):
        if isinstance(n, int):
            q_lo, ws = n * BLOCK, min(max((n - 1) * BLOCK, 0), seq - span)
            return q_lo, ws, 2 * BLOCK - (q_lo - ws)
        q_lo = pl.multiple_of(n * BLOCK, BLOCK)
        ws = pl.multiple_of(jnp.clip((n - 1) * BLOCK, 0, seq - span), BLOCK)
        return q_lo, ws, pl.multiple_of(2 * BLOCK - (q_lo - ws), BLOCK)

    def scores_group(n, slot, g):
        q_lo, ws, b_lo = window(n)
        qT = qT_ref[0, g * GROUP * HEAD_DIM:(g + 1) * GROUP * HEAD_DIM, pl.ds(q_lo, BLOCK)]
        w = jnp.concatenate(
            [_padded_q(qT[j * HEAD_DIM:(j + 1) * HEAD_DIM], g) for j in range(GROUP)], axis=1)
        sT = jnp.dot(k_ref[0, pl.ds(ws, span), :], w, preferred_element_type=F32)
        ms = []
        for j in range(GROUP):
            s = sT[:, j * BLOCK:(j + 1) * BLOCK] + bias_ref[g * GROUP + j, pl.ds(b_lo, span), :]
            s_ref[slot, :, g * gn + j * BLOCK:g * gn + (j + 1) * BLOCK] = s
            ms.append(jnp.max(_fold_max(s), axis=0, keepdims=True))
        return jnp.maximum(jnp.concatenate(ms, axis=1), sink_row(g))

    def values_group(n, slot, g, m):
        _, ws, _ = window(n)
        pT = jnp.exp2(s_ref[slot, :, g * gn:(g + 1) * gn] - m).astype(BF16)
        oT = jnp.dot(vT_ref[0, g, :, pl.ds(ws, span)], pT, preferred_element_type=F32)
        o = _normalise(oT, jnp.exp2(sink_row(g) - m))
        return [o[:, j * BLOCK:(j + 1) * BLOCK] for j in range(GROUP)]

    def step(n, slot, m_prev, do_scores=True, do_values=True):
        ms, outs = [], []
        for g in range(KV_HEADS):
            if do_scores:
                ms.append(scores_group(n, slot, g))
            if do_values:
                outs += values_group(n - 1, 1 - slot, g, m_prev[g])
        if do_values:
            q_lo, _, _ = window(n - 1)
            o_ref[0, pl.ds(q_lo, BLOCK), :] = jnp.concatenate(outs, axis=0).T.astype(o_ref.dtype)
        return tuple(ms) if do_scores else None

    unroll = 4

    def body(u, m):
        for i in range(1, unroll + 1):
            m = step(unroll * u + i, i % 2, m)
        return m

    m = step(0, 0, None, do_values=False)
    n_loop = (nblk - 1) // unroll
    m = lax.fori_loop(0, n_loop, body, m)
    for n in range(n_loop * unroll + 1, nblk):
        m = step(n, n % 2, m)
    step(nblk, nblk % 2, m, do_scores=False)


def _attn_window(sink, qT, k, vT):
    bsz, _, seq = qT.shape
    bias = _alibi_table()
    return pl.pallas_call(
        _attn_window_kernel,
        grid=(bsz,),
        in_specs=[
            pl.BlockSpec(memory_space=pltpu.SMEM),
            pl.BlockSpec(bias.shape, lambda b: (0, 0, 0)),
            pl.BlockSpec((1, Q_WIDTH, seq), lambda b: (b, 0, 0)),
            pl.BlockSpec((1, seq, KV_WIDTH), lambda b: (b, 0, 0)),
            pl.BlockSpec((1, KV_HEADS, V_AUG, seq), lambda b: (b, 0, 0, 0)),
        ],
        out_specs=pl.BlockSpec((1, seq, Q_WIDTH), lambda b: (b, 0, 0)),
        out_shape=jax.ShapeDtypeStruct((bsz, seq, Q_WIDTH), BF16),
        scratch_shapes=[pltpu.VMEM((2, 3 * BLOCK, HEADS * BLOCK), F32)],
        compiler_params=pltpu.CompilerParams(
            dimension_semantics=("parallel",), vmem_limit_bytes=VMEM_LIMIT),
        name="attn_window",
    )(sink, bias, qT, k, vT)


def _attn_global_kernel(qT_ref, k_ref, vT_ref, o_ref, s_ref, *, tq, kc):
    seq = k_ref.shape[1]
    gw = GROUP * HEAD_DIM
    n_tiles = seq // tq
    chunks = range(0, seq, kc)

    def padded_q(t, h):
        qT_h = qT_ref[0, h * HEAD_DIM:(h + 1) * HEAD_DIM, pl.ds(_aligned(t * tq, tq), tq)]
        return _padded_q(qT_h, h // GROUP)

    def scores_chunk(slot, w, k_lo, m8):
        sT = jnp.dot(k_ref[0, k_lo:k_lo + kc, :], w, preferred_element_type=F32)
        s_ref[slot, k_lo:k_lo + kc, :] = sT
        m8_c = _fold_max(sT)
        return m8_c if m8 is None else jnp.maximum(m8, m8_c)

    def values_chunk(g, slot, m, k_lo, acc):
        pT = jnp.exp2(s_ref[slot, k_lo:k_lo + kc, :] - m).astype(BF16)
        oT = jnp.dot(vT_ref[0, g, :, k_lo:k_lo + kc], pT, preferred_element_type=F32)
        return oT if acc is None else acc + oT

    def first_scores():
        w, m8 = padded_q(0, 0), None
        for k_lo in chunks:
            m8 = scores_chunk(0, w, k_lo, m8)
        return jnp.max(m8, axis=0, keepdims=True)

    def tile(t, m, last=False):
        outs = []
        for h in range(HEADS):
            nxt = (t, h + 1) if h + 1 < HEADS else (None if last else (t + 1, 0))
            w = padded_q(*nxt) if nxt else None
            m8 = acc = None
            for k_lo in chunks:
                if nxt:
                    m8 = scores_chunk((h + 1) % 2, w, k_lo, m8)
                acc = values_chunk(h // GROUP, h % 2, m, k_lo, acc)
            outs.append(_normalise(acc))
            m = jnp.max(m8, axis=0, keepdims=True) if nxt else None
            if h % GROUP == GROUP - 1:
                g = h // GROUP
                o_ref[0, pl.ds(_aligned(t * tq, tq), tq), g * gw:(g + 1) * gw] = (
                    jnp.concatenate(outs, axis=0).T.astype(o_ref.dtype))
                outs = []
        return m

    m = first_scores()
    m = lax.fori_loop(0, n_tiles - 1, tile, m)
    tile(n_tiles - 1, m, last=True)


def _attn_global(qT, k, vT, tq, kc):
    bsz, _, seq = qT.shape
    return pl.pallas_call(
        functools.partial(_attn_global_kernel, tq=tq, kc=kc),
        grid=(bsz,),
        in_specs=[
            pl.BlockSpec((1, Q_WIDTH, seq), lambda b: (b, 0, 0)),
            pl.BlockSpec((1, seq, KV_WIDTH), lambda b: (b, 0, 0)),
            pl.BlockSpec((1, KV_HEADS, V_AUG, seq), lambda b: (b, 0, 0, 0)),
        ],
        out_specs=pl.BlockSpec((1, seq, Q_WIDTH), lambda b: (b, 0, 0)),
        out_shape=jax.ShapeDtypeStruct((bsz, seq, Q_WIDTH), BF16),
        scratch_shapes=[pltpu.VMEM((2, seq, tq), F32)],
        compiler_params=pltpu.CompilerParams(
            dimension_semantics=("parallel",), vmem_limit_bytes=VMEM_LIMIT),
        name="attn_global",
    )(qT, k, vT)


def _out_kernel(x_ref, ya_ref, yb_ref, wz_ref, wg_ref, bg_ref, wpa_ref, wpb_ref, wo_ref, lng_ref, lnb_ref,
                o_ref, *, sub):
    for lo in range(0, x_ref.shape[0], sub):
        rows = slice(lo, lo + sub)
        x = x_ref[rows, :]
        xb = x.astype(BF16)
        z = jnp.dot(xb, wz_ref[...], preferred_element_type=F32)
        gl = jnp.dot(xb, wg_ref[...], preferred_element_type=F32)
        sz = z * jax.nn.sigmoid(z)
        ua = (ya_ref[rows, :].astype(F32) * sz[:, :Q_WIDTH]).astype(BF16)
        ub = (yb_ref[rows, :].astype(F32) * sz[:, Q_WIDTH:]).astype(BF16)
        pa = jnp.dot(ua, wpa_ref[...], preferred_element_type=F32)
        pb = jnp.dot(ub, wpb_ref[...], preferred_element_type=F32)
        gate = jax.nn.sigmoid(gl + bg_ref[...])
        mix = (gate[:, :D_MODEL] * pa + gate[:, D_MODEL:] * pb).astype(BF16)
        y = jnp.dot(mix, wo_ref[...], preferred_element_type=F32)
        r = DN_ALPHA * x + y
        mu = jnp.mean(r, axis=-1, keepdims=True)
        rc = r - mu
        var = jnp.mean(rc * rc, axis=-1, keepdims=True)
        o_ref[rows, :] = rc * lax.rsqrt(var + LN_EPS) * lng_ref[...] + lnb_ref[...]


def _out_block(x2, ya2, yb2, wz, wg, bg, wpa, wpb, wo, lng, lnb, tm, sub):
    n, d = x2.shape
    const = lambda a: pl.BlockSpec(a.shape, lambda i: (0, 0), pipeline_mode=pl.Buffered(1))
    row = lambda w: pl.BlockSpec((tm, w), lambda i: (i, 0))
    return pl.pallas_call(
        functools.partial(_out_kernel, sub=sub),
        grid=(n // tm,),
        in_specs=[row(d), row(Q_WIDTH), row(Q_WIDTH),
                  const(wz), const(wg), const(bg), const(wpa), const(wpb), const(wo), const(lng), const(lnb)],
        out_specs=row(d),
        out_shape=jax.ShapeDtypeStruct((n, d), x2.dtype),
        compiler_params=pltpu.CompilerParams(
            dimension_semantics=("parallel",), vmem_limit_bytes=VMEM_LIMIT),
        name="out_block",
    )(x2, ya2, yb2, wz, wg, bg, wpa, wpb, wo, lng, lnb)


def _rope_tables(seq):
    half = HEAD_DIM // 2
    axis_pairs = half // 2
    pos = np.arange(seq)
    freqs = ROPE_THETA ** (-np.arange(axis_pairs, dtype=np.float64) / axis_pairs)
    ang = np.concatenate([(pos // GRID_W)[:, None] * freqs, (pos % GRID_W)[:, None] * freqs], axis=-1)
    cos, sin = np.cos(ang), np.sin(ang)
    cosk = np.tile(cos, (1, KV_WIDTH // half))
    sink = np.tile(np.concatenate([-sin, sin], axis=-1), (1, KV_HEADS))
    return tuple(jnp.asarray(a, F32) for a in (cos.T, sin.T, cosk, sink))


def _layer(x, layer, w_in, b_gate, sink_a, qnorm_b, knorm_b, w_proj_a, w_proj_b, w_out, ln_g, ln_b):
    bsz, seq, d = x.shape
    tm_qkv, sub_qkv, tq, tm_out = 2048, 256, 512, 1024

    wt, wk = _prep_qkv_weights(w_in, layer)
    gq = qnorm_b[layer].astype(F32)[:, None]
    gk = knorm_b[layer].astype(F32)[None, :]
    cosT, sinT, cosk, sink = _rope_tables(seq)

    (qaT, ka, vaT, qbT, kb, vbT), (wz, wg, wpa, wpb, wo) = _qkv_proj(
        x, wt, wk, gq, gk, cosT, sinT, cosk, sink, w_in, w_proj_a, w_proj_b, w_out, layer, tm_qkv, sub_qkv)
    ya = _attn_window(sink_a[layer].astype(F32), qaT, ka, vaT)
    yb = _attn_global(qbT, kb, vbT, tq, 256)

    out = _out_block(
        x.reshape(bsz * seq, d), ya.reshape(bsz * seq, Q_WIDTH), yb.reshape(bsz * seq, Q_WIDTH),
        wz, wg, b_gate[layer].astype(F32)[None, :], wpa, wpb, wo,
        ln_g[layer].astype(F32)[None, :], ln_b[layer].astype(F32)[None, :], tm_out, 256)
    return out.reshape(bsz, seq, d)


def kernel(x, w_in, b_gate, sink_a, qnorm_b, knorm_b, w_proj_a, w_proj_b, w_out, ln_g, ln_b):
    h = x
    for layer in range(DEPTH):
        h = _layer(h, layer, w_in, b_gate, sink_a, qnorm_b, knorm_b, w_proj_a, w_proj_b, w_out, ln_g, ln_b)
    return h
```

```python
import functools

import jax
import jax.numpy as jnp
import numpy as np
from jax import lax
from jax.experimental import pallas as pl
from jax.experimental.pallas import tpu as pltpu

D_MODEL = 1024
HEAD_DIM = 64
HEADS = 8
KV_HEADS = 2
GROUP = HEADS // KV_HEADS
Q_WIDTH = HEADS * HEAD_DIM
KV_WIDTH = KV_HEADS * HEAD_DIM
WINDOW = 128
BLOCK = 128
GRID_W = 64
ROPE_THETA = 10000.0
QK_EPS = 1e-6
LN_EPS = 1e-5
DEPTH = 1
DN_ALPHA = (2.0 * DEPTH) ** 0.25
SCALE = HEAD_DIM ** -0.5
LOG2E = 1.4426950408889634

_SPLITS = (Q_WIDTH, KV_WIDTH, KV_WIDTH, Q_WIDTH, Q_WIDTH, KV_WIDTH, KV_WIDTH, Q_WIDTH, D_MODEL, D_MODEL)
_OFF = tuple(int(v) for v in np.cumsum((0,) + _SPLITS))
(QA, KA, VA, ZA, QB, KB, VB, ZB, GA, GB) = range(10)

MXU_TILE = 256
F32_SUBLANES = 8
BF16_SUBLANES = 16
VMEM_BYTES = 64 * 1024 * 1024
VMEM_LIMIT = VMEM_BYTES * 7 // 8

V_AUG = HEAD_DIM + BF16_SUBLANES

BF16 = jnp.bfloat16
F32 = jnp.float32


def _cols(w, *ids):
    return jnp.concatenate([w[:, _OFF[i]:_OFF[i + 1]] for i in ids], axis=1)


def _prep_kernel(w_ref, wt_out, wk_out):
    w = w_ref[0]
    wt_out[...] = _cols(w, QA, VA, QB, VB).T.astype(BF16)
    wk_out[...] = _cols(w, KA, KB).astype(BF16)


def _prep_qkv_weights(w_in, layer, steps=4):
    d = w_in.shape[1]
    t_width = 2 * (Q_WIDTH + KV_WIDTH)
    return pl.pallas_call(
        _prep_kernel,
        grid=(steps,),
        in_specs=[pl.BlockSpec((1, d // steps, _OFF[ZB]), lambda i: (layer, i, 0))],
        out_specs=[pl.BlockSpec((t_width, d // steps), lambda i: (0, i)),
                   pl.BlockSpec((d // steps, 2 * KV_WIDTH), lambda i: (i, 0))],
        out_shape=[jax.ShapeDtypeStruct((t_width, d), BF16), jax.ShapeDtypeStruct((d, 2 * KV_WIDTH), BF16)],
        compiler_params=pltpu.CompilerParams(
            dimension_semantics=("parallel",), vmem_limit_bytes=VMEM_LIMIT),
        name="prep_qkv_weights",
    )(w_in)


def _qkv_kernel(x_ref, wt_ref, wk_ref, gq_ref, gk_ref, cosT_ref, sinT_ref, cosk_ref, sink_ref,
                qaT_ref, ka_ref, vaT_ref, qbT_ref, kb_ref, vbT_ref, *, sub):
    half = HEAD_DIM // 2
    ones = jnp.ones((V_AUG - HEAD_DIM, sub), BF16)
    eye = (lax.broadcasted_iota(jnp.int32, (HEAD_DIM, HEAD_DIM), 0)
           == lax.broadcasted_iota(jnp.int32, (HEAD_DIM, HEAD_DIM), 1))
    gq = jnp.sum(jnp.where(eye, gq_ref[...], 0.0), axis=1, keepdims=True)[None]
    gk = jnp.concatenate([gk_ref[...]] * KV_HEADS, axis=1)

    for lo in range(0, x_ref.shape[1], sub):
        tok = slice(lo, lo + sub)
        xb = x_ref[0, tok, :].astype(BF16)
        t = lax.dot_general(wt_ref[...], xb, (((1,), (1,)), ((), ())), preferred_element_type=F32)
        kk = jnp.dot(xb, wk_ref[...], preferred_element_type=F32)

        def store_v(vT_ref, vT):
            for g in range(KV_HEADS):
                vT_ref[0, g, 0:HEAD_DIM, tok] = vT[g * HEAD_DIM:(g + 1) * HEAD_DIM].astype(BF16)
                vT_ref[0, g, HEAD_DIM:V_AUG, tok] = ones

        r0 = 0
        qaT_ref[0, :, tok] = (t[r0:r0 + Q_WIDTH] * (SCALE * LOG2E)).astype(BF16)
        r0 += Q_WIDTH
        store_v(vaT_ref, t[r0:r0 + KV_WIDTH])
        r0 += KV_WIDTH
        qb = t[r0:r0 + Q_WIDTH].reshape(HEADS, HEAD_DIM, sub)
        r0 += Q_WIDTH
        store_v(vbT_ref, t[r0:r0 + KV_WIDTH])

        ms = jnp.mean(qb * qb, axis=1, keepdims=True)
        qn = qb * lax.rsqrt(ms + QK_EPS) * gq
        q1, q2 = qn[:, :half], qn[:, half:]
        cosT, sinT = cosT_ref[:, tok][None], sinT_ref[:, tok][None]
        qr = jnp.concatenate([q1 * cosT - q2 * sinT, q2 * cosT + q1 * sinT], axis=1)
        qbT_ref[0, :, tok] = (qr * (SCALE * LOG2E)).reshape(Q_WIDTH, sub).astype(BF16)

        ka_ref[0, tok, :] = kk[:, :KV_WIDTH].astype(BF16)

        kb = kk[:, KV_WIDTH:]
        lane = lax.broadcasted_iota(jnp.int32, kb.shape, 1)
        k2 = kb * kb
        first = lane < HEAD_DIM
        ms0 = jnp.sum(jnp.where(first, k2, 0.0), axis=1, keepdims=True) * (1.0 / HEAD_DIM)
        ms1 = jnp.sum(jnp.where(first, 0.0, k2), axis=1, keepdims=True) * (1.0 / HEAD_DIM)
        kn = kb * lax.rsqrt(jnp.where(first, ms0, ms1) + QK_EPS) * gk
        swapped = jnp.where(lane % HEAD_DIM < half,
                            pltpu.roll(kn, KV_WIDTH - half, axis=1),
                            pltpu.roll(kn, half, axis=1))
        kb_ref[0, tok, :] = (kn * cosk_ref[tok, :] + swapped * sink_ref[tok, :]).astype(BF16)


def _qkv_proj(x, wt, wk, gq, gk, cosT, sinT, cosk, sink, tm, sub):
    bsz, seq, d = x.shape
    nt = seq // tm
    const = lambda shape: pl.BlockSpec(shape, lambda b, i: (0,) * len(shape))
    qT_spec = pl.BlockSpec((1, Q_WIDTH, tm), lambda b, i: (b, 0, i))
    k_spec = pl.BlockSpec((1, tm, KV_WIDTH), lambda b, i: (b, i, 0))
    vT_spec = pl.BlockSpec((1, KV_HEADS, V_AUG, tm), lambda b, i: (b, 0, 0, i))
    qT_shape = jax.ShapeDtypeStruct((bsz, Q_WIDTH, seq), BF16)
    k_shape = jax.ShapeDtypeStruct((bsz, seq, KV_WIDTH), BF16)
    vT_shape = jax.ShapeDtypeStruct((bsz, KV_HEADS, V_AUG, seq), BF16)
    return pl.pallas_call(
        functools.partial(_qkv_kernel, sub=sub),
        grid=(bsz, nt),
        in_specs=[
            pl.BlockSpec((1, tm, d), lambda b, i: (b, i, 0)),
            const(wt.shape), const(wk.shape), const(gq.shape), const(gk.shape),
            pl.BlockSpec((HEAD_DIM // 2, tm), lambda b, i: (0, i)),
            pl.BlockSpec((HEAD_DIM // 2, tm), lambda b, i: (0, i)),
            pl.BlockSpec((tm, KV_WIDTH), lambda b, i: (i, 0)),
            pl.BlockSpec((tm, KV_WIDTH), lambda b, i: (i, 0)),
        ],
        out_specs=[qT_spec, k_spec, vT_spec, qT_spec, k_spec, vT_spec],
        out_shape=[qT_shape, k_shape, vT_shape, qT_shape, k_shape, vT_shape],
        compiler_params=pltpu.CompilerParams(
            dimension_semantics=("parallel", "parallel"), vmem_limit_bytes=VMEM_LIMIT),
        name="qkv_proj",
    )(x, wt, wk, gq, gk, cosT, sinT, cosk, sink)


def _padded_q(qT_h, g):
    zeros = jnp.zeros_like(qT_h)
    parts = [zeros] * KV_HEADS
    parts[g] = qT_h
    return jnp.concatenate(parts, axis=0)


def _aligned(v, m):
    return v if isinstance(v, int) else pl.multiple_of(v, m)


def _fold_max(s):
    rows, n = s.shape
    while rows % (2 * F32_SUBLANES) == 0:
        rows //= 2
        s = jnp.maximum(s[:rows], s[rows:])
    return s if rows == F32_SUBLANES else jnp.max(s.reshape(rows // F32_SUBLANES, F32_SUBLANES, n), axis=0)


def _normalise(oT, extra=None):
    denom = oT[HEAD_DIM:HEAD_DIM + 1]
    if extra is not None:
        denom = denom + extra
    return oT[:HEAD_DIM] * (1.0 / denom)


def _alibi_table():
    r = np.arange(5 * BLOCK)[:, None]
    c = np.arange(BLOCK)[None, :]
    dist = np.abs(r - 2 * BLOCK - c).astype(np.float64)
    slopes = 2.0 ** (-8.0 * (np.arange(HEADS) + 1.0) / HEADS) * LOG2E
    bias = np.where(dist <= WINDOW, -slopes[:, None, None] * dist, -np.inf)
    return jnp.asarray(bias, F32)


def _attn_window_kernel(sink_ref, bias_ref, qT_ref, k_ref, vT_ref, o_ref, s_ref):
    seq = k_ref.shape[1]
    nblk = seq // BLOCK
    span = 3 * BLOCK
    gn = GROUP * BLOCK

    def sink_row(g):
        return jnp.concatenate(
            [jnp.full((1, BLOCK), sink_ref[g * GROUP + j] * LOG2E, F32) for j in range(GROUP)], axis=1)

    def window(n):
        if isinstance(n, int):
            q_lo, ws = n * BLOCK, min(max((n - 1) * BLOCK, 0), seq - span)
            return q_lo, ws, 2 * BLOCK - (q_lo - ws)
        q_lo = pl.multiple_of(n * BLOCK, BLOCK)
        ws = pl.multiple_of(jnp.clip((n - 1) * BLOCK, 0, seq - span), BLOCK)
        return q_lo, ws, pl.multiple_of(2 * BLOCK - (q_lo - ws), BLOCK)

    def scores_group(n, slot, g):
        q_lo, ws, b_lo = window(n)
        qT = qT_ref[0, g * GROUP * HEAD_DIM:(g + 1) * GROUP * HEAD_DIM, pl.ds(q_lo, BLOCK)]
        w = jnp.concatenate(
            [_padded_q(qT[j * HEAD_DIM:(j + 1) * HEAD_DIM], g) for j in range(GROUP)], axis=1)
        sT = jnp.dot(k_ref[0, pl.ds(ws, span), :], w, preferred_element_type=F32)
        ms = []
        for j in range(GROUP):
            s = sT[:, j * BLOCK:(j + 1) * BLOCK] + bias_ref[g * GROUP + j, pl.ds(b_lo, span), :]
            s_ref[slot, :, g * gn + j * BLOCK:g * gn + (j + 1) * BLOCK] = s
            ms.append(jnp.max(_fold_max(s), axis=0, keepdims=True))
        return jnp.maximum(jnp.concatenate(ms, axis=1), sink_row(g))

    def values_group(n, slot, g, m):
        _, ws, _ = window(n)
        pT = jnp.exp2(s_ref[slot, :, g * gn:(g + 1) * gn] - m).astype(BF16)
        oT = jnp.dot(vT_ref[0, g, :, pl.ds(ws, span)], pT, preferred_element_type=F32)
        o = _normalise(oT, jnp.exp2(sink_row(g) - m))
        return [o[:, j * BLOCK:(j + 1) * BLOCK] for j in range(GROUP)]

    def step(n, slot, m_prev, do_scores=True, do_values=True):
        ms, outs = [], []
        for g in range(KV_HEADS):
            if do_scores:
                ms.append(scores_group(n, slot, g))
            if do_values:
                outs += values_group(n - 1, 1 - slot, g, m_prev[g])
        if do_values:
            q_lo, _, _ = window(n - 1)
            o_ref[0, pl.ds(q_lo, BLOCK), :] = jnp.concatenate(outs, axis=0).T.astype(o_ref.dtype)
        return tuple(ms) if do_scores else None

    unroll = 6

    def body(u, m):
        for i in range(1, unroll + 1):
            m = step(unroll * u + i, i % 2, m)
        return m

    m = step(0, 0, None, do_values=False)
    n_loop = (nblk - 1) // unroll
    m = lax.fori_loop(0, n_loop, body, m)
    for n in range(n_loop * unroll + 1, nblk):
        m = step(n, n % 2, m)
    step(nblk, nblk % 2, m, do_scores=False)


def _attn_window(sink, qT, k, vT):
    bsz, _, seq = qT.shape
    bias = _alibi_table()
    return pl.pallas_call(
        _attn_window_kernel,
        grid=(bsz,),
        in_specs=[
            pl.BlockSpec(memory_space=pltpu.SMEM),
            pl.BlockSpec(bias.shape, lambda b: (0, 0, 0)),
            pl.BlockSpec((1, Q_WIDTH, seq), lambda b: (b, 0, 0)),
            pl.BlockSpec((1, seq, KV_WIDTH), lambda b: (b, 0, 0)),
            pl.BlockSpec((1, KV_HEADS, V_AUG, seq), lambda b: (b, 0, 0, 0)),
        ],
        out_specs=pl.BlockSpec((1, seq, Q_WIDTH), lambda b: (b, 0, 0)),
        out_shape=jax.ShapeDtypeStruct((bsz, seq, Q_WIDTH), BF16),
        scratch_shapes=[pltpu.VMEM((2, 3 * BLOCK, HEADS * BLOCK), F32)],
        compiler_params=pltpu.CompilerParams(
            dimension_semantics=("parallel",), vmem_limit_bytes=VMEM_LIMIT),
        name="attn_window",
    )(sink, bias, qT, k, vT)


def _attn_global_kernel(qT_ref, k_ref, vT_ref, w_ref, wpa_ref, wpb_ref, wo_ref,
                        o_ref, wz_out, wg_out, wpa_out, wpb_out, wo_out, s_ref, *, tq, kc):
    seq = k_ref.shape[1]
    gw = GROUP * HEAD_DIM
    n_tiles = seq // tq
    chunks = range(0, seq, kc)

    def padded_q(t, h):
        qT_h = qT_ref[0, h * HEAD_DIM:(h + 1) * HEAD_DIM, pl.ds(_aligned(t * tq, tq), tq)]
        return _padded_q(qT_h, h // GROUP)

    def scores_chunk(slot, w, k_lo, m8):
        sT = jnp.dot(k_ref[0, k_lo:k_lo + kc, :], w, preferred_element_type=F32)
        s_ref[slot, k_lo:k_lo + kc, :] = sT
        m8_c = _fold_max(sT)
        return m8_c if m8 is None else jnp.maximum(m8, m8_c)

    def values_chunk(g, slot, m, k_lo, acc):
        pT = jnp.exp2(s_ref[slot, k_lo:k_lo + kc, :] - m).astype(BF16)
        oT = jnp.dot(vT_ref[0, g, :, k_lo:k_lo + kc], pT, preferred_element_type=F32)
        return oT if acc is None else acc + oT

    def cast_weight_rows(t):
        def rows(ref):
            n = ref.shape[-2] // n_tiles
            return pl.ds(_aligned(t * n, n), n)

        w = w_ref[0, rows(w_ref), :]
        wz_out[rows(wz_out), :] = _cols(w, ZA, ZB).astype(BF16)
        wg_out[rows(wg_out), :] = _cols(w, GA, GB).astype(BF16)
        wpa_out[rows(wpa_out), :] = wpa_ref[0, rows(wpa_ref), :].astype(BF16)
        wpb_out[rows(wpb_out), :] = wpb_ref[0, rows(wpb_ref), :].astype(BF16)
        wo_out[rows(wo_out), :] = wo_ref[0, rows(wo_ref), :].astype(BF16)

    def first_scores():
        w, m8 = padded_q(0, 0), None
        for k_lo in chunks:
            m8 = scores_chunk(0, w, k_lo, m8)
        return jnp.max(m8, axis=0, keepdims=True)

    def tile(t, m, last=False):
        cast_weight_rows(t)
        outs = []
        for h in range(HEADS):
            nxt = (t, h + 1) if h + 1 < HEADS else (None if last else (t + 1, 0))
            w = padded_q(*nxt) if nxt else None
            m8 = acc = None
            for k_lo in chunks:
                if nxt:
                    m8 = scores_chunk((h + 1) % 2, w, k_lo, m8)
                acc = values_chunk(h // GROUP, h % 2, m, k_lo, acc)
            outs.append(_normalise(acc))
            m = jnp.max(m8, axis=0, keepdims=True) if nxt else None
            if h % GROUP == GROUP - 1:
                g = h // GROUP
                o_ref[0, pl.ds(_aligned(t * tq, tq), tq), g * gw:(g + 1) * gw] = (
                    jnp.concatenate(outs, axis=0).T.astype(o_ref.dtype))
                outs = []
        return m

    m = first_scores()
    m = lax.fori_loop(0, n_tiles - 1, tile, m)
    tile(n_tiles - 1, m, last=True)


def _attn_global(qT, k, vT, w_in, w_proj_a, w_proj_b, w_out, layer, tq, kc):
    bsz, _, seq = qT.shape
    d = w_in.shape[1]
    slab_in = lambda a: pl.BlockSpec((1, a.shape[1] // bsz, a.shape[2]), lambda b: (layer, b, 0))
    slab_out = lambda rows, width: pl.BlockSpec((rows // bsz, width), lambda b: (b, 0))
    w_shapes = [(d, 2 * Q_WIDTH), (d, 2 * D_MODEL), w_proj_a.shape[1:], w_proj_b.shape[1:], w_out.shape[1:]]
    outs = pl.pallas_call(
        functools.partial(_attn_global_kernel, tq=tq, kc=kc),
        grid=(bsz,),
        in_specs=[
            pl.BlockSpec((1, Q_WIDTH, seq), lambda b: (b, 0, 0)),
            pl.BlockSpec((1, seq, KV_WIDTH), lambda b: (b, 0, 0)),
            pl.BlockSpec((1, KV_HEADS, V_AUG, seq), lambda b: (b, 0, 0, 0)),
            slab_in(w_in), slab_in(w_proj_a), slab_in(w_proj_b), slab_in(w_out),
        ],
        out_specs=[pl.BlockSpec((1, seq, Q_WIDTH), lambda b: (b, 0, 0))] + [slab_out(*s) for s in w_shapes],
        out_shape=[jax.ShapeDtypeStruct((bsz, seq, Q_WIDTH), BF16)]
        + [jax.ShapeDtypeStruct(s, BF16) for s in w_shapes],
        scratch_shapes=[pltpu.VMEM((2, seq, tq), F32)],
        compiler_params=pltpu.CompilerParams(
            dimension_semantics=("parallel",), vmem_limit_bytes=VMEM_LIMIT),
        name="attn_global",
    )(qT, k, vT, w_in, w_proj_a, w_proj_b, w_out)
    return outs[0], outs[1:]


def _out_kernel(x_ref, ya_ref, yb_ref, wz_ref, wg_ref, bg_ref, wpa_ref, wpb_ref, wo_ref, lng_ref, lnb_ref,
                o_ref, *, sub):
    for lo in range(0, x_ref.shape[0], sub):
        rows = slice(lo, lo + sub)
        x = x_ref[rows, :]
        xb = x.astype(BF16)
        z = jnp.dot(xb, wz_ref[...], preferred_element_type=F32)
        gl = jnp.dot(xb, wg_ref[...], preferred_element_type=F32)
        sz = z * jax.nn.sigmoid(z)
        ua = (ya_ref[rows, :].astype(F32) * sz[:, :Q_WIDTH]).astype(BF16)
        ub = (yb_ref[rows, :].astype(F32) * sz[:, Q_WIDTH:]).astype(BF16)
        pa = jnp.dot(ua, wpa_ref[...], preferred_element_type=F32)
        pb = jnp.dot(ub, wpb_ref[...], preferred_element_type=F32)
        gate = jax.nn.sigmoid(gl + bg_ref[...])
        mix = (gate[:, :D_MODEL] * pa + gate[:, D_MODEL:] * pb).astype(BF16)
        y = jnp.dot(mix, wo_ref[...], preferred_element_type=F32)
        r = DN_ALPHA * x + y
        mu = jnp.mean(r, axis=-1, keepdims=True)
        rc = r - mu
        var = jnp.mean(rc * rc, axis=-1, keepdims=True)
        o_ref[rows, :] = rc * lax.rsqrt(var + LN_EPS) * lng_ref[...] + lnb_ref[...]


def _out_block(x2, ya2, yb2, wz, wg, bg, wpa, wpb, wo, lng, lnb, tm, sub):
    n, d = x2.shape
    const = lambda a: pl.BlockSpec(a.shape, lambda i: (0, 0), pipeline_mode=pl.Buffered(1))
    row = lambda w: pl.BlockSpec((tm, w), lambda i: (i, 0))
    return pl.pallas_call(
        functools.partial(_out_kernel, sub=sub),
        grid=(n // tm,),
        in_specs=[row(d), row(Q_WIDTH), row(Q_WIDTH),
                  const(wz), const(wg), const(bg), const(wpa), const(wpb), const(wo), const(lng), const(lnb)],
        out_specs=row(d),
        out_shape=jax.ShapeDtypeStruct((n, d), x2.dtype),
        compiler_params=pltpu.CompilerParams(
            dimension_semantics=("parallel",), vmem_limit_bytes=VMEM_LIMIT),
        name="out_block",
    )(x2, ya2, yb2, wz, wg, bg, wpa, wpb, wo, lng, lnb)


def _rope_tables(seq):
    half = HEAD_DIM // 2
    axis_pairs = half // 2
    pos = np.arange(seq)
    freqs = ROPE_THETA ** (-np.arange(axis_pairs, dtype=np.float64) / axis_pairs)
    ang = np.concatenate([(pos // GRID_W)[:, None] * freqs, (pos % GRID_W)[:, None] * freqs], axis=-1)
    cos, sin = np.cos(ang), np.sin(ang)
    cosk = np.tile(cos, (1, KV_WIDTH // half))
    sink = np.tile(np.concatenate([-sin, sin], axis=-1), (1, KV_HEADS))
    return tuple(jnp.asarray(a, F32) for a in (cos.T, sin.T, cosk, sink))


def _layer(x, layer, w_in, b_gate, sink_a, qnorm_b, knorm_b, w_proj_a, w_proj_b, w_out, ln_g, ln_b):
    bsz, seq, d = x.shape
    tm_qkv, sub_qkv, tm_out, sub_out = 2048, MXU_TILE, 1024, MXU_TILE
    tq, kc = 2 * MXU_TILE, MXU_TILE

    wt, wk = _prep_qkv_weights(w_in, layer)
    gq = qnorm_b[layer].astype(F32)[None, :]
    gk = knorm_b[layer].astype(F32)[None, :]
    cosT, sinT, cosk, sink = _rope_tables(seq)

    qaT, ka, vaT, qbT, kb, vbT = _qkv_proj(x, wt, wk, gq, gk, cosT, sinT, cosk, sink, tm_qkv, sub_qkv)
    ya = _attn_window(sink_a[layer].astype(F32), qaT, ka, vaT)
    yb, (wz, wg, wpa, wpb, wo) = _attn_global(qbT, kb, vbT, w_in, w_proj_a, w_proj_b, w_out, layer, tq, kc)

    out = _out_block(
        x.reshape(bsz * seq, d), ya.reshape(bsz * seq, Q_WIDTH), yb.reshape(bsz * seq, Q_WIDTH),
        wz, wg, b_gate[layer].astype(F32)[None, :], wpa, wpb, wo,
        ln_g[layer].astype(F32)[None, :], ln_b[layer].astype(F32)[None, :], tm_out, sub_out)
    return out.reshape(bsz, seq, d)


def kernel(x, w_in, b_gate, sink_a, qnorm_b, knorm_b, w_proj_a, w_proj_b, w_out, ln_g, ln_b):
    h = x
    for layer in range(DEPTH):
        h = _layer(h, layer, w_in, b_gate, sink_a, qnorm_b, knorm_b, w_proj_a, w_proj_b, w_out, ln_g, ln_b)
    return h
```

```python
import functools

import jax
import jax.numpy as jnp
import numpy as np
from jax import lax
from jax.experimental import pallas as pl
from jax.experimental.pallas import tpu as pltpu

D_MODEL = 1024
HEAD_DIM = 64
HEADS = 8
KV_HEADS = 2
GROUP = HEADS // KV_HEADS
Q_WIDTH = HEADS * HEAD_DIM
KV_WIDTH = KV_HEADS * HEAD_DIM
WINDOW = 128
BLOCK = 128
GRID_W = 64
ROPE_THETA = 10000.0
QK_EPS = 1e-6
LN_EPS = 1e-5
DEPTH = 1
DN_ALPHA = (2.0 * DEPTH) ** 0.25
SCALE = HEAD_DIM ** -0.5
LOG2E = 1.4426950408889634

_SPLITS = (Q_WIDTH, KV_WIDTH, KV_WIDTH, Q_WIDTH, Q_WIDTH, KV_WIDTH, KV_WIDTH, Q_WIDTH, D_MODEL, D_MODEL)
_OFF = tuple(int(v) for v in np.cumsum((0,) + _SPLITS))
(QA, KA, VA, ZA, QB, KB, VB, ZB, GA, GB) = range(10)

MXU_TILE = 256
F32_SUBLANES = 8
BF16_SUBLANES = 16
VMEM_BYTES = 64 * 1024 * 1024
VMEM_LIMIT = VMEM_BYTES * 7 // 8

V_AUG = HEAD_DIM + BF16_SUBLANES

BF16 = jnp.bfloat16
F32 = jnp.float32


def _cols(w, *ids):
    return jnp.concatenate([w[:, _OFF[i]:_OFF[i + 1]] for i in ids], axis=1)


def _prep_kernel(w_ref, wt_out, wk_out):
    w = w_ref[0]
    wt_out[...] = _cols(w, QA, VA, QB, VB).T.astype(BF16)
    wk_out[...] = _cols(w, KA, KB).astype(BF16)


def _prep_qkv_weights(w_in, layer, steps=4):
    d = w_in.shape[1]
    t_width = 2 * (Q_WIDTH + KV_WIDTH)
    return pl.pallas_call(
        _prep_kernel,
        grid=(steps,),
        in_specs=[pl.BlockSpec((1, d // steps, _OFF[ZB]), lambda i: (layer, i, 0))],
        out_specs=[pl.BlockSpec((t_width, d // steps), lambda i: (0, i)),
                   pl.BlockSpec((d // steps, 2 * KV_WIDTH), lambda i: (i, 0))],
        out_shape=[jax.ShapeDtypeStruct((t_width, d), BF16), jax.ShapeDtypeStruct((d, 2 * KV_WIDTH), BF16)],
        compiler_params=pltpu.CompilerParams(
            dimension_semantics=("parallel",), vmem_limit_bytes=VMEM_LIMIT),
        name="prep_qkv_weights",
    )(w_in)


def _qkv_kernel(x_ref, wt_ref, wk_ref, gq_ref, gk_ref, cosT_ref, sinT_ref, cosk_ref, sink_ref,
                qaT_ref, ka_ref, vaT_ref, qbT_ref, kb_ref, vbT_ref, *, sub):
    half = HEAD_DIM // 2
    ones = jnp.ones((V_AUG - HEAD_DIM, sub), BF16)
    eye = (lax.broadcasted_iota(jnp.int32, (HEAD_DIM, HEAD_DIM), 0)
           == lax.broadcasted_iota(jnp.int32, (HEAD_DIM, HEAD_DIM), 1))
    gq = jnp.sum(jnp.where(eye, gq_ref[...], 0.0), axis=1, keepdims=True)[None]
    gk = jnp.concatenate([gk_ref[...]] * KV_HEADS, axis=1)

    for lo in range(0, x_ref.shape[1], sub):
        tok = slice(lo, lo + sub)
        xb = x_ref[0, tok, :].astype(BF16)
        t = lax.dot_general(wt_ref[...], xb, (((1,), (1,)), ((), ())), preferred_element_type=F32)
        kk = jnp.dot(xb, wk_ref[...], preferred_element_type=F32)

        def store_v(vT_ref, vT):
            for g in range(KV_HEADS):
                vT_ref[0, g, 0:HEAD_DIM, tok] = vT[g * HEAD_DIM:(g + 1) * HEAD_DIM].astype(BF16)
                vT_ref[0, g, HEAD_DIM:V_AUG, tok] = ones

        r0 = 0
        qaT_ref[0, :, tok] = (t[r0:r0 + Q_WIDTH] * (SCALE * LOG2E)).astype(BF16)
        r0 += Q_WIDTH
        store_v(vaT_ref, t[r0:r0 + KV_WIDTH])
        r0 += KV_WIDTH
        qb = t[r0:r0 + Q_WIDTH].reshape(HEADS, HEAD_DIM, sub)
        r0 += Q_WIDTH
        store_v(vbT_ref, t[r0:r0 + KV_WIDTH])

        ms = jnp.mean(qb * qb, axis=1, keepdims=True)
        qn = qb * lax.rsqrt(ms + QK_EPS) * gq
        q1, q2 = qn[:, :half], qn[:, half:]
        cosT, sinT = cosT_ref[:, tok][None], sinT_ref[:, tok][None]
        qr = jnp.concatenate([q1 * cosT - q2 * sinT, q2 * cosT + q1 * sinT], axis=1)
        qbT_ref[0, :, tok] = (qr * (SCALE * LOG2E)).reshape(Q_WIDTH, sub).astype(BF16)

        ka_ref[0, tok, :] = kk[:, :KV_WIDTH].astype(BF16)

        kb = kk[:, KV_WIDTH:]
        lane = lax.broadcasted_iota(jnp.int32, kb.shape, 1)
        k2 = kb * kb
        first = lane < HEAD_DIM
        ms0 = jnp.sum(jnp.where(first, k2, 0.0), axis=1, keepdims=True) * (1.0 / HEAD_DIM)
        ms1 = jnp.sum(jnp.where(first, 0.0, k2), axis=1, keepdims=True) * (1.0 / HEAD_DIM)
        kn = kb * lax.rsqrt(jnp.where(first, ms0, ms1) + QK_EPS) * gk
        swapped = jnp.where(lane % HEAD_DIM < half,
                            pltpu.roll(kn, KV_WIDTH - half, axis=1),
                            pltpu.roll(kn, half, axis=1))
        kb_ref[0, tok, :] = (kn * cosk_ref[tok, :] + swapped * sink_ref[tok, :]).astype(BF16)


def _qkv_proj(x, wt, wk, gq, gk, cosT, sinT, cosk, sink, tm, sub):
    bsz, seq, d = x.shape
    nt = seq // tm
    const = lambda shape: pl.BlockSpec(shape, lambda b, i: (0,) * len(shape))
    qT_spec = pl.BlockSpec((1, Q_WIDTH, tm), lambda b, i: (b, 0, i))
    k_spec = pl.BlockSpec((1, tm, KV_WIDTH), lambda b, i: (b, i, 0))
    vT_spec = pl.BlockSpec((1, KV_HEADS, V_AUG, tm), lambda b, i: (b, 0, 0, i))
    qT_shape = jax.ShapeDtypeStruct((bsz, Q_WIDTH, seq), BF16)
    k_shape = jax.ShapeDtypeStruct((bsz, seq, KV_WIDTH), BF16)
    vT_shape = jax.ShapeDtypeStruct((bsz, KV_HEADS, V_AUG, seq), BF16)
    return pl.pallas_call(
        functools.partial(_qkv_kernel, sub=sub),
        grid=(bsz, nt),
        in_specs=[
            pl.BlockSpec((1, tm, d), lambda b, i: (b, i, 0)),
            const(wt.shape), const(wk.shape), const(gq.shape), const(gk.shape),
            pl.BlockSpec((HEAD_DIM // 2, tm), lambda b, i: (0, i)),
            pl.BlockSpec((HEAD_DIM // 2, tm), lambda b, i: (0, i)),
            pl.BlockSpec((tm, KV_WIDTH), lambda b, i: (i, 0)),
            pl.BlockSpec((tm, KV_WIDTH), lambda b, i: (i, 0)),
        ],
        out_specs=[qT_spec, k_spec, vT_spec, qT_spec, k_spec, vT_spec],
        out_shape=[qT_shape, k_shape, vT_shape, qT_shape, k_shape, vT_shape],
        compiler_params=pltpu.CompilerParams(
            dimension_semantics=("parallel", "parallel"), vmem_limit_bytes=VMEM_LIMIT),
        name="qkv_proj",
    )(x, wt, wk, gq, gk, cosT, sinT, cosk, sink)


def _padded_q(qT_h, g):
    zeros = jnp.zeros_like(qT_h)
    parts = [zeros] * KV_HEADS
    parts[g] = qT_h
    return jnp.concatenate(parts, axis=0)


def _aligned(v, m):
    return v if isinstance(v, int) else pl.multiple_of(v, m)


def _fold_max(s):
    rows, n = s.shape
    while rows % (2 * F32_SUBLANES) == 0:
        rows //= 2
        s = jnp.maximum(s[:rows], s[rows:])
    return s if rows == F32_SUBLANES else jnp.max(s.reshape(rows // F32_SUBLANES, F32_SUBLANES, n), axis=0)


def _normalise(oT, extra=None):
    denom = oT[HEAD_DIM:HEAD_DIM + 1]
    if extra is not None:
        denom = denom + extra
    return oT[:HEAD_DIM] * (1.0 / denom)


def _alibi_table():
    r = np.arange(5 * BLOCK)[:, None]
    c = np.arange(BLOCK)[None, :]
    dist = np.abs(r - 2 * BLOCK - c).astype(np.float64)
    slopes = 2.0 ** (-8.0 * (np.arange(HEADS) + 1.0) / HEADS) * LOG2E
    bias = np.where(dist <= WINDOW, -slopes[:, None, None] * dist, -np.inf)
    return jnp.asarray(bias, F32)


def _attn_kernel(sink_ref, bias_ref, qaT_ref, ka_ref, vaT_ref, qbT_ref, kb_ref, vbT_ref,
                 w_ref, wpa_ref, wpb_ref, wo_ref,
                 ya_ref, yb_ref, wz_out, wg_out, wpa_out, wpb_out, wo_out, sa_ref, sb_ref, *, tq, kc):
    seq = kb_ref.shape[1]
    gw = GROUP * HEAD_DIM
    n_tiles = seq // tq
    chunks = range(0, seq, kc)
    nblk = seq // BLOCK
    span = 3 * BLOCK
    gn = GROUP * BLOCK
    a_steps = HEADS // KV_HEADS
    assert nblk == a_steps * n_tiles

    def sink_row(g):
        return jnp.concatenate(
            [jnp.full((1, BLOCK), sink_ref[g * GROUP + j] * LOG2E, F32) for j in range(GROUP)], axis=1)

    def window(n):
        if isinstance(n, int):
            q_lo, ws = n * BLOCK, min(max((n - 1) * BLOCK, 0), seq - span)
            return q_lo, ws, 2 * BLOCK - (q_lo - ws)
        q_lo = pl.multiple_of(n * BLOCK, BLOCK)
        ws = pl.multiple_of(jnp.clip((n - 1) * BLOCK, 0, seq - span), BLOCK)
        return q_lo, ws, pl.multiple_of(2 * BLOCK - (q_lo - ws), BLOCK)

    def a_scores(n, slot, g):
        q_lo, ws, b_lo = window(n)
        qT = qaT_ref[0, g * gw:(g + 1) * gw, pl.ds(q_lo, BLOCK)]
        w = jnp.concatenate(
            [_padded_q(qT[j * HEAD_DIM:(j + 1) * HEAD_DIM], g) for j in range(GROUP)], axis=1)
        sT = jnp.dot(ka_ref[0, pl.ds(ws, span), :], w, preferred_element_type=F32)
        ms = []
        for j in range(GROUP):
            s = sT[:, j * BLOCK:(j + 1) * BLOCK] + bias_ref[g * GROUP + j, pl.ds(b_lo, span), :]
            sa_ref[slot, :, g * gn + j * BLOCK:g * gn + (j + 1) * BLOCK] = s
            ms.append(jnp.max(_fold_max(s), axis=0, keepdims=True))
        return jnp.maximum(jnp.concatenate(ms, axis=1), sink_row(g))

    def a_values(n, slot, g, m):
        _, ws, _ = window(n)
        pT = jnp.exp2(sa_ref[slot, :, g * gn:(g + 1) * gn] - m).astype(BF16)
        oT = jnp.dot(vaT_ref[0, g, :, pl.ds(ws, span)], pT, preferred_element_type=F32)
        o = _normalise(oT, jnp.exp2(sink_row(g) - m))
        return [o[:, j * BLOCK:(j + 1) * BLOCK] for j in range(GROUP)]

    def a_item(n, slot, g, m_prev, outs, do_scores=True):
        m = a_scores(n, slot, g) if do_scores else None
        outs = outs + a_values(n - 1, 1 - slot, g, m_prev)
        if g == KV_HEADS - 1:
            q_lo, _, _ = window(n - 1)
            ya_ref[0, pl.ds(q_lo, BLOCK), :] = jnp.concatenate(outs, axis=0).T.astype(ya_ref.dtype)
            outs = []
        return m, outs

    def padded_q(t, h):
        qT_h = qbT_ref[0, h * HEAD_DIM:(h + 1) * HEAD_DIM, pl.ds(_aligned(t * tq, tq), tq)]
        return _padded_q(qT_h, h // GROUP)

    def scores_chunk(slot, w, k_lo, m8):
        sT = jnp.dot(kb_ref[0, k_lo:k_lo + kc, :], w, preferred_element_type=F32)
        sb_ref[slot, k_lo:k_lo + kc, :] = sT
        m8_c = _fold_max(sT)
        return m8_c if m8 is None else jnp.maximum(m8, m8_c)

    def values_chunk(g, slot, m, k_lo, acc):
        pT = jnp.exp2(sb_ref[slot, k_lo:k_lo + kc, :] - m).astype(BF16)
        oT = jnp.dot(vbT_ref[0, g, :, k_lo:k_lo + kc], pT, preferred_element_type=F32)
        return oT if acc is None else acc + oT

    def cast_weight_rows(t):
        def rows(ref):
            n = ref.shape[-2] // n_tiles
            return pl.ds(_aligned(t * n, n), n)

        w = w_ref[0, rows(w_ref), :]
        wz_out[rows(wz_out), :] = _cols(w, ZA, ZB).astype(BF16)
        wg_out[rows(wg_out), :] = _cols(w, GA, GB).astype(BF16)
        wpa_out[rows(wpa_out), :] = wpa_ref[0, rows(wpa_ref), :].astype(BF16)
        wpb_out[rows(wpb_out), :] = wpb_ref[0, rows(wpb_ref), :].astype(BF16)
        wo_out[rows(wo_out), :] = wo_ref[0, rows(wo_ref), :].astype(BF16)

    def fill():
        w, m8 = padded_q(0, 0), None
        for k_lo in chunks:
            m8 = scores_chunk(0, w, k_lo, m8)
        return (jnp.max(m8, axis=0, keepdims=True),) + tuple(a_scores(0, 0, g) for g in range(KV_HEADS))

    def tile(t, carry, last=False):
        m, ma = carry[0], list(carry[1:])
        cast_weight_rows(t)
        outs, a_outs, ma_next = [], [], [None] * KV_HEADS
        for h in range(HEADS):
            nxt = (t, h + 1) if h + 1 < HEADS else (None if last else (t + 1, 0))
            w = padded_q(*nxt) if nxt else None
            m8 = acc = None
            for k_lo in chunks:
                if nxt:
                    m8 = scores_chunk((h + 1) % 2, w, k_lo, m8)
                acc = values_chunk(h // GROUP, h % 2, m, k_lo, acc)
            outs.append(_normalise(acc))
            m = jnp.max(m8, axis=0, keepdims=True) if nxt else None
            if h % GROUP == GROUP - 1:
                g = h // GROUP
                yb_ref[0, pl.ds(_aligned(t * tq, tq), tq), g * gw:(g + 1) * gw] = (
                    jnp.concatenate(outs, axis=0).T.astype(yb_ref.dtype))
                outs = []

            i, ga = h // KV_HEADS, h % KV_HEADS
            n = a_steps * t + i + 1
            a_has_scores = not (last and i == a_steps - 1)
            ma_next[ga], a_outs = a_item(n, (i + 1) % 2, ga, ma[ga], a_outs, a_has_scores)
            if ga == KV_HEADS - 1:
                ma, ma_next = ma_next, [None] * KV_HEADS
        return None if last else (m,) + tuple(ma)

    carry = fill()
    carry = lax.fori_loop(0, n_tiles - 1, tile, carry)
    tile(n_tiles - 1, carry, last=True)


def _attention(sink, qaT, ka, vaT, qbT, kb, vbT, w_in, w_proj_a, w_proj_b, w_out, layer, tq, kc):
    bsz, _, seq = qbT.shape
    d = w_in.shape[1]
    bias = _alibi_table()
    qT_spec = pl.BlockSpec((1, Q_WIDTH, seq), lambda b: (b, 0, 0))
    k_spec = pl.BlockSpec((1, seq, KV_WIDTH), lambda b: (b, 0, 0))
    vT_spec = pl.BlockSpec((1, KV_HEADS, V_AUG, seq), lambda b: (b, 0, 0, 0))
    y_spec = pl.BlockSpec((1, seq, Q_WIDTH), lambda b: (b, 0, 0))
    y_shape = jax.ShapeDtypeStruct((bsz, seq, Q_WIDTH), BF16)
    slab_in = lambda a: pl.BlockSpec((1, a.shape[1] // bsz, a.shape[2]), lambda b: (layer, b, 0))
    slab_out = lambda rows, width: pl.BlockSpec((rows // bsz, width), lambda b: (b, 0))
    w_shapes = [(d, 2 * Q_WIDTH), (d, 2 * D_MODEL), w_proj_a.shape[1:], w_proj_b.shape[1:], w_out.shape[1:]]
    outs = pl.pallas_call(
        functools.partial(_attn_kernel, tq=tq, kc=kc),
        grid=(bsz,),
        in_specs=[
            pl.BlockSpec(memory_space=pltpu.SMEM),
            pl.BlockSpec(bias.shape, lambda b: (0, 0, 0)),
            qT_spec, k_spec, vT_spec, qT_spec, k_spec, vT_spec,
            slab_in(w_in), slab_in(w_proj_a), slab_in(w_proj_b), slab_in(w_out),
        ],
        out_specs=[y_spec, y_spec] + [slab_out(*s) for s in w_shapes],
        out_shape=[y_shape, y_shape] + [jax.ShapeDtypeStruct(s, BF16) for s in w_shapes],
        scratch_shapes=[pltpu.VMEM((2, 3 * BLOCK, HEADS * BLOCK), F32), pltpu.VMEM((2, seq, tq), F32)],
        compiler_params=pltpu.CompilerParams(
            dimension_semantics=("parallel",), vmem_limit_bytes=VMEM_LIMIT),
        name="attention",
    )(sink, bias, qaT, ka, vaT, qbT, kb, vbT, w_in, w_proj_a, w_proj_b, w_out)
    return outs[0], outs[1], outs[2:]


def _out_kernel(x_ref, ya_ref, yb_ref, wz_ref, wg_ref, bg_ref, wpa_ref, wpb_ref, wo_ref, lng_ref, lnb_ref,
                o_ref, *, sub):
    for lo in range(0, x_ref.shape[0], sub):
        rows = slice(lo, lo + sub)
        x = x_ref[rows, :]
        xb = x.astype(BF16)
        z = jnp.dot(xb, wz_ref[...], preferred_element_type=F32)
        gl = jnp.dot(xb, wg_ref[...], preferred_element_type=F32)
        sz = z * jax.nn.sigmoid(z)
        ua = (ya_ref[rows, :].astype(F32) * sz[:, :Q_WIDTH]).astype(BF16)
        ub = (yb_ref[rows, :].astype(F32) * sz[:, Q_WIDTH:]).astype(BF16)
        pa = jnp.dot(ua, wpa_ref[...], preferred_element_type=F32)
        pb = jnp.dot(ub, wpb_ref[...], preferred_element_type=F32)
        gate = jax.nn.sigmoid(gl + bg_ref[...])
        mix = (gate[:, :D_MODEL] * pa + gate[:, D_MODEL:] * pb).astype(BF16)
        y = jnp.dot(mix, wo_ref[...], preferred_element_type=F32)
        r = DN_ALPHA * x + y
        mu = jnp.mean(r, axis=-1, keepdims=True)
        rc = r - mu
        var = jnp.mean(rc * rc, axis=-1, keepdims=True)
        o_ref[rows, :] = rc * lax.rsqrt(var + LN_EPS) * lng_ref[...] + lnb_ref[...]


def _out_block(x2, ya2, yb2, wz, wg, bg, wpa, wpb, wo, lng, lnb, tm, sub):
    n, d = x2.shape
    const = lambda a: pl.BlockSpec(a.shape, lambda i: (0, 0), pipeline_mode=pl.Buffered(1))
    row = lambda w: pl.BlockSpec((tm, w), lambda i: (i, 0))
    return pl.pallas_call(
        functools.partial(_out_kernel, sub=sub),
        grid=(n // tm,),
        in_specs=[row(d), row(Q_WIDTH), row(Q_WIDTH),
                  const(wz), const(wg), const(bg), const(wpa), const(wpb), const(wo), const(lng), const(lnb)],
        out_specs=row(d),
        out_shape=jax.ShapeDtypeStruct((n, d), x2.dtype),
        compiler_params=pltpu.CompilerParams(
            dimension_semantics=("parallel",), vmem_limit_bytes=VMEM_LIMIT),
        name="out_block",
    )(x2, ya2, yb2, wz, wg, bg, wpa, wpb, wo, lng, lnb)


def _rope_tables(seq):
    half = HEAD_DIM // 2
    axis_pairs = half // 2
    pos = np.arange(seq)
    freqs = ROPE_THETA ** (-np.arange(axis_pairs, dtype=np.float64) / axis_pairs)
    ang = np.concatenate([(pos // GRID_W)[:, None] * freqs, (pos % GRID_W)[:, None] * freqs], axis=-1)
    cos, sin = np.cos(ang), np.sin(ang)
    cosk = np.tile(cos, (1, KV_WIDTH // half))
    sink = np.tile(np.concatenate([-sin, sin], axis=-1), (1, KV_HEADS))
    return tuple(jnp.asarray(a, F32) for a in (cos.T, sin.T, cosk, sink))


def _layer(x, layer, w_in, b_gate, sink_a, qnorm_b, knorm_b, w_proj_a, w_proj_b, w_out, ln_g, ln_b):
    bsz, seq, d = x.shape
    tm_qkv, sub_qkv, tm_out, sub_out = 2048, MXU_TILE, 1024, MXU_TILE
    tq, kc = 2 * MXU_TILE, MXU_TILE

    wt, wk = _prep_qkv_weights(w_in, layer)
    gq = qnorm_b[layer].astype(F32)[None, :]
    gk = knorm_b[layer].astype(F32)[None, :]
    cosT, sinT, cosk, sink = _rope_tables(seq)

    qaT, ka, vaT, qbT, kb, vbT = _qkv_proj(x, wt, wk, gq, gk, cosT, sinT, cosk, sink, tm_qkv, sub_qkv)
    ya, yb, (wz, wg, wpa, wpb, wo) = _attention(
        sink_a[layer].astype(F32), qaT, ka, vaT, qbT, kb, vbT, w_in, w_proj_a, w_proj_b, w_out, layer, tq, kc)

    out = _out_block(
        x.reshape(bsz * seq, d), ya.reshape(bsz * seq, Q_WIDTH), yb.reshape(bsz * seq, Q_WIDTH),
        wz, wg, b_gate[layer].astype(F32)[None, :], wpa, wpb, wo,
        ln_g[layer].astype(F32)[None, :], ln_b[layer].astype(F32)[None, :], tm_out, sub_out)
    return out.reshape(bsz, seq, d)


def kernel(x, w_in, b_gate, sink_a, qnorm_b, knorm_b, w_proj_a, w_proj_b, w_out, ln_g, ln_b):
    h = x
    for layer in range(DEPTH):
        h = _layer(h, layer, w_in, b_gate, sink_a, qnorm_b, knorm_b, w_proj_a, w_proj_b, w_out, ln_g, ln_b)
    return h
```

```python
import functools

import jax
import jax.numpy as jnp
import numpy as np
from jax import lax
from jax.experimental import pallas as pl
from jax.experimental.pallas import tpu as pltpu

D_MODEL = 1024
HEAD_DIM = 64
HEADS = 8
KV_HEADS = 2
GROUP = HEADS // KV_HEADS
Q_WIDTH = HEADS * HEAD_DIM
KV_WIDTH = KV_HEADS * HEAD_DIM
WINDOW = 128
BLOCK = 128
GRID_W = 64
ROPE_THETA = 10000.0
QK_EPS = 1e-6
LN_EPS = 1e-5
DEPTH = 1
DN_ALPHA = (2.0 * DEPTH) ** 0.25
SCALE = HEAD_DIM ** -0.5
LOG2E = 1.4426950408889634

_SPLITS = (Q_WIDTH, KV_WIDTH, KV_WIDTH, Q_WIDTH, Q_WIDTH, KV_WIDTH, KV_WIDTH, Q_WIDTH, D_MODEL, D_MODEL)
_OFF = tuple(int(v) for v in np.cumsum((0,) + _SPLITS))
(QA, KA, VA, ZA, QB, KB, VB, ZB, GA, GB) = range(10)

MXU_TILE = 256
F32_SUBLANES = 8
BF16_SUBLANES = 16
VMEM_BYTES = 64 * 1024 * 1024
VMEM_LIMIT = VMEM_BYTES * 7 // 8

V_AUG = HEAD_DIM + BF16_SUBLANES

BF16 = jnp.bfloat16
F32 = jnp.float32


def _cols(w, *ids):
    return jnp.concatenate([w[:, _OFF[i]:_OFF[i + 1]] for i in ids], axis=1)


def _prep_kernel(w_ref, wt_out, wk_out):
    w = w_ref[0]
    wt_out[...] = _cols(w, QA, VA, QB, VB).T.astype(BF16)
    wk_out[...] = _cols(w, KA, KB).astype(BF16)


def _prep_qkv_weights(w_in, layer, steps=4):
    d = w_in.shape[1]
    t_width = 2 * (Q_WIDTH + KV_WIDTH)
    return pl.pallas_call(
        _prep_kernel,
        grid=(steps,),
        in_specs=[pl.BlockSpec((1, d // steps, _OFF[ZB]), lambda i: (layer, i, 0))],
        out_specs=[pl.BlockSpec((t_width, d // steps), lambda i: (0, i)),
                   pl.BlockSpec((d // steps, 2 * KV_WIDTH), lambda i: (i, 0))],
        out_shape=[jax.ShapeDtypeStruct((t_width, d), BF16), jax.ShapeDtypeStruct((d, 2 * KV_WIDTH), BF16)],
        compiler_params=pltpu.CompilerParams(
            dimension_semantics=("parallel",), vmem_limit_bytes=VMEM_LIMIT),
        name="prep_qkv_weights",
    )(w_in)


def _qkv_kernel(x_ref, wt_ref, wk_ref, gq_ref, gk_ref, cosT_ref, sinT_ref, cosk_ref, sink_ref,
                qaT_ref, ka_ref, vaT_ref, qbT_ref, kb_ref, vbT_ref, *, sub):
    half = HEAD_DIM // 2
    ones = jnp.ones((V_AUG - HEAD_DIM, sub), BF16)
    eye = (lax.broadcasted_iota(jnp.int32, (HEAD_DIM, HEAD_DIM), 0)
           == lax.broadcasted_iota(jnp.int32, (HEAD_DIM, HEAD_DIM), 1))
    gq = jnp.sum(jnp.where(eye, gq_ref[...], 0.0), axis=1, keepdims=True)[None]
    gk = jnp.concatenate([gk_ref[...]] * KV_HEADS, axis=1)

    for lo in range(0, x_ref.shape[1], sub):
        tok = slice(lo, lo + sub)
        xb = x_ref[0, tok, :].astype(BF16)
        t = lax.dot_general(wt_ref[...], xb, (((1,), (1,)), ((), ())), preferred_element_type=F32)
        kk = jnp.dot(xb, wk_ref[...], preferred_element_type=F32)

        def store_v(vT_ref, vT):
            for g in range(KV_HEADS):
                vT_ref[0, g, 0:HEAD_DIM, tok] = vT[g * HEAD_DIM:(g + 1) * HEAD_DIM].astype(BF16)
                vT_ref[0, g, HEAD_DIM:V_AUG, tok] = ones

        r0 = 0
        qaT_ref[0, :, tok] = (t[r0:r0 + Q_WIDTH] * (SCALE * LOG2E)).astype(BF16)
        r0 += Q_WIDTH
        store_v(vaT_ref, t[r0:r0 + KV_WIDTH])
        r0 += KV_WIDTH
        qb = t[r0:r0 + Q_WIDTH].reshape(HEADS, HEAD_DIM, sub)
        r0 += Q_WIDTH
        store_v(vbT_ref, t[r0:r0 + KV_WIDTH])

        ms = jnp.mean(qb * qb, axis=1, keepdims=True)
        qn = qb * lax.rsqrt(ms + QK_EPS) * gq
        q1, q2 = qn[:, :half], qn[:, half:]
        cosT, sinT = cosT_ref[:, tok][None], sinT_ref[:, tok][None]
        qr = jnp.concatenate([q1 * cosT - q2 * sinT, q2 * cosT + q1 * sinT], axis=1)
        qbT_ref[0, :, tok] = (qr * (SCALE * LOG2E)).reshape(Q_WIDTH, sub).astype(BF16)

        ka_ref[0, tok, :] = kk[:, :KV_WIDTH].astype(BF16)

        kb = kk[:, KV_WIDTH:]
        lane = lax.broadcasted_iota(jnp.int32, kb.shape, 1)
        k2 = kb * kb
        first = lane < HEAD_DIM
        ms0 = jnp.sum(jnp.where(first, k2, 0.0), axis=1, keepdims=True) * (1.0 / HEAD_DIM)
        ms1 = jnp.sum(jnp.where(first, 0.0, k2), axis=1, keepdims=True) * (1.0 / HEAD_DIM)
        kn = kb * lax.rsqrt(jnp.where(first, ms0, ms1) + QK_EPS) * gk
        swapped = jnp.where(lane % HEAD_DIM < half,
                            pltpu.roll(kn, KV_WIDTH - half, axis=1),
                            pltpu.roll(kn, half, axis=1))
        kb_ref[0, tok, :] = (kn * cosk_ref[tok, :] + swapped * sink_ref[tok, :]).astype(BF16)


def _qkv_proj(x, wt, wk, gq, gk, cosT, sinT, cosk, sink, tm, sub):
    bsz, seq, d = x.shape
    nt = seq // tm
    const = lambda shape: pl.BlockSpec(shape, lambda b, i: (0,) * len(shape))
    qT_spec = pl.BlockSpec((1, Q_WIDTH, tm), lambda b, i: (b, 0, i))
    k_spec = pl.BlockSpec((1, tm, KV_WIDTH), lambda b, i: (b, i, 0))
    vT_spec = pl.BlockSpec((1, KV_HEADS, V_AUG, tm), lambda b, i: (b, 0, 0, i))
    qT_shape = jax.ShapeDtypeStruct((bsz, Q_WIDTH, seq), BF16)
    k_shape = jax.ShapeDtypeStruct((bsz, seq, KV_WIDTH), BF16)
    vT_shape = jax.ShapeDtypeStruct((bsz, KV_HEADS, V_AUG, seq), BF16)
    return pl.pallas_call(
        functools.partial(_qkv_kernel, sub=sub),
        grid=(bsz, nt),
        in_specs=[
            pl.BlockSpec((1, tm, d), lambda b, i: (b, i, 0)),
            const(wt.shape), const(wk.shape), const(gq.shape), const(gk.shape),
            pl.BlockSpec((HEAD_DIM // 2, tm), lambda b, i: (0, i)),
            pl.BlockSpec((HEAD_DIM // 2, tm), lambda b, i: (0, i)),
            pl.BlockSpec((tm, KV_WIDTH), lambda b, i: (i, 0)),
            pl.BlockSpec((tm, KV_WIDTH), lambda b, i: (i, 0)),
        ],
        out_specs=[qT_spec, k_spec, vT_spec, qT_spec, k_spec, vT_spec],
        out_shape=[qT_shape, k_shape, vT_shape, qT_shape, k_shape, vT_shape],
        compiler_params=pltpu.CompilerParams(
            dimension_semantics=("parallel", "parallel"), vmem_limit_bytes=VMEM_LIMIT),
        name="qkv_proj",
    )(x, wt, wk, gq, gk, cosT, sinT, cosk, sink)


def _padded_q(qT_h, g):
    zeros = jnp.zeros_like(qT_h)
    parts = [zeros] * KV_HEADS
    parts[g] = qT_h
    return jnp.concatenate(parts, axis=0)


def _aligned(v, m):
    return v if isinstance(v, int) else pl.multiple_of(v, m)


def _fold_max(s):
    rows, n = s.shape
    while rows % (2 * F32_SUBLANES) == 0:
        rows //= 2
        s = jnp.maximum(s[:rows], s[rows:])
    return s if rows == F32_SUBLANES else jnp.max(s.reshape(rows // F32_SUBLANES, F32_SUBLANES, n), axis=0)


def _normalise(oT, extra=None):
    denom = oT[HEAD_DIM:HEAD_DIM + 1]
    if extra is not None:
        denom = denom + extra
    return oT[:HEAD_DIM] * (1.0 / denom)


def _alibi_table():
    r = np.arange(5 * BLOCK)[:, None]
    c = np.arange(BLOCK)[None, :]
    dist = np.abs(r - 2 * BLOCK - c).astype(np.float64)
    slopes = 2.0 ** (-8.0 * (np.arange(HEADS) + 1.0) / HEADS) * LOG2E
    bias = np.where(dist <= WINDOW, -slopes[:, None, None] * dist, -np.inf)
    return jnp.asarray(bias, F32)


def _attn_kernel(sink_ref, bias_ref, qaT_ref, ka_ref, vaT_ref, qbT_ref, kb_ref, vbT_ref,
                 w_ref, wpa_ref, wpb_ref, wo_ref,
                 ya_ref, yb_ref, wz_out, wg_out, wpa_out, wpb_out, wo_out, sa_ref, sb_ref, *, tq, kc):
    seq = kb_ref.shape[1]
    gw = GROUP * HEAD_DIM
    n_tiles = seq // tq
    chunks = range(0, seq, kc)
    nblk = seq // BLOCK
    span = 3 * BLOCK
    gn = GROUP * BLOCK
    a_steps = HEADS // KV_HEADS
    assert nblk == a_steps * n_tiles

    def sink_row(g):
        return jnp.concatenate(
            [jnp.full((1, BLOCK), sink_ref[g * GROUP + j] * LOG2E, F32) for j in range(GROUP)], axis=1)

    def window(n):
        if isinstance(n, int):
            q_lo, ws = n * BLOCK, min(max((n - 1) * BLOCK, 0), seq - span)
            return q_lo, ws, 2 * BLOCK - (q_lo - ws)
        q_lo = pl.multiple_of(n * BLOCK, BLOCK)
        ws = pl.multiple_of(jnp.clip((n - 1) * BLOCK, 0, seq - span), BLOCK)
        return q_lo, ws, pl.multiple_of(2 * BLOCK - (q_lo - ws), BLOCK)

    def a_scores(n, slot, g):
        q_lo, ws, b_lo = window(n)
        qT = qaT_ref[0, g * gw:(g + 1) * gw, pl.ds(q_lo, BLOCK)]
        w = jnp.concatenate(
            [_padded_q(qT[j * HEAD_DIM:(j + 1) * HEAD_DIM], g) for j in range(GROUP)], axis=1)
        sT = jnp.dot(ka_ref[0, pl.ds(ws, span), :], w, preferred_element_type=F32)
        ms = []
        for j in range(GROUP):
            s = sT[:, j * BLOCK:(j + 1) * BLOCK] + bias_ref[g * GROUP + j, pl.ds(b_lo, span), :]
            sa_ref[slot, :, g * gn + j * BLOCK:g * gn + (j + 1) * BLOCK] = s
            ms.append(jnp.max(_fold_max(s), axis=0, keepdims=True))
        return jnp.maximum(jnp.concatenate(ms, axis=1), sink_row(g))

    def a_values(n, slot, g, m):
        _, ws, _ = window(n)
        pT = jnp.exp2(sa_ref[slot, :, g * gn:(g + 1) * gn] - m).astype(BF16)
        oT = jnp.dot(vaT_ref[0, g, :, pl.ds(ws, span)], pT, preferred_element_type=F32)
        o = _normalise(oT, jnp.exp2(sink_row(g) - m))
        return [o[:, j * BLOCK:(j + 1) * BLOCK] for j in range(GROUP)]

    def a_item(n, slot, g, m_prev, outs, do_scores=True):
        m = a_scores(n, slot, g) if do_scores else None
        outs = outs + a_values(n - 1, 1 - slot, g, m_prev)
        if g == KV_HEADS - 1:
            q_lo, _, _ = window(n - 1)
            ya_ref[0, pl.ds(q_lo, BLOCK), :] = jnp.concatenate(outs, axis=0).T.astype(ya_ref.dtype)
            outs = []
        return m, outs

    def padded_q(t, h):
        qT_h = qbT_ref[0, h * HEAD_DIM:(h + 1) * HEAD_DIM, pl.ds(_aligned(t * tq, tq), tq)]
        return _padded_q(qT_h, h // GROUP)

    def scores_chunk(slot, w, k_lo, m8):
        sT = jnp.dot(kb_ref[0, k_lo:k_lo + kc, :], w, preferred_element_type=F32)
        sb_ref[slot, k_lo:k_lo + kc, :] = sT
        m8_c = _fold_max(sT)
        return m8_c if m8 is None else jnp.maximum(m8, m8_c)

    def values_chunk(g, slot, m, k_lo, acc):
        pT = jnp.exp2(sb_ref[slot, k_lo:k_lo + kc, :] - m).astype(BF16)
        oT = jnp.dot(vbT_ref[0, g, :, k_lo:k_lo + kc], pT, preferred_element_type=F32)
        return oT if acc is None else acc + oT

    def cast_weight_rows(t):
        def rows(ref):
            n = ref.shape[-2] // n_tiles
            return pl.ds(_aligned(t * n, n), n)

        w = w_ref[0, rows(w_ref), :]
        wz_out[rows(wz_out), :] = _cols(w, ZA, ZB).astype(BF16)
        wg_out[rows(wg_out), :] = _cols(w, GA, GB).astype(BF16)
        wpa_out[rows(wpa_out), :] = wpa_ref[0, rows(wpa_ref), :].astype(BF16)
        wpb_out[rows(wpb_out), :] = wpb_ref[0, rows(wpb_ref), :].astype(BF16)
        wo_out[rows(wo_out), :] = wo_ref[0, rows(wo_ref), :].astype(BF16)

    def fill():
        w, m8 = padded_q(0, 0), None
        for k_lo in chunks:
            m8 = scores_chunk(0, w, k_lo, m8)
        return (jnp.max(m8, axis=0, keepdims=True),) + tuple(a_scores(0, 0, g) for g in range(KV_HEADS))

    def tile(t, carry, last=False):
        m, ma = carry[0], list(carry[1:])
        cast_weight_rows(t)
        outs, a_outs, ma_next = [], [], [None] * KV_HEADS
        for h in range(HEADS):
            nxt = (t, h + 1) if h + 1 < HEADS else (None if last else (t + 1, 0))
            w = padded_q(*nxt) if nxt else None
            m8 = acc = None
            for k_lo in chunks:
                if nxt:
                    m8 = scores_chunk((h + 1) % 2, w, k_lo, m8)
                acc = values_chunk(h // GROUP, h % 2, m, k_lo, acc)
            outs.append(_normalise(acc))
            m = jnp.max(m8, axis=0, keepdims=True) if nxt else None
            if h % GROUP == GROUP - 1:
                g = h // GROUP
                yb_ref[0, pl.ds(_aligned(t * tq, tq), tq), g * gw:(g + 1) * gw] = (
                    jnp.concatenate(outs, axis=0).T.astype(yb_ref.dtype))
                outs = []

            i, ga = h // KV_HEADS, h % KV_HEADS
            n = a_steps * t + i + 1
            a_has_scores = not (last and i == a_steps - 1)
            ma_next[ga], a_outs = a_item(n, (i + 1) % 2, ga, ma[ga], a_outs, a_has_scores)
            if ga == KV_HEADS - 1:
                ma, ma_next = ma_next, [None] * KV_HEADS
        return None if last else (m,) + tuple(ma)

    carry = fill()
    carry = lax.fori_loop(0, n_tiles - 1, tile, carry)
    tile(n_tiles - 1, carry, last=True)


def _attention(sink, qaT, ka, vaT, qbT, kb, vbT, w_in, w_proj_a, w_proj_b, w_out, layer, tq, kc):
    bsz, _, seq = qbT.shape
    d = w_in.shape[1]
    bias = _alibi_table()
    qT_spec = pl.BlockSpec((1, Q_WIDTH, seq), lambda b: (b, 0, 0))
    k_spec = pl.BlockSpec((1, seq, KV_WIDTH), lambda b: (b, 0, 0))
    vT_spec = pl.BlockSpec((1, KV_HEADS, V_AUG, seq), lambda b: (b, 0, 0, 0))
    y_spec = pl.BlockSpec((1, seq, Q_WIDTH), lambda b: (b, 0, 0))
    y_shape = jax.ShapeDtypeStruct((bsz, seq, Q_WIDTH), BF16)
    slab_in = lambda a: pl.BlockSpec((1, a.shape[1] // bsz, a.shape[2]), lambda b: (layer, b, 0))
    slab_out = lambda rows, width: pl.BlockSpec((rows // bsz, width), lambda b: (b, 0))
    w_shapes = [(d, 2 * Q_WIDTH), (d, 2 * D_MODEL), w_proj_a.shape[1:], w_proj_b.shape[1:], w_out.shape[1:]]
    outs = pl.pallas_call(
        functools.partial(_attn_kernel, tq=tq, kc=kc),
        grid=(bsz,),
        in_specs=[
            pl.BlockSpec(memory_space=pltpu.SMEM),
            pl.BlockSpec(bias.shape, lambda b: (0, 0, 0)),
            qT_spec, k_spec, vT_spec, qT_spec, k_spec, vT_spec,
            slab_in(w_in), slab_in(w_proj_a), slab_in(w_proj_b), slab_in(w_out),
        ],
        out_specs=[y_spec, y_spec] + [slab_out(*s) for s in w_shapes],
        out_shape=[y_shape, y_shape] + [jax.ShapeDtypeStruct(s, BF16) for s in w_shapes],
        scratch_shapes=[pltpu.VMEM((2, 3 * BLOCK, HEADS * BLOCK), F32), pltpu.VMEM((2, seq, tq), F32)],
        compiler_params=pltpu.CompilerParams(
            dimension_semantics=("parallel",), vmem_limit_bytes=VMEM_LIMIT),
        name="attention",
    )(sink, bias, qaT, ka, vaT, qbT, kb, vbT, w_in, w_proj_a, w_proj_b, w_out)
    return outs[0], outs[1], outs[2:]


def _out_kernel(x_ref, ya_ref, yb_ref, wz_ref, wg_ref, bg_ref, wpa_ref, wpb_ref, wo_ref, lng_ref, lnb_ref,
                o_ref, *, sub):
    def input_proj(rows):
        xb = x_ref[rows, :].astype(BF16)
        z = jnp.dot(xb, wz_ref[...], preferred_element_type=F32)
        gl = jnp.dot(xb, wg_ref[...], preferred_element_type=F32)
        return z, gl

    def branch_proj(rows, z):
        sz = z * jax.nn.sigmoid(z)
        ua = (ya_ref[rows, :].astype(F32) * sz[:, :Q_WIDTH]).astype(BF16)
        ub = (yb_ref[rows, :].astype(F32) * sz[:, Q_WIDTH:]).astype(BF16)
        pa = jnp.dot(ua, wpa_ref[...], preferred_element_type=F32)
        pb = jnp.dot(ub, wpb_ref[...], preferred_element_type=F32)
        return pa, pb

    def merge_norm(rows, gl, pa, pb):
        gate = jax.nn.sigmoid(gl + bg_ref[...])
        mix = (gate[:, :D_MODEL] * pa + gate[:, D_MODEL:] * pb).astype(BF16)
        y = jnp.dot(mix, wo_ref[...], preferred_element_type=F32)
        r = DN_ALPHA * x_ref[rows, :] + y
        mu = jnp.mean(r, axis=-1, keepdims=True)
        rc = r - mu
        var = jnp.mean(rc * rc, axis=-1, keepdims=True)
        o_ref[rows, :] = rc * lax.rsqrt(var + LN_EPS) * lng_ref[...] + lnb_ref[...]

    tiles = [slice(lo, lo + sub) for lo in range(0, x_ref.shape[0], sub)]
    stage1, stage2 = {}, {}
    for step in range(len(tiles) + 2):
        if step < len(tiles):
            stage1[step] = input_proj(tiles[step])
        if 0 <= step - 1 < len(tiles):
            stage2[step - 1] = branch_proj(tiles[step - 1], stage1[step - 1][0])
        if 0 <= step - 2 < len(tiles):
            merge_norm(tiles[step - 2], stage1.pop(step - 2)[1], *stage2.pop(step - 2))


def _out_block(x2, ya2, yb2, wz, wg, bg, wpa, wpb, wo, lng, lnb, tm, sub):
    n, d = x2.shape
    const = lambda a: pl.BlockSpec(a.shape, lambda i: (0, 0), pipeline_mode=pl.Buffered(1))
    row = lambda w: pl.BlockSpec((tm, w), lambda i: (i, 0))
    return pl.pallas_call(
        functools.partial(_out_kernel, sub=sub),
        grid=(n // tm,),
        in_specs=[row(d), row(Q_WIDTH), row(Q_WIDTH),
                  const(wz), const(wg), const(bg), const(wpa), const(wpb), const(wo), const(lng), const(lnb)],
        out_specs=row(d),
        out_shape=jax.ShapeDtypeStruct((n, d), x2.dtype),
        compiler_params=pltpu.CompilerParams(
            dimension_semantics=("parallel",), vmem_limit_bytes=VMEM_LIMIT),
        name="out_block",
    )(x2, ya2, yb2, wz, wg, bg, wpa, wpb, wo, lng, lnb)


def _rope_tables(seq):
    half = HEAD_DIM // 2
    axis_pairs = half // 2
    pos = np.arange(seq)
    freqs = ROPE_THETA ** (-np.arange(axis_pairs, dtype=np.float64) / axis_pairs)
    ang = np.concatenate([(pos // GRID_W)[:, None] * freqs, (pos % GRID_W)[:, None] * freqs], axis=-1)
    cos, sin = np.cos(ang), np.sin(ang)
    cosk = np.tile(cos, (1, KV_WIDTH // half))
    sink = np.tile(np.concatenate([-sin, sin], axis=-1), (1, KV_HEADS))
    return tuple(jnp.asarray(a, F32) for a in (cos.T, sin.T, cosk, sink))


def _layer(x, layer, w_in, b_gate, sink_a, qnorm_b, knorm_b, w_proj_a, w_proj_b, w_out, ln_g, ln_b):
    bsz, seq, d = x.shape
    tm_qkv, sub_qkv, tm_out, sub_out = 2048, MXU_TILE, 1024, MXU_TILE
    tq, kc = 2 * MXU_TILE, MXU_TILE

    wt, wk = _prep_qkv_weights(w_in, layer)
    gq = qnorm_b[layer].astype(F32)[None, :]
    gk = knorm_b[layer].astype(F32)[None, :]
    cosT, sinT, cosk, sink = _rope_tables(seq)

    qaT, ka, vaT, qbT, kb, vbT = _qkv_proj(x, wt, wk, gq, gk, cosT, sinT, cosk, sink, tm_qkv, sub_qkv)
    ya, yb, (wz, wg, wpa, wpb, wo) = _attention(
        sink_a[layer].astype(F32), qaT, ka, vaT, qbT, kb, vbT, w_in, w_proj_a, w_proj_b, w_out, layer, tq, kc)

    out = _out_block(
        x.reshape(bsz * seq, d), ya.reshape(bsz * seq, Q_WIDTH), yb.reshape(bsz * seq, Q_WIDTH),
        wz, wg, b_gate[layer].astype(F32)[None, :], wpa, wpb, wo,
        ln_g[layer].astype(F32)[None, :], ln_b[layer].astype(F32)[None, :], tm_out, sub_out)
    return out.reshape(bsz, seq, d)


def kernel(x, w_in, b_gate, sink_a, qnorm_b, knorm_b, w_proj_a, w_proj_b, w_out, ln_g, ln_b):
    h = x
    for layer in range(DEPTH):
        h = _layer(h, layer, w_in, b_gate, sink_a, qnorm_b, knorm_b, w_proj_a, w_proj_b, w_out, ln_g, ln_b)
    return h
```

```python
import functools

import jax
import jax.numpy as jnp
import numpy as np
from jax import lax
from jax.experimental import pallas as pl
from jax.experimental.pallas import tpu as pltpu

D_MODEL = 1024
HEAD_DIM = 64
HEADS = 8
KV_HEADS = 2
GROUP = HEADS // KV_HEADS
Q_WIDTH = HEADS * HEAD_DIM
KV_WIDTH = KV_HEADS * HEAD_DIM
WINDOW = 128
BLOCK = 128
GRID_W = 64
ROPE_THETA = 10000.0
QK_EPS = 1e-6
LN_EPS = 1e-5
DEPTH = 1
DN_ALPHA = (2.0 * DEPTH) ** 0.25
SCALE = HEAD_DIM ** -0.5
LOG2E = 1.4426950408889634

_SPLITS = (Q_WIDTH, KV_WIDTH, KV_WIDTH, Q_WIDTH, Q_WIDTH, KV_WIDTH, KV_WIDTH, Q_WIDTH, D_MODEL, D_MODEL)
_OFF = tuple(int(v) for v in np.cumsum((0,) + _SPLITS))
(QA, KA, VA, ZA, QB, KB, VB, ZB, GA, GB) = range(10)

MXU_TILE = 256
F32_SUBLANES = 8
BF16_SUBLANES = 16
VMEM_BYTES = 64 * 1024 * 1024
VMEM_LIMIT = VMEM_BYTES * 7 // 8

V_AUG = HEAD_DIM + BF16_SUBLANES

BF16 = jnp.bfloat16
F32 = jnp.float32


def _cols(w, *ids):
    return jnp.concatenate([w[:, _OFF[i]:_OFF[i + 1]] for i in ids], axis=1)


def _prep_kernel(w_ref, wt_out, wk_out):
    w = w_ref[0]
    wt_out[...] = _cols(w, QA, VA, QB, VB).T.astype(BF16)
    wk_out[...] = _cols(w, KA, KB).astype(BF16)


def _prep_qkv_weights(w_in, layer, steps=4):
    d = w_in.shape[1]
    t_width = 2 * (Q_WIDTH + KV_WIDTH)
    return pl.pallas_call(
        _prep_kernel,
        grid=(steps,),
        in_specs=[pl.BlockSpec((1, d // steps, _OFF[ZB]), lambda i: (layer, i, 0))],
        out_specs=[pl.BlockSpec((t_width, d // steps), lambda i: (0, i)),
                   pl.BlockSpec((d // steps, 2 * KV_WIDTH), lambda i: (i, 0))],
        out_shape=[jax.ShapeDtypeStruct((t_width, d), BF16), jax.ShapeDtypeStruct((d, 2 * KV_WIDTH), BF16)],
        compiler_params=pltpu.CompilerParams(
            dimension_semantics=("parallel",), vmem_limit_bytes=VMEM_LIMIT),
        name="prep_qkv_weights",
    )(w_in)


def _qkv_kernel(x_ref, wt_ref, wk_ref, gq_ref, gk_ref, cosT_ref, sinT_ref, cosk_ref, sink_ref,
                qaT_ref, ka_ref, vaT_ref, qbT_ref, kb_ref, vbT_ref, *, sub):
    half = HEAD_DIM // 2
    ones = jnp.ones((V_AUG - HEAD_DIM, sub), BF16)
    eye = (lax.broadcasted_iota(jnp.int32, (HEAD_DIM, HEAD_DIM), 0)
           == lax.broadcasted_iota(jnp.int32, (HEAD_DIM, HEAD_DIM), 1))
    gq = jnp.sum(jnp.where(eye, gq_ref[...], 0.0), axis=1, keepdims=True)[None]
    gk = jnp.concatenate([gk_ref[...]] * KV_HEADS, axis=1)

    for lo in range(0, x_ref.shape[1], sub):
        tok = slice(lo, lo + sub)
        xb = x_ref[0, tok, :].astype(BF16)
        t = lax.dot_general(wt_ref[...], xb, (((1,), (1,)), ((), ())), preferred_element_type=F32)
        kk = jnp.dot(xb, wk_ref[...], preferred_element_type=F32)

        def store_v(vT_ref, vT):
            for g in range(KV_HEADS):
                vT_ref[0, g, 0:HEAD_DIM, tok] = vT[g * HEAD_DIM:(g + 1) * HEAD_DIM].astype(BF16)
                vT_ref[0, g, HEAD_DIM:V_AUG, tok] = ones

        r0 = 0
        qaT_ref[0, :, tok] = (t[r0:r0 + Q_WIDTH] * (SCALE * LOG2E)).astype(BF16)
        r0 += Q_WIDTH
        store_v(vaT_ref, t[r0:r0 + KV_WIDTH])
        r0 += KV_WIDTH
        qb = t[r0:r0 + Q_WIDTH].reshape(HEADS, HEAD_DIM, sub)
        r0 += Q_WIDTH
        store_v(vbT_ref, t[r0:r0 + KV_WIDTH])

        ms = jnp.mean(qb * qb, axis=1, keepdims=True)
        qn = qb * lax.rsqrt(ms + QK_EPS) * gq
        q1, q2 = qn[:, :half], qn[:, half:]
        cosT, sinT = cosT_ref[:, tok][None], sinT_ref[:, tok][None]
        qr = jnp.concatenate([q1 * cosT - q2 * sinT, q2 * cosT + q1 * sinT], axis=1)
        qbT_ref[0, :, tok] = (qr * (SCALE * LOG2E)).reshape(Q_WIDTH, sub).astype(BF16)

        ka_ref[0, tok, :] = kk[:, :KV_WIDTH].astype(BF16)

        kb = kk[:, KV_WIDTH:]
        lane = lax.broadcasted_iota(jnp.int32, kb.shape, 1)
        k2 = kb * kb
        first = lane < HEAD_DIM
        ms0 = jnp.sum(jnp.where(first, k2, 0.0), axis=1, keepdims=True) * (1.0 / HEAD_DIM)
        ms1 = jnp.sum(jnp.where(first, 0.0, k2), axis=1, keepdims=True) * (1.0 / HEAD_DIM)
        kn = kb * lax.rsqrt(jnp.where(first, ms0, ms1) + QK_EPS) * gk
        swapped = jnp.where(lane % HEAD_DIM < half,
                            pltpu.roll(kn, KV_WIDTH - half, axis=1),
                            pltpu.roll(kn, half, axis=1))
        kb_ref[0, tok, :] = (kn * cosk_ref[tok, :] + swapped * sink_ref[tok, :]).astype(BF16)


def _qkv_proj(x, wt, wk, gq, gk, cosT, sinT, cosk, sink, tm, sub):
    bsz, seq, d = x.shape
    nt = seq // tm
    const = lambda shape: pl.BlockSpec(shape, lambda b, i: (0,) * len(shape))
    qT_spec = pl.BlockSpec((1, Q_WIDTH, tm), lambda b, i: (b, 0, i))
    k_spec = pl.BlockSpec((1, tm, KV_WIDTH), lambda b, i: (b, i, 0))
    vT_spec = pl.BlockSpec((1, KV_HEADS, V_AUG, tm), lambda b, i: (b, 0, 0, i))
    qT_shape = jax.ShapeDtypeStruct((bsz, Q_WIDTH, seq), BF16)
    k_shape = jax.ShapeDtypeStruct((bsz, seq, KV_WIDTH), BF16)
    vT_shape = jax.ShapeDtypeStruct((bsz, KV_HEADS, V_AUG, seq), BF16)
    return pl.pallas_call(
        functools.partial(_qkv_kernel, sub=sub),
        grid=(bsz, nt),
        in_specs=[
            pl.BlockSpec((1, tm, d), lambda b, i: (b, i, 0)),
            const(wt.shape), const(wk.shape), const(gq.shape), const(gk.shape),
            pl.BlockSpec((HEAD_DIM // 2, tm), lambda b, i: (0, i)),
            pl.BlockSpec((HEAD_DIM // 2, tm), lambda b, i: (0, i)),
            pl.BlockSpec((tm, KV_WIDTH), lambda b, i: (i, 0)),
            pl.BlockSpec((tm, KV_WIDTH), lambda b, i: (i, 0)),
        ],
        out_specs=[qT_spec, k_spec, vT_spec, qT_spec, k_spec, vT_spec],
        out_shape=[qT_shape, k_shape, vT_shape, qT_shape, k_shape, vT_shape],
        compiler_params=pltpu.CompilerParams(
            dimension_semantics=("parallel", "parallel"), vmem_limit_bytes=VMEM_LIMIT),
        name="qkv_proj",
    )(x, wt, wk, gq, gk, cosT, sinT, cosk, sink)


def _padded_q(qT_h, g):
    zeros = jnp.zeros_like(qT_h)
    parts = [zeros] * KV_HEADS
    parts[g] = qT_h
    return jnp.concatenate(parts, axis=0)


def _aligned(v, m):
    return v if isinstance(v, int) else pl.multiple_of(v, m)


def _fold_max(s):
    rows, n = s.shape
    while rows % (2 * F32_SUBLANES) == 0:
        rows //= 2
        s = jnp.maximum(s[:rows], s[rows:])
    return s if rows == F32_SUBLANES else jnp.max(s.reshape(rows // F32_SUBLANES, F32_SUBLANES, n), axis=0)


def _normalise(oT, extra=None):
    denom = oT[HEAD_DIM:HEAD_DIM + 1]
    if extra is not None:
        denom = denom + extra
    return oT[:HEAD_DIM] * (1.0 / denom)


def _alibi_table():
    r = np.arange(5 * BLOCK)[:, None]
    c = np.arange(BLOCK)[None, :]
    dist = np.abs(r - 2 * BLOCK - c).astype(np.float64)
    slopes = 2.0 ** (-8.0 * (np.arange(HEADS) + 1.0) / HEADS) * LOG2E
    bias = np.where(dist <= WINDOW, -slopes[:, None, None] * dist, -np.inf)
    return jnp.asarray(bias, F32)


def _attn_kernel(sink_ref, bias_ref, qaT_ref, ka_ref, vaT_ref, qbT_ref, kb_ref, vbT_ref,
                 qa_nx_ref, ka_nx_ref, qb_nx_ref, kb_nx_ref, w_ref, wpa_ref, wpb_ref, wo_ref,
                 ya_ref, yb_ref, wz_out, wg_out, wpa_out, wpb_out, wo_out,
                 sa_ref, sb_ref, ma_ref, mb_ref, *, tq, kc):
    seq = kb_ref.shape[1]
    gw = GROUP * HEAD_DIM
    n_tiles = seq // tq
    chunks = range(0, seq, kc)
    nblk = seq // BLOCK
    span = 3 * BLOCK
    gn = GROUP * BLOCK
    a_steps = HEADS // KV_HEADS
    assert nblk == a_steps * n_tiles

    def sink_row(g):
        return jnp.concatenate(
            [jnp.full((1, BLOCK), sink_ref[g * GROUP + j] * LOG2E, F32) for j in range(GROUP)], axis=1)

    def window(n):
        if isinstance(n, int):
            q_lo, ws = n * BLOCK, min(max((n - 1) * BLOCK, 0), seq - span)
            return q_lo, ws, 2 * BLOCK - (q_lo - ws)
        q_lo = pl.multiple_of(n * BLOCK, BLOCK)
        ws = pl.multiple_of(jnp.clip((n - 1) * BLOCK, 0, seq - span), BLOCK)
        return q_lo, ws, pl.multiple_of(2 * BLOCK - (q_lo - ws), BLOCK)

    def a_scores(n, slot, g, q_ref=qaT_ref, k_ref=ka_ref):
        q_lo, ws, b_lo = window(n)
        qT = q_ref[0, g * gw:(g + 1) * gw, pl.ds(q_lo, BLOCK)]
        w = jnp.concatenate(
            [_padded_q(qT[j * HEAD_DIM:(j + 1) * HEAD_DIM], g) for j in range(GROUP)], axis=1)
        sT = jnp.dot(k_ref[0, pl.ds(ws, span), :], w, preferred_element_type=F32)
        ms = []
        for j in range(GROUP):
            s = sT[:, j * BLOCK:(j + 1) * BLOCK] + bias_ref[g * GROUP + j, pl.ds(b_lo, span), :]
            sa_ref[slot, :, g * gn + j * BLOCK:g * gn + (j + 1) * BLOCK] = s
            ms.append(jnp.max(_fold_max(s), axis=0, keepdims=True))
        return jnp.maximum(jnp.concatenate(ms, axis=1), sink_row(g))

    def a_values(n, slot, g, m):
        _, ws, _ = window(n)
        pT = jnp.exp2(sa_ref[slot, :, g * gn:(g + 1) * gn] - m).astype(BF16)
        oT = jnp.dot(vaT_ref[0, g, :, pl.ds(ws, span)], pT, preferred_element_type=F32)
        o = _normalise(oT, jnp.exp2(sink_row(g) - m))
        return [o[:, j * BLOCK:(j + 1) * BLOCK] for j in range(GROUP)]

    def a_item(n, slot, g, m_prev, outs):
        if isinstance(n, int) and n == nblk:
            m = a_scores(0, slot, g, qa_nx_ref, ka_nx_ref)
        else:
            m = a_scores(n, slot, g)
        outs = outs + a_values(n - 1, 1 - slot, g, m_prev)
        if g == KV_HEADS - 1:
            q_lo, _, _ = window(n - 1)
            ya_ref[0, pl.ds(q_lo, BLOCK), :] = jnp.concatenate(outs, axis=0).T.astype(ya_ref.dtype)
            outs = []
        return m, outs

    def padded_q(t, h):
        qT_h = qbT_ref[0, h * HEAD_DIM:(h + 1) * HEAD_DIM, pl.ds(_aligned(t * tq, tq), tq)]
        return _padded_q(qT_h, h // GROUP)

    def scores_chunk(slot, w, k_lo, m8, k_ref=kb_ref):
        sT = jnp.dot(k_ref[0, k_lo:k_lo + kc, :], w, preferred_element_type=F32)
        sb_ref[slot, k_lo:k_lo + kc, :] = sT
        m8_c = _fold_max(sT)
        return m8_c if m8 is None else jnp.maximum(m8, m8_c)

    def values_chunk(g, slot, m, k_lo, acc):
        pT = jnp.exp2(sb_ref[slot, k_lo:k_lo + kc, :] - m).astype(BF16)
        oT = jnp.dot(vbT_ref[0, g, :, k_lo:k_lo + kc], pT, preferred_element_type=F32)
        return oT if acc is None else acc + oT

    def cast_weight_rows(t):
        def rows(ref):
            n = ref.shape[-2] // n_tiles
            return pl.ds(_aligned(t * n, n), n)

        w = w_ref[0, rows(w_ref), :]
        wz_out[rows(wz_out), :] = _cols(w, ZA, ZB).astype(BF16)
        wg_out[rows(wg_out), :] = _cols(w, GA, GB).astype(BF16)
        wpa_out[rows(wpa_out), :] = wpa_ref[0, rows(wpa_ref), :].astype(BF16)
        wpb_out[rows(wpb_out), :] = wpb_ref[0, rows(wpb_ref), :].astype(BF16)
        wo_out[rows(wo_out), :] = wo_ref[0, rows(wo_ref), :].astype(BF16)

    @pl.when(pl.program_id(0) == 0)
    def _():
        w, m8 = padded_q(0, 0), None
        for k_lo in chunks:
            m8 = scores_chunk(0, w, k_lo, m8)
        mb_ref[...] = jnp.max(m8, axis=0, keepdims=True)
        for g in range(KV_HEADS):
            ma_ref[g] = a_scores(0, 0, g)

    def tile(t, carry, last=False):
        m, ma = carry[0], list(carry[1:])
        cast_weight_rows(t)
        outs, a_outs, ma_next = [], [], [None] * KV_HEADS
        for h in range(HEADS):
            if last and h == HEADS - 1:
                w, k_ref = _padded_q(qb_nx_ref[0], 0), kb_nx_ref
            else:
                w, k_ref = padded_q(*((t, h + 1) if h + 1 < HEADS else (t + 1, 0))), kb_ref
            m8 = acc = None
            for k_lo in chunks:
                m8 = scores_chunk((h + 1) % 2, w, k_lo, m8, k_ref)
                acc = values_chunk(h // GROUP, h % 2, m, k_lo, acc)
            outs.append(_normalise(acc))
            m = jnp.max(m8, axis=0, keepdims=True)
            if h % GROUP == GROUP - 1:
                g = h // GROUP
                yb_ref[0, pl.ds(_aligned(t * tq, tq), tq), g * gw:(g + 1) * gw] = (
                    jnp.concatenate(outs, axis=0).T.astype(yb_ref.dtype))
                outs = []

            i, ga = h // KV_HEADS, h % KV_HEADS
            ma_next[ga], a_outs = a_item(a_steps * t + i + 1, (i + 1) % 2, ga, ma[ga], a_outs)
            if ga == KV_HEADS - 1:
                ma, ma_next = ma_next, [None] * KV_HEADS
        return (m,) + tuple(ma)

    carry = (mb_ref[...],) + tuple(ma_ref[g] for g in range(KV_HEADS))
    carry = lax.fori_loop(0, n_tiles - 1, tile, carry)
    carry = tile(n_tiles - 1, carry, last=True)
    mb_ref[...] = carry[0]
    for g in range(KV_HEADS):
        ma_ref[g] = carry[1 + g]


def _attention(sink, qaT, ka, vaT, qbT, kb, vbT, w_in, w_proj_a, w_proj_b, w_out, layer, tq, kc):
    bsz, _, seq = qbT.shape
    d = w_in.shape[1]
    bias = _alibi_table()
    qT_spec = pl.BlockSpec((1, Q_WIDTH, seq), lambda b: (b, 0, 0))
    k_spec = pl.BlockSpec((1, seq, KV_WIDTH), lambda b: (b, 0, 0))
    vT_spec = pl.BlockSpec((1, KV_HEADS, V_AUG, seq), lambda b: (b, 0, 0, 0))
    y_spec = pl.BlockSpec((1, seq, Q_WIDTH), lambda b: (b, 0, 0))
    y_shape = jax.ShapeDtypeStruct((bsz, seq, Q_WIDTH), BF16)
    nx = lambda shape: pl.BlockSpec(shape, lambda b: (jnp.minimum(b + 1, bsz - 1), 0, 0))
    slab_in = lambda a: pl.BlockSpec((1, a.shape[1] // bsz, a.shape[2]), lambda b: (layer, b, 0))
    slab_out = lambda rows, width: pl.BlockSpec((rows // bsz, width), lambda b: (b, 0))
    w_shapes = [(d, 2 * Q_WIDTH), (d, 2 * D_MODEL), w_proj_a.shape[1:], w_proj_b.shape[1:], w_out.shape[1:]]
    outs = pl.pallas_call(
        functools.partial(_attn_kernel, tq=tq, kc=kc),
        grid=(bsz,),
        in_specs=[
            pl.BlockSpec(memory_space=pltpu.SMEM),
            pl.BlockSpec(bias.shape, lambda b: (0, 0, 0)),
            qT_spec, k_spec, vT_spec, qT_spec, k_spec, vT_spec,
            nx((1, Q_WIDTH, BLOCK)), nx((1, 3 * BLOCK, KV_WIDTH)), nx((1, HEAD_DIM, tq)), nx((1, seq, KV_WIDTH)),
            slab_in(w_in), slab_in(w_proj_a), slab_in(w_proj_b), slab_in(w_out),
        ],
        out_specs=[y_spec, y_spec] + [slab_out(*s) for s in w_shapes],
        out_shape=[y_shape, y_shape] + [jax.ShapeDtypeStruct(s, BF16) for s in w_shapes],
        scratch_shapes=[pltpu.VMEM((2, 3 * BLOCK, HEADS * BLOCK), F32), pltpu.VMEM((2, seq, tq), F32),
                        pltpu.VMEM((KV_HEADS, 1, GROUP * BLOCK), F32), pltpu.VMEM((1, tq), F32)],
        compiler_params=pltpu.CompilerParams(
            dimension_semantics=("arbitrary",), vmem_limit_bytes=VMEM_LIMIT),
        name="attention",
    )(sink, bias, qaT, ka, vaT, qbT, kb, vbT, qaT, ka, qbT, kb, w_in, w_proj_a, w_proj_b, w_out)
    return outs[0], outs[1], outs[2:]


def _out_kernel(x_ref, ya_ref, yb_ref, wz_ref, wg_ref, bg_ref, wpa_ref, wpb_ref, wo_ref, lng_ref, lnb_ref,
                o_ref, *, sub):
    def input_proj(rows):
        xb = x_ref[rows, :].astype(BF16)
        z = jnp.dot(xb, wz_ref[...], preferred_element_type=F32)
        gl = jnp.dot(xb, wg_ref[...], preferred_element_type=F32)
        return z, gl

    def branch_proj(rows, z):
        sz = z * jax.nn.sigmoid(z)
        ua = (ya_ref[rows, :].astype(F32) * sz[:, :Q_WIDTH]).astype(BF16)
        ub = (yb_ref[rows, :].astype(F32) * sz[:, Q_WIDTH:]).astype(BF16)
        pa = jnp.dot(ua, wpa_ref[...], preferred_element_type=F32)
        pb = jnp.dot(ub, wpb_ref[...], preferred_element_type=F32)
        return pa, pb

    def merge_norm(rows, gl, pa, pb):
        gate = jax.nn.sigmoid(gl + bg_ref[...])
        mix = (gate[:, :D_MODEL] * pa + gate[:, D_MODEL:] * pb).astype(BF16)
        y = jnp.dot(mix, wo_ref[...], preferred_element_type=F32)
        r = DN_ALPHA * x_ref[rows, :] + y
        mu = jnp.mean(r, axis=-1, keepdims=True)
        rc = r - mu
        var = jnp.mean(rc * rc, axis=-1, keepdims=True)
        o_ref[rows, :] = rc * lax.rsqrt(var + LN_EPS) * lng_ref[...] + lnb_ref[...]

    tiles = [slice(lo, lo + sub) for lo in range(0, x_ref.shape[0], sub)]
    stage1, stage2 = {}, {}
    for step in range(len(tiles) + 2):
        if step < len(tiles):
            stage1[step] = input_proj(tiles[step])
        if 0 <= step - 1 < len(tiles):
            stage2[step - 1] = branch_proj(tiles[step - 1], stage1[step - 1][0])
        if 0 <= step - 2 < len(tiles):
            merge_norm(tiles[step - 2], stage1.pop(step - 2)[1], *stage2.pop(step - 2))


def _out_block(x2, ya2, yb2, wz, wg, bg, wpa, wpb, wo, lng, lnb, tm, sub):
    n, d = x2.shape
    const = lambda a: pl.BlockSpec(a.shape, lambda i: (0, 0), pipeline_mode=pl.Buffered(1))
    row = lambda w: pl.BlockSpec((tm, w), lambda i: (i, 0))
    return pl.pallas_call(
        functools.partial(_out_kernel, sub=sub),
        grid=(n // tm,),
        in_specs=[row(d), row(Q_WIDTH), row(Q_WIDTH),
                  const(wz), const(wg), const(bg), const(wpa), const(wpb), const(wo), const(lng), const(lnb)],
        out_specs=row(d),
        out_shape=jax.ShapeDtypeStruct((n, d), x2.dtype),
        compiler_params=pltpu.CompilerParams(
            dimension_semantics=("parallel",), vmem_limit_bytes=VMEM_LIMIT),
        name="out_block",
    )(x2, ya2, yb2, wz, wg, bg, wpa, wpb, wo, lng, lnb)


def _rope_tables(seq):
    half = HEAD_DIM // 2
    axis_pairs = half // 2
    pos = np.arange(seq)
    freqs = ROPE_THETA ** (-np.arange(axis_pairs, dtype=np.float64) / axis_pairs)
    ang = np.concatenate([(pos // GRID_W)[:, None] * freqs, (pos % GRID_W)[:, None] * freqs], axis=-1)
    cos, sin = np.cos(ang), np.sin(ang)
    cosk = np.tile(cos, (1, KV_WIDTH // half))
    sink = np.tile(np.concatenate([-sin, sin], axis=-1), (1, KV_HEADS))
    return tuple(jnp.asarray(a, F32) for a in (cos.T, sin.T, cosk, sink))


def _layer(x, layer, w_in, b_gate, sink_a, qnorm_b, knorm_b, w_proj_a, w_proj_b, w_out, ln_g, ln_b):
    bsz, seq, d = x.shape
    tm_qkv, sub_qkv, tm_out, sub_out = 2048, MXU_TILE, 1024, MXU_TILE
    tq, kc = 2 * MXU_TILE, MXU_TILE

    wt, wk = _prep_qkv_weights(w_in, layer)
    gq = qnorm_b[layer].astype(F32)[None, :]
    gk = knorm_b[layer].astype(F32)[None, :]
    cosT, sinT, cosk, sink = _rope_tables(seq)

    qaT, ka, vaT, qbT, kb, vbT = _qkv_proj(x, wt, wk, gq, gk, cosT, sinT, cosk, sink, tm_qkv, sub_qkv)
    ya, yb, (wz, wg, wpa, wpb, wo) = _attention(
        sink_a[layer].astype(F32), qaT, ka, vaT, qbT, kb, vbT, w_in, w_proj_a, w_proj_b, w_out, layer, tq, kc)

    out = _out_block(
        x.reshape(bsz * seq, d), ya.reshape(bsz * seq, Q_WIDTH), yb.reshape(bsz * seq, Q_WIDTH),
        wz, wg, b_gate[layer].astype(F32)[None, :], wpa, wpb, wo,
        ln_g[layer].astype(F32)[None, :], ln_b[layer].astype(F32)[None, :], tm_out, sub_out)
    return out.reshape(bsz, seq, d)


def kernel(x, w_in, b_gate, sink_a, qnorm_b, knorm_b, w_proj_a, w_proj_b, w_out, ln_g, ln_b):
    h = x
    for layer in range(DEPTH):
        h = _layer(h, layer, w_in, b_gate, sink_a, qnorm_b, knorm_b, w_proj_a, w_proj_b, w_out, ln_g, ln_b)
    return h
```

```python
import functools

import jax
import jax.numpy as jnp
import numpy as np
from jax import lax
from jax.experimental import pallas as pl
from jax.experimental.pallas import tpu as pltpu

D_MODEL = 1024
HEAD_DIM = 64
HEADS = 8
KV_HEADS = 2
GROUP = HEADS // KV_HEADS
Q_WIDTH = HEADS * HEAD_DIM
KV_WIDTH = KV_HEADS * HEAD_DIM
WINDOW = 128
BLOCK = 128
GRID_W = 64
ROPE_THETA = 10000.0
QK_EPS = 1e-6
LN_EPS = 1e-5
DEPTH = 1
DN_ALPHA = (2.0 * DEPTH) ** 0.25
SCALE = HEAD_DIM ** -0.5
LOG2E = 1.4426950408889634

_SPLITS = (Q_WIDTH, KV_WIDTH, KV_WIDTH, Q_WIDTH, Q_WIDTH, KV_WIDTH, KV_WIDTH, Q_WIDTH, D_MODEL, D_MODEL)
_OFF = tuple(int(v) for v in np.cumsum((0,) + _SPLITS))
(QA, KA, VA, ZA, QB, KB, VB, ZB, GA, GB) = range(10)

MXU_TILE = 256
F32_SUBLANES = 8
BF16_SUBLANES = 16
VMEM_BYTES = 64 * 1024 * 1024
VMEM_LIMIT = VMEM_BYTES * 7 // 8

V_AUG = HEAD_DIM + BF16_SUBLANES

BF16 = jnp.bfloat16
F32 = jnp.float32


def _cols(w, *ids):
    return jnp.concatenate([w[:, _OFF[i]:_OFF[i + 1]] for i in ids], axis=1)


def _prep_kernel(w_ref, wt_out, wk_out):
    w = w_ref[0]
    wt_out[...] = _cols(w, QA, VA, QB, VB).T.astype(BF16)
    wk_out[...] = _cols(w, KA, KB).astype(BF16)


def _prep_qkv_weights(w_in, layer, steps=8):
    d = w_in.shape[1]
    t_width = 2 * (Q_WIDTH + KV_WIDTH)
    return pl.pallas_call(
        _prep_kernel,
        grid=(steps,),
        in_specs=[pl.BlockSpec((1, d // steps, _OFF[ZB]), lambda i: (layer, i, 0))],
        out_specs=[pl.BlockSpec((t_width, d // steps), lambda i: (0, i)),
                   pl.BlockSpec((d // steps, 2 * KV_WIDTH), lambda i: (i, 0))],
        out_shape=[jax.ShapeDtypeStruct((t_width, d), BF16), jax.ShapeDtypeStruct((d, 2 * KV_WIDTH), BF16)],
        compiler_params=pltpu.CompilerParams(
            dimension_semantics=("parallel",), vmem_limit_bytes=VMEM_LIMIT),
        name="prep_qkv_weights",
    )(w_in)


def _qkv_kernel(x_ref, wt_ref, wk_ref, gq_ref, gk_ref, cosT_ref, sinT_ref, cosk_ref, sink_ref,
                qaT_ref, ka_ref, vaT_ref, qbT_ref, kb_ref, vbT_ref, *, sub):
    half = HEAD_DIM // 2
    ones = jnp.ones((V_AUG - HEAD_DIM, sub), BF16)
    eye = (lax.broadcasted_iota(jnp.int32, (HEAD_DIM, HEAD_DIM), 0)
           == lax.broadcasted_iota(jnp.int32, (HEAD_DIM, HEAD_DIM), 1))
    gq = jnp.sum(jnp.where(eye, gq_ref[...], 0.0), axis=1, keepdims=True)[None]
    gk = jnp.concatenate([gk_ref[...]] * KV_HEADS, axis=1)

    for lo in range(0, x_ref.shape[1], sub):
        tok = slice(lo, lo + sub)
        xb = x_ref[0, tok, :].astype(BF16)
        t = lax.dot_general(wt_ref[...], xb, (((1,), (1,)), ((), ())), preferred_element_type=F32)
        kk = jnp.dot(xb, wk_ref[...], preferred_element_type=F32)

        def store_v(vT_ref, vT):
            for g in range(KV_HEADS):
                vT_ref[0, g, 0:HEAD_DIM, tok] = vT[g * HEAD_DIM:(g + 1) * HEAD_DIM].astype(BF16)
                vT_ref[0, g, HEAD_DIM:V_AUG, tok] = ones

        r0 = 0
        qaT_ref[0, :, tok] = (t[r0:r0 + Q_WIDTH] * (SCALE * LOG2E)).astype(BF16)
        r0 += Q_WIDTH
        store_v(vaT_ref, t[r0:r0 + KV_WIDTH])
        r0 += KV_WIDTH
        qb = t[r0:r0 + Q_WIDTH].reshape(HEADS, HEAD_DIM, sub)
        r0 += Q_WIDTH
        store_v(vbT_ref, t[r0:r0 + KV_WIDTH])

        ms = jnp.mean(qb * qb, axis=1, keepdims=True)
        qn = qb * lax.rsqrt(ms + QK_EPS) * gq
        q1, q2 = qn[:, :half], qn[:, half:]
        cosT, sinT = cosT_ref[:, tok][None], sinT_ref[:, tok][None]
        qr = jnp.concatenate([q1 * cosT - q2 * sinT, q2 * cosT + q1 * sinT], axis=1)
        qbT_ref[0, :, tok] = (qr * (SCALE * LOG2E)).reshape(Q_WIDTH, sub).astype(BF16)

        ka_ref[0, tok, :] = kk[:, :KV_WIDTH].astype(BF16)

        kb = kk[:, KV_WIDTH:]
        lane = lax.broadcasted_iota(jnp.int32, kb.shape, 1)
        k2 = kb * kb
        first = lane < HEAD_DIM
        ms0 = jnp.sum(jnp.where(first, k2, 0.0), axis=1, keepdims=True) * (1.0 / HEAD_DIM)
        ms1 = jnp.sum(jnp.where(first, 0.0, k2), axis=1, keepdims=True) * (1.0 / HEAD_DIM)
        kn = kb * lax.rsqrt(jnp.where(first, ms0, ms1) + QK_EPS) * gk
        swapped = jnp.where(lane % HEAD_DIM < half,
                            pltpu.roll(kn, KV_WIDTH - half, axis=1),
                            pltpu.roll(kn, half, axis=1))
        kb_ref[0, tok, :] = (kn * cosk_ref[tok, :] + swapped * sink_ref[tok, :]).astype(BF16)


def _qkv_proj(x, wt, wk, gq, gk, cosT, sinT, cosk, sink, tm, sub):
    bsz, seq, d = x.shape
    nt = seq // tm
    const = lambda shape: pl.BlockSpec(shape, lambda b, i: (0,) * len(shape))
    qT_spec = pl.BlockSpec((1, Q_WIDTH, tm), lambda b, i: (b, 0, i))
    k_spec = pl.BlockSpec((1, tm, KV_WIDTH), lambda b, i: (b, i, 0))
    vT_spec = pl.BlockSpec((1, KV_HEADS, V_AUG, tm), lambda b, i: (b, 0, 0, i))
    qT_shape = jax.ShapeDtypeStruct((bsz, Q_WIDTH, seq), BF16)
    k_shape = jax.ShapeDtypeStruct((bsz, seq, KV_WIDTH), BF16)
    vT_shape = jax.ShapeDtypeStruct((bsz, KV_HEADS, V_AUG, seq), BF16)
    return pl.pallas_call(
        functools.partial(_qkv_kernel, sub=sub),
        grid=(bsz, nt),
        in_specs=[
            pl.BlockSpec((1, tm, d), lambda b, i: (b, i, 0)),
            const(wt.shape), const(wk.shape), const(gq.shape), const(gk.shape),
            pl.BlockSpec((HEAD_DIM // 2, tm), lambda b, i: (0, i)),
            pl.BlockSpec((HEAD_DIM // 2, tm), lambda b, i: (0, i)),
            pl.BlockSpec((tm, KV_WIDTH), lambda b, i: (i, 0)),
            pl.BlockSpec((tm, KV_WIDTH), lambda b, i: (i, 0)),
        ],
        out_specs=[qT_spec, k_spec, vT_spec, qT_spec, k_spec, vT_spec],
        out_shape=[qT_shape, k_shape, vT_shape, qT_shape, k_shape, vT_shape],
        compiler_params=pltpu.CompilerParams(
            dimension_semantics=("parallel", "parallel"), vmem_limit_bytes=VMEM_LIMIT),
        name="qkv_proj",
    )(x, wt, wk, gq, gk, cosT, sinT, cosk, sink)


def _padded_q(qT_h, g):
    zeros = jnp.zeros_like(qT_h)
    parts = [zeros] * KV_HEADS
    parts[g] = qT_h
    return jnp.concatenate(parts, axis=0)


def _aligned(v, m):
    return v if isinstance(v, int) else pl.multiple_of(v, m)


def _fold_max(s):
    rows, n = s.shape
    while rows % (2 * F32_SUBLANES) == 0:
        rows //= 2
        s = jnp.maximum(s[:rows], s[rows:])
    return s if rows == F32_SUBLANES else jnp.max(s.reshape(rows // F32_SUBLANES, F32_SUBLANES, n), axis=0)


def _normalise(oT, extra=None):
    denom = oT[HEAD_DIM:HEAD_DIM + 1]
    if extra is not None:
        denom = denom + extra
    return oT[:HEAD_DIM] * (1.0 / denom)


def _alibi_table():
    r = np.arange(5 * BLOCK)[:, None]
    c = np.arange(BLOCK)[None, :]
    dist = np.abs(r - 2 * BLOCK - c).astype(np.float64)
    slopes = 2.0 ** (-8.0 * (np.arange(HEADS) + 1.0) / HEADS) * LOG2E
    bias = np.where(dist <= WINDOW, -slopes[:, None, None] * dist, -np.inf)
    return jnp.asarray(bias, F32)


def _attn_kernel(sink_ref, bias_ref, qaT_ref, ka_ref, vaT_ref, qbT_ref, kb_ref, vbT_ref,
                 w_ref, wpa_ref, wpb_ref, wo_ref,
                 ya_ref, yb_ref, wz_out, wg_out, wpa_out, wpb_out, wo_out, sa_ref, sb_ref, *, tq, kc):
    seq = kb_ref.shape[1]
    gw = GROUP * HEAD_DIM
    n_tiles = seq // tq
    chunks = range(0, seq, kc)
    nblk = seq // BLOCK
    span = 3 * BLOCK
    gn = GROUP * BLOCK
    a_steps = HEADS // KV_HEADS
    assert nblk == a_steps * n_tiles

    def sink_row(g):
        return jnp.concatenate(
            [jnp.full((1, BLOCK), sink_ref[g * GROUP + j] * LOG2E, F32) for j in range(GROUP)], axis=1)

    def window(n):
        if isinstance(n, int):
            q_lo, ws = n * BLOCK, min(max((n - 1) * BLOCK, 0), seq - span)
            return q_lo, ws, 2 * BLOCK - (q_lo - ws)
        q_lo = pl.multiple_of(n * BLOCK, BLOCK)
        ws = pl.multiple_of(jnp.clip((n - 1) * BLOCK, 0, seq - span), BLOCK)
        return q_lo, ws, pl.multiple_of(2 * BLOCK - (q_lo - ws), BLOCK)

    def a_scores(n, slot, g):
        q_lo, ws, b_lo = window(n)
        qT = qaT_ref[0, g * gw:(g + 1) * gw, pl.ds(q_lo, BLOCK)]
        w = jnp.concatenate(
            [_padded_q(qT[j * HEAD_DIM:(j + 1) * HEAD_DIM], g) for j in range(GROUP)], axis=1)
        sT = jnp.dot(ka_ref[0, pl.ds(ws, span), :], w, preferred_element_type=F32)
        ms = []
        for j in range(GROUP):
            s = sT[:, j * BLOCK:(j + 1) * BLOCK] + bias_ref[g * GROUP + j, pl.ds(b_lo, span), :]
            sa_ref[slot, :, g * gn + j * BLOCK:g * gn + (j + 1) * BLOCK] = s
            ms.append(jnp.max(_fold_max(s), axis=0, keepdims=True))
        return jnp.maximum(jnp.concatenate(ms, axis=1), sink_row(g))

    def a_values(n, slot, g, m):
        _, ws, _ = window(n)
        pT = jnp.exp2(sa_ref[slot, :, g * gn:(g + 1) * gn] - m).astype(BF16)
        oT = jnp.dot(vaT_ref[0, g, :, pl.ds(ws, span)], pT, preferred_element_type=F32)
        o = _normalise(oT, jnp.exp2(sink_row(g) - m))
        return [o[:, j * BLOCK:(j + 1) * BLOCK] for j in range(GROUP)]

    def a_item(n, slot, g, m_prev, outs, do_scores=True):
        m = a_scores(n, slot, g) if do_scores else None
        outs = outs + a_values(n - 1, 1 - slot, g, m_prev)
        if g == KV_HEADS - 1:
            q_lo, _, _ = window(n - 1)
            ya_ref[0, pl.ds(q_lo, BLOCK), :] = jnp.concatenate(outs, axis=0).T.astype(ya_ref.dtype)
            outs = []
        return m, outs

    def padded_q(t, h):
        qT_h = qbT_ref[0, h * HEAD_DIM:(h + 1) * HEAD_DIM, pl.ds(_aligned(t * tq, tq), tq)]
        return _padded_q(qT_h, h // GROUP)

    def scores_chunk(slot, w, k_lo, m8):
        sT = jnp.dot(kb_ref[0, k_lo:k_lo + kc, :], w, preferred_element_type=F32)
        sb_ref[slot, k_lo:k_lo + kc, :] = sT
        m8_c = _fold_max(sT)
        return m8_c if m8 is None else jnp.maximum(m8, m8_c)

    def values_chunk(g, slot, m, k_lo, acc):
        pT = jnp.exp2(sb_ref[slot, k_lo:k_lo + kc, :] - m).astype(BF16)
        oT = jnp.dot(vbT_ref[0, g, :, k_lo:k_lo + kc], pT, preferred_element_type=F32)
        return oT if acc is None else acc + oT

    def cast_weight_rows(t):
        def rows(ref):
            n = ref.shape[-2] // n_tiles
            return pl.ds(_aligned(t * n, n), n)

        w = w_ref[0, rows(w_ref), :]
        wz_out[rows(wz_out), :] = _cols(w, ZA, ZB).astype(BF16)
        wg_out[rows(wg_out), :] = _cols(w, GA, GB).astype(BF16)
        wpa_out[rows(wpa_out), :] = wpa_ref[0, rows(wpa_ref), :].astype(BF16)
        wpb_out[rows(wpb_out), :] = wpb_ref[0, rows(wpb_ref), :].astype(BF16)
        wo_out[rows(wo_out), :] = wo_ref[0, rows(wo_ref), :].astype(BF16)

    def fill():
        w, m8 = padded_q(0, 0), None
        for k_lo in chunks:
            m8 = scores_chunk(0, w, k_lo, m8)
        return (jnp.max(m8, axis=0, keepdims=True),) + tuple(a_scores(0, 0, g) for g in range(KV_HEADS))

    def tile(t, carry, last=False):
        m, ma = carry[0], list(carry[1:])
        cast_weight_rows(t)
        outs, a_outs, ma_next = [], [], [None] * KV_HEADS
        for h in range(HEADS):
            nxt = (t, h + 1) if h + 1 < HEADS else (None if last else (t + 1, 0))
            w = padded_q(*nxt) if nxt else None
            m8 = acc = None
            for k_lo in chunks:
                if nxt:
                    m8 = scores_chunk((h + 1) % 2, w, k_lo, m8)
                acc = values_chunk(h // GROUP, h % 2, m, k_lo, acc)
            outs.append(_normalise(acc))
            m = jnp.max(m8, axis=0, keepdims=True) if nxt else None
            if h % GROUP == GROUP - 1:
                g = h // GROUP
                yb_ref[0, pl.ds(_aligned(t * tq, tq), tq), g * gw:(g + 1) * gw] = (
                    jnp.concatenate(outs, axis=0).T.astype(yb_ref.dtype))
                outs = []

            i, ga = h // KV_HEADS, h % KV_HEADS
            n = a_steps * t + i + 1
            a_has_scores = not (last and i == a_steps - 1)
            ma_next[ga], a_outs = a_item(n, (i + 1) % 2, ga, ma[ga], a_outs, a_has_scores)
            if ga == KV_HEADS - 1:
                ma, ma_next = ma_next, [None] * KV_HEADS
        return None if last else (m,) + tuple(ma)

    carry = fill()
    carry = lax.fori_loop(0, n_tiles - 1, tile, carry)
    tile(n_tiles - 1, carry, last=True)


def _attention(sink, qaT, ka, vaT, qbT, kb, vbT, w_in, w_proj_a, w_proj_b, w_out, layer, tq, kc):
    bsz, _, seq = qbT.shape
    d = w_in.shape[1]
    bias = _alibi_table()
    qT_spec = pl.BlockSpec((1, Q_WIDTH, seq), lambda b: (b, 0, 0))
    k_spec = pl.BlockSpec((1, seq, KV_WIDTH), lambda b: (b, 0, 0))
    vT_spec = pl.BlockSpec((1, KV_HEADS, V_AUG, seq), lambda b: (b, 0, 0, 0))
    y_spec = pl.BlockSpec((1, seq, Q_WIDTH), lambda b: (b, 0, 0))
    y_shape = jax.ShapeDtypeStruct((bsz, seq, Q_WIDTH), BF16)
    slab_in = lambda a: pl.BlockSpec((1, a.shape[1] // bsz, a.shape[2]), lambda b: (layer, b, 0))
    slab_out = lambda rows, width: pl.BlockSpec((rows // bsz, width), lambda b: (b, 0))
    w_shapes = [(d, 2 * Q_WIDTH), (d, 2 * D_MODEL), w_proj_a.shape[1:], w_proj_b.shape[1:], w_out.shape[1:]]
    outs = pl.pallas_call(
        functools.partial(_attn_kernel, tq=tq, kc=kc),
        grid=(bsz,),
        in_specs=[
            pl.BlockSpec(memory_space=pltpu.SMEM),
            pl.BlockSpec(bias.shape, lambda b: (0, 0, 0)),
            qT_spec, k_spec, vT_spec, qT_spec, k_spec, vT_spec,
            slab_in(w_in), slab_in(w_proj_a), slab_in(w_proj_b), slab_in(w_out),
        ],
        out_specs=[y_spec, y_spec] + [slab_out(*s) for s in w_shapes],
        out_shape=[y_shape, y_shape] + [jax.ShapeDtypeStruct(s, BF16) for s in w_shapes],
        scratch_shapes=[pltpu.VMEM((2, 3 * BLOCK, HEADS * BLOCK), F32), pltpu.VMEM((2, seq, tq), F32)],
        compiler_params=pltpu.CompilerParams(
            dimension_semantics=("parallel",), vmem_limit_bytes=VMEM_LIMIT),
        name="attention",
    )(sink, bias, qaT, ka, vaT, qbT, kb, vbT, w_in, w_proj_a, w_proj_b, w_out)
    return outs[0], outs[1], outs[2:]


def _out_kernel(x_ref, ya_ref, yb_ref, wz_ref, wg_ref, bg_ref, wpa_ref, wpb_ref, wo_ref, lng_ref, lnb_ref,
                o_ref, *, sub):
    def input_proj(rows):
        xb = x_ref[rows, :].astype(BF16)
        z = jnp.dot(xb, wz_ref[...], preferred_element_type=F32)
        gl = jnp.dot(xb, wg_ref[...], preferred_element_type=F32)
        return z, gl

    def branch_proj(rows, z):
        sz = z * jax.nn.sigmoid(z)
        ua = (ya_ref[rows, :].astype(F32) * sz[:, :Q_WIDTH]).astype(BF16)
        ub = (yb_ref[rows, :].astype(F32) * sz[:, Q_WIDTH:]).astype(BF16)
        pa = jnp.dot(ua, wpa_ref[...], preferred_element_type=F32)
        pb = jnp.dot(ub, wpb_ref[...], preferred_element_type=F32)
        return pa, pb

    def merge_norm(rows, gl, pa, pb):
        gate = jax.nn.sigmoid(gl + bg_ref[...])
        mix = (gate[:, :D_MODEL] * pa + gate[:, D_MODEL:] * pb).astype(BF16)
        y = jnp.dot(mix, wo_ref[...], preferred_element_type=F32)
        r = DN_ALPHA * x_ref[rows, :] + y
        mu = jnp.mean(r, axis=-1, keepdims=True)
        rc = r - mu
        var = jnp.mean(rc * rc, axis=-1, keepdims=True)
        o_ref[rows, :] = rc * lax.rsqrt(var + LN_EPS) * lng_ref[...] + lnb_ref[...]

    tiles = [slice(lo, lo + sub) for lo in range(0, x_ref.shape[0], sub)]
    stage1, stage2 = {}, {}
    for step in range(len(tiles) + 2):
        if step < len(tiles):
            stage1[step] = input_proj(tiles[step])
        if 0 <= step - 1 < len(tiles):
            stage2[step - 1] = branch_proj(tiles[step - 1], stage1[step - 1][0])
        if 0 <= step - 2 < len(tiles):
            merge_norm(tiles[step - 2], stage1.pop(step - 2)[1], *stage2.pop(step - 2))


def _out_block(x2, ya2, yb2, wz, wg, bg, wpa, wpb, wo, lng, lnb, tm, sub):
    n, d = x2.shape
    const = lambda a: pl.BlockSpec(a.shape, lambda i: (0, 0), pipeline_mode=pl.Buffered(1))
    row = lambda w: pl.BlockSpec((tm, w), lambda i: (i, 0))
    return pl.pallas_call(
        functools.partial(_out_kernel, sub=sub),
        grid=(n // tm,),
        in_specs=[row(d), row(Q_WIDTH), row(Q_WIDTH),
                  const(wz), const(wg), const(bg), const(wpa), const(wpb), const(wo), const(lng), const(lnb)],
        out_specs=row(d),
        out_shape=jax.ShapeDtypeStruct((n, d), x2.dtype),
        compiler_params=pltpu.CompilerParams(
            dimension_semantics=("parallel",), vmem_limit_bytes=VMEM_LIMIT),
        name="out_block",
    )(x2, ya2, yb2, wz, wg, bg, wpa, wpb, wo, lng, lnb)


def _rope_tables(seq):
    half = HEAD_DIM // 2
    axis_pairs = half // 2
    pos = np.arange(seq)
    freqs = ROPE_THETA ** (-np.arange(axis_pairs, dtype=np.float64) / axis_pairs)
    ang = np.concatenate([(pos // GRID_W)[:, None] * freqs, (pos % GRID_W)[:, None] * freqs], axis=-1)
    cos, sin = np.cos(ang), np.sin(ang)
    cosk = np.tile(cos, (1, KV_WIDTH // half))
    sink = np.tile(np.concatenate([-sin, sin], axis=-1), (1, KV_HEADS))
    return tuple(jnp.asarray(a, F32) for a in (cos.T, sin.T, cosk, sink))


def _layer(x, layer, w_in, b_gate, sink_a, qnorm_b, knorm_b, w_proj_a, w_proj_b, w_out, ln_g, ln_b):
    bsz, seq, d = x.shape
    tm_qkv, sub_qkv, tm_out, sub_out = 2048, MXU_TILE, 1024, MXU_TILE
    tq, kc = 2 * MXU_TILE, MXU_TILE

    wt, wk = _prep_qkv_weights(w_in, layer)
    gq = qnorm_b[layer].astype(F32)[None, :]
    gk = knorm_b[layer].astype(F32)[None, :]
    cosT, sinT, cosk, sink = _rope_tables(seq)

    qaT, ka, vaT, qbT, kb, vbT = _qkv_proj(x, wt, wk, gq, gk, cosT, sinT, cosk, sink, tm_qkv, sub_qkv)
    ya, yb, (wz, wg, wpa, wpb, wo) = _attention(
        sink_a[layer].astype(F32), qaT, ka, vaT, qbT, kb, vbT, w_in, w_proj_a, w_proj_b, w_out, layer, tq, kc)

    out = _out_block(
        x.reshape(bsz * seq, d), ya.reshape(bsz * seq, Q_WIDTH), yb.reshape(bsz * seq, Q_WIDTH),
        wz, wg, b_gate[layer].astype(F32)[None, :], wpa, wpb, wo,
        ln_g[layer].astype(F32)[None, :], ln_b[layer].astype(F32)[None, :], tm_out, sub_out)
    return out.reshape(bsz, seq, d)


def kernel(x, w_in, b_gate, sink_a, qnorm_b, knorm_b, w_proj_a, w_proj_b, w_out, ln_g, ln_b):
    h = x
    for layer in range(DEPTH):
        h = _layer(h, layer, w_in, b_gate, sink_a, qnorm_b, knorm_b, w_proj_a, w_proj_b, w_out, ln_g, ln_b)
    return h
```

```python
import functools

import jax
import jax.numpy as jnp
import numpy as np
from jax import lax
from jax.experimental import pallas as pl
from jax.experimental.pallas import tpu as pltpu

D_MODEL = 1024
HEAD_DIM = 64
HEADS = 8
KV_HEADS = 2
GROUP = HEADS // KV_HEADS
Q_WIDTH = HEADS * HEAD_DIM
KV_WIDTH = KV_HEADS * HEAD_DIM
WINDOW = 128
BLOCK = 128
GRID_W = 64
ROPE_THETA = 10000.0
QK_EPS = 1e-6
LN_EPS = 1e-5
DEPTH = 1
DN_ALPHA = (2.0 * DEPTH) ** 0.25
SCALE = HEAD_DIM ** -0.5
LOG2E = 1.4426950408889634

_SPLITS = (Q_WIDTH, KV_WIDTH, KV_WIDTH, Q_WIDTH, Q_WIDTH, KV_WIDTH, KV_WIDTH, Q_WIDTH, D_MODEL, D_MODEL)
_OFF = tuple(int(v) for v in np.cumsum((0,) + _SPLITS))
(QA, KA, VA, ZA, QB, KB, VB, ZB, GA, GB) = range(10)

MXU_TILE = 256
F32_SUBLANES = 8
BF16_SUBLANES = 16
VMEM_BYTES = 64 * 1024 * 1024
VMEM_LIMIT = VMEM_BYTES * 7 // 8

V_AUG = HEAD_DIM + BF16_SUBLANES

BF16 = jnp.bfloat16
F32 = jnp.float32


def _cols(w, *ids):
    return jnp.concatenate([w[:, _OFF[i]:_OFF[i + 1]] for i in ids], axis=1)


def _prep_kernel(w_ref, wt_out, wk_out):
    w = w_ref[0]
    wt_out[...] = _cols(w, QA, VA, QB, VB).T.astype(BF16)
    wk_out[...] = _cols(w, KA, KB).astype(BF16)


def _prep_qkv_weights(w_in, layer, steps=2):
    d = w_in.shape[1]
    t_width = 2 * (Q_WIDTH + KV_WIDTH)
    return pl.pallas_call(
        _prep_kernel,
        grid=(steps,),
        in_specs=[pl.BlockSpec((1, d // steps, _OFF[ZB]), lambda i: (layer, i, 0))],
        out_specs=[pl.BlockSpec((t_width, d // steps), lambda i: (0, i)),
                   pl.BlockSpec((d // steps, 2 * KV_WIDTH), lambda i: (i, 0))],
        out_shape=[jax.ShapeDtypeStruct((t_width, d), BF16), jax.ShapeDtypeStruct((d, 2 * KV_WIDTH), BF16)],
        compiler_params=pltpu.CompilerParams(
            dimension_semantics=("parallel",), vmem_limit_bytes=VMEM_LIMIT),
        name="prep_qkv_weights",
    )(w_in)


def _qkv_kernel(x_ref, wt_ref, wk_ref, gq_ref, gk_ref, cosT_ref, sinT_ref, cosk_ref, sink_ref,
                qaT_ref, ka_ref, vaT_ref, qbT_ref, kb_ref, vbT_ref, *, sub):
    half = HEAD_DIM // 2
    ones = jnp.ones((V_AUG - HEAD_DIM, sub), BF16)
    eye = (lax.broadcasted_iota(jnp.int32, (HEAD_DIM, HEAD_DIM), 0)
           == lax.broadcasted_iota(jnp.int32, (HEAD_DIM, HEAD_DIM), 1))
    gq = jnp.sum(jnp.where(eye, gq_ref[...], 0.0), axis=1, keepdims=True)[None]
    gk = jnp.concatenate([gk_ref[...]] * KV_HEADS, axis=1)

    for lo in range(0, x_ref.shape[1], sub):
        tok = slice(lo, lo + sub)
        xb = x_ref[0, tok, :].astype(BF16)
        t = lax.dot_general(wt_ref[...], xb, (((1,), (1,)), ((), ())), preferred_element_type=F32)
        kk = jnp.dot(xb, wk_ref[...], preferred_element_type=F32)

        def store_v(vT_ref, vT):
            for g in range(KV_HEADS):
                vT_ref[0, g, 0:HEAD_DIM, tok] = vT[g * HEAD_DIM:(g + 1) * HEAD_DIM].astype(BF16)
                vT_ref[0, g, HEAD_DIM:V_AUG, tok] = ones

        r0 = 0
        qaT_ref[0, :, tok] = (t[r0:r0 + Q_WIDTH] * (SCALE * LOG2E)).astype(BF16)
        r0 += Q_WIDTH
        store_v(vaT_ref, t[r0:r0 + KV_WIDTH])
        r0 += KV_WIDTH
        qb = t[r0:r0 + Q_WIDTH].reshape(HEADS, HEAD_DIM, sub)
        r0 += Q_WIDTH
        store_v(vbT_ref, t[r0:r0 + KV_WIDTH])

        ms = jnp.mean(qb * qb, axis=1, keepdims=True)
        qn = qb * lax.rsqrt(ms + QK_EPS) * gq
        q1, q2 = qn[:, :half], qn[:, half:]
        cosT, sinT = cosT_ref[:, tok][None], sinT_ref[:, tok][None]
        qr = jnp.concatenate([q1 * cosT - q2 * sinT, q2 * cosT + q1 * sinT], axis=1)
        qbT_ref[0, :, tok] = (qr * (SCALE * LOG2E)).reshape(Q_WIDTH, sub).astype(BF16)

        ka_ref[0, tok, :] = kk[:, :KV_WIDTH].astype(BF16)

        kb = kk[:, KV_WIDTH:]
        lane = lax.broadcasted_iota(jnp.int32, kb.shape, 1)
        k2 = kb * kb
        first = lane < HEAD_DIM
        ms0 = jnp.sum(jnp.where(first, k2, 0.0), axis=1, keepdims=True) * (1.0 / HEAD_DIM)
        ms1 = jnp.sum(jnp.where(first, 0.0, k2), axis=1, keepdims=True) * (1.0 / HEAD_DIM)
        kn = kb * lax.rsqrt(jnp.where(first, ms0, ms1) + QK_EPS) * gk
        swapped = jnp.where(lane % HEAD_DIM < half,
                            pltpu.roll(kn, KV_WIDTH - half, axis=1),
                            pltpu.roll(kn, half, axis=1))
        kb_ref[0, tok, :] = (kn * cosk_ref[tok, :] + swapped * sink_ref[tok, :]).astype(BF16)


def _qkv_proj(x, wt, wk, gq, gk, cosT, sinT, cosk, sink, tm, sub):
    bsz, seq, d = x.shape
    nt = seq // tm
    const = lambda shape: pl.BlockSpec(shape, lambda b, i: (0,) * len(shape))
    qT_spec = pl.BlockSpec((1, Q_WIDTH, tm), lambda b, i: (b, 0, i))
    k_spec = pl.BlockSpec((1, tm, KV_WIDTH), lambda b, i: (b, i, 0))
    vT_spec = pl.BlockSpec((1, KV_HEADS, V_AUG, tm), lambda b, i: (b, 0, 0, i))
    qT_shape = jax.ShapeDtypeStruct((bsz, Q_WIDTH, seq), BF16)
    k_shape = jax.ShapeDtypeStruct((bsz, seq, KV_WIDTH), BF16)
    vT_shape = jax.ShapeDtypeStruct((bsz, KV_HEADS, V_AUG, seq), BF16)
    return pl.pallas_call(
        functools.partial(_qkv_kernel, sub=sub),
        grid=(bsz, nt),
        in_specs=[
            pl.BlockSpec((1, tm, d), lambda b, i: (b, i, 0)),
            const(wt.shape), const(wk.shape), const(gq.shape), const(gk.shape),
            pl.BlockSpec((HEAD_DIM // 2, tm), lambda b, i: (0, i)),
            pl.BlockSpec((HEAD_DIM // 2, tm), lambda b, i: (0, i)),
            pl.BlockSpec((tm, KV_WIDTH), lambda b, i: (i, 0)),
            pl.BlockSpec((tm, KV_WIDTH), lambda b, i: (i, 0)),
        ],
        out_specs=[qT_spec, k_spec, vT_spec, qT_spec, k_spec, vT_spec],
        out_shape=[qT_shape, k_shape, vT_shape, qT_shape, k_shape, vT_shape],
        compiler_params=pltpu.CompilerParams(
            dimension_semantics=("parallel", "parallel"), vmem_limit_bytes=VMEM_LIMIT),
        name="qkv_proj",
    )(x, wt, wk, gq, gk, cosT, sinT, cosk, sink)


def _padded_q(qT_h, g):
    zeros = jnp.zeros_like(qT_h)
    parts = [zeros] * KV_HEADS
    parts[g] = qT_h
    return jnp.concatenate(parts, axis=0)


def _aligned(v, m):
    return v if isinstance(v, int) else pl.multiple_of(v, m)


def _fold_max(s):
    rows, n = s.shape
    while rows % (2 * F32_SUBLANES) == 0:
        rows //= 2
        s = jnp.maximum(s[:rows], s[rows:])
    return s if rows == F32_SUBLANES else jnp.max(s.reshape(rows // F32_SUBLANES, F32_SUBLANES, n), axis=0)


def _normalise(oT, extra=None):
    denom = oT[HEAD_DIM:HEAD_DIM + 1]
    if extra is not None:
        denom = denom + extra
    return oT[:HEAD_DIM] * (1.0 / denom)


def _alibi_table():
    r = np.arange(5 * BLOCK)[:, None]
    c = np.arange(BLOCK)[None, :]
    dist = np.abs(r - 2 * BLOCK - c).astype(np.float64)
    slopes = 2.0 ** (-8.0 * (np.arange(HEADS) + 1.0) / HEADS) * LOG2E
    bias = np.where(dist <= WINDOW, -slopes[:, None, None] * dist, -np.inf)
    return jnp.asarray(bias, F32)


def _attn_kernel(sink_ref, bias_ref, qaT_ref, ka_ref, vaT_ref, qbT_ref, kb_ref, vbT_ref,
                 w_ref, wpa_ref, wpb_ref, wo_ref,
                 ya_ref, yb_ref, wz_out, wg_out, wpa_out, wpb_out, wo_out, sa_ref, sb_ref, *, tq, kc):
    seq = kb_ref.shape[1]
    gw = GROUP * HEAD_DIM
    n_tiles = seq // tq
    chunks = range(0, seq, kc)
    nblk = seq // BLOCK
    span = 3 * BLOCK
    gn = GROUP * BLOCK
    a_steps = HEADS // KV_HEADS
    assert nblk == a_steps * n_tiles

    def sink_row(g):
        return jnp.concatenate(
            [jnp.full((1, BLOCK), sink_ref[g * GROUP + j] * LOG2E, F32) for j in range(GROUP)], axis=1)

    def window(n):
        if isinstance(n, int):
            q_lo, ws = n * BLOCK, min(max((n - 1) * BLOCK, 0), seq - span)
            return q_lo, ws, 2 * BLOCK - (q_lo - ws)
        q_lo = pl.multiple_of(n * BLOCK, BLOCK)
        ws = pl.multiple_of(jnp.clip((n - 1) * BLOCK, 0, seq - span), BLOCK)
        return q_lo, ws, pl.multiple_of(2 * BLOCK - (q_lo - ws), BLOCK)

    def a_scores(n, slot, g):
        q_lo, ws, b_lo = window(n)
        qT = qaT_ref[0, g * gw:(g + 1) * gw, pl.ds(q_lo, BLOCK)]
        w = jnp.concatenate(
            [_padded_q(qT[j * HEAD_DIM:(j + 1) * HEAD_DIM], g) for j in range(GROUP)], axis=1)
        sT = jnp.dot(ka_ref[0, pl.ds(ws, span), :], w, preferred_element_type=F32)
        ms = []
        for j in range(GROUP):
            s = sT[:, j * BLOCK:(j + 1) * BLOCK] + bias_ref[g * GROUP + j, pl.ds(b_lo, span), :]
            sa_ref[slot, :, g * gn + j * BLOCK:g * gn + (j + 1) * BLOCK] = s
            ms.append(jnp.max(_fold_max(s), axis=0, keepdims=True))
        return jnp.maximum(jnp.concatenate(ms, axis=1), sink_row(g))

    def a_values(n, slot, g, m):
        _, ws, _ = window(n)
        pT = jnp.exp2(sa_ref[slot, :, g * gn:(g + 1) * gn] - m).astype(BF16)
        oT = jnp.dot(vaT_ref[0, g, :, pl.ds(ws, span)], pT, preferred_element_type=F32)
        o = _normalise(oT, jnp.exp2(sink_row(g) - m))
        return [o[:, j * BLOCK:(j + 1) * BLOCK] for j in range(GROUP)]

    def a_item(n, slot, g, m_prev, outs, do_scores=True):
        m = a_scores(n, slot, g) if do_scores else None
        outs = outs + a_values(n - 1, 1 - slot, g, m_prev)
        if g == KV_HEADS - 1:
            q_lo, _, _ = window(n - 1)
            ya_ref[0, pl.ds(q_lo, BLOCK), :] = jnp.concatenate(outs, axis=0).T.astype(ya_ref.dtype)
            outs = []
        return m, outs

    def padded_q(t, h):
        qT_h = qbT_ref[0, h * HEAD_DIM:(h + 1) * HEAD_DIM, pl.ds(_aligned(t * tq, tq), tq)]
        return _padded_q(qT_h, h // GROUP)

    def scores_chunk(slot, w, k_lo, m8):
        sT = jnp.dot(kb_ref[0, k_lo:k_lo + kc, :], w, preferred_element_type=F32)
        sb_ref[slot, k_lo:k_lo + kc, :] = sT
        m8_c = _fold_max(sT)
        return m8_c if m8 is None else jnp.maximum(m8, m8_c)

    def values_chunk(g, slot, m, k_lo, acc):
        pT = jnp.exp2(sb_ref[slot, k_lo:k_lo + kc, :] - m).astype(BF16)
        oT = jnp.dot(vbT_ref[0, g, :, k_lo:k_lo + kc], pT, preferred_element_type=F32)
        return oT if acc is None else acc + oT

    def cast_weight_rows(t):
        def rows(ref):
            n = ref.shape[-2] // n_tiles
            return pl.ds(_aligned(t * n, n), n)

        w = w_ref[0, rows(w_ref), :]
        wz_out[rows(wz_out), :] = _cols(w, ZA, ZB).astype(BF16)
        wg_out[rows(wg_out), :] = _cols(w, GA, GB).astype(BF16)
        wpa_out[rows(wpa_out), :] = wpa_ref[0, rows(wpa_ref), :].astype(BF16)
        wpb_out[rows(wpb_out), :] = wpb_ref[0, rows(wpb_ref), :].astype(BF16)
        wo_out[rows(wo_out), :] = wo_ref[0, rows(wo_ref), :].astype(BF16)

    def fill():
        w, m8 = padded_q(0, 0), None
        for k_lo in chunks:
            m8 = scores_chunk(0, w, k_lo, m8)
        return (jnp.max(m8, axis=0, keepdims=True),) + tuple(a_scores(0, 0, g) for g in range(KV_HEADS))

    def tile(t, carry, last=False):
        m, ma = carry[0], list(carry[1:])
        cast_weight_rows(t)
        outs, a_outs, ma_next = [], [], [None] * KV_HEADS
        for h in range(HEADS):
            nxt = (t, h + 1) if h + 1 < HEADS else (None if last else (t + 1, 0))
            w = padded_q(*nxt) if nxt else None
            m8 = acc = None
            for k_lo in chunks:
                if nxt:
                    m8 = scores_chunk((h + 1) % 2, w, k_lo, m8)
                acc = values_chunk(h // GROUP, h % 2, m, k_lo, acc)
            outs.append(_normalise(acc))
            m = jnp.max(m8, axis=0, keepdims=True) if nxt else None
            if h % GROUP == GROUP - 1:
                g = h // GROUP
                yb_ref[0, pl.ds(_aligned(t * tq, tq), tq), g * gw:(g + 1) * gw] = (
                    jnp.concatenate(outs, axis=0).T.astype(yb_ref.dtype))
                outs = []

            i, ga = h // KV_HEADS, h % KV_HEADS
            n = a_steps * t + i + 1
            a_has_scores = not (last and i == a_steps - 1)
            ma_next[ga], a_outs = a_item(n, (i + 1) % 2, ga, ma[ga], a_outs, a_has_scores)
            if ga == KV_HEADS - 1:
                ma, ma_next = ma_next, [None] * KV_HEADS
        return None if last else (m,) + tuple(ma)

    carry = fill()
    carry = lax.fori_loop(0, n_tiles - 1, tile, carry)
    tile(n_tiles - 1, carry, last=True)


def _attention(sink, qaT, ka, vaT, qbT, kb, vbT, w_in, w_proj_a, w_proj_b, w_out, layer, tq, kc):
    bsz, _, seq = qbT.shape
    d = w_in.shape[1]
    bias = _alibi_table()
    qT_spec = pl.BlockSpec((1, Q_WIDTH, seq), lambda b: (b, 0, 0))
    k_spec = pl.BlockSpec((1, seq, KV_WIDTH), lambda b: (b, 0, 0))
    vT_spec = pl.BlockSpec((1, KV_HEADS, V_AUG, seq), lambda b: (b, 0, 0, 0))
    y_spec = pl.BlockSpec((1, seq, Q_WIDTH), lambda b: (b, 0, 0))
    y_shape = jax.ShapeDtypeStruct((bsz, seq, Q_WIDTH), BF16)
    slab_in = lambda a: pl.BlockSpec((1, a.shape[1] // bsz, a.shape[2]), lambda b: (layer, b, 0))
    slab_out = lambda rows, width: pl.BlockSpec((rows // bsz, width), lambda b: (b, 0))
    w_shapes = [(d, 2 * Q_WIDTH), (d, 2 * D_MODEL), w_proj_a.shape[1:], w_proj_b.shape[1:], w_out.shape[1:]]
    outs = pl.pallas_call(
        functools.partial(_attn_kernel, tq=tq, kc=kc),
        grid=(bsz,),
        in_specs=[
            pl.BlockSpec(memory_space=pltpu.SMEM),
            pl.BlockSpec(bias.shape, lambda b: (0, 0, 0)),
            qT_spec, k_spec, vT_spec, qT_spec, k_spec, vT_spec,
            slab_in(w_in), slab_in(w_proj_a), slab_in(w_proj_b), slab_in(w_out),
        ],
        out_specs=[y_spec, y_spec] + [slab_out(*s) for s in w_shapes],
        out_shape=[y_shape, y_shape] + [jax.ShapeDtypeStruct(s, BF16) for s in w_shapes],
        scratch_shapes=[pltpu.VMEM((2, 3 * BLOCK, HEADS * BLOCK), F32), pltpu.VMEM((2, seq, tq), F32)],
        compiler_params=pltpu.CompilerParams(
            dimension_semantics=("parallel",), vmem_limit_bytes=VMEM_LIMIT),
        name="attention",
    )(sink, bias, qaT, ka, vaT, qbT, kb, vbT, w_in, w_proj_a, w_proj_b, w_out)
    return outs[0], outs[1], outs[2:]


def _out_kernel(x_ref, ya_ref, yb_ref, wz_ref, wg_ref, bg_ref, wpa_ref, wpb_ref, wo_ref, lng_ref, lnb_ref,
                o_ref, *, sub):
    def input_proj(rows):
        xb = x_ref[rows, :].astype(BF16)
        z = jnp.dot(xb, wz_ref[...], preferred_element_type=F32)
        gl = jnp.dot(xb, wg_ref[...], preferred_element_type=F32)
        return z, gl

    def branch_proj(rows, z):
        sz = z * jax.nn.sigmoid(z)
        ua = (ya_ref[rows, :].astype(F32) * sz[:, :Q_WIDTH]).astype(BF16)
        ub = (yb_ref[rows, :].astype(F32) * sz[:, Q_WIDTH:]).astype(BF16)
        pa = jnp.dot(ua, wpa_ref[...], preferred_element_type=F32)
        pb = jnp.dot(ub, wpb_ref[...], preferred_element_type=F32)
        return pa, pb

    def merge_norm(rows, gl, pa, pb):
        gate = jax.nn.sigmoid(gl + bg_ref[...])
        mix = (gate[:, :D_MODEL] * pa + gate[:, D_MODEL:] * pb).astype(BF16)
        y = jnp.dot(mix, wo_ref[...], preferred_element_type=F32)
        r = DN_ALPHA * x_ref[rows, :] + y
        mu = jnp.mean(r, axis=-1, keepdims=True)
        rc = r - mu
        var = jnp.mean(rc * rc, axis=-1, keepdims=True)
        o_ref[rows, :] = rc * lax.rsqrt(var + LN_EPS) * lng_ref[...] + lnb_ref[...]

    tiles = [slice(lo, lo + sub) for lo in range(0, x_ref.shape[0], sub)]
    stage1, stage2 = {}, {}
    for step in range(len(tiles) + 2):
        if step < len(tiles):
            stage1[step] = input_proj(tiles[step])
        if 0 <= step - 1 < len(tiles):
            stage2[step - 1] = branch_proj(tiles[step - 1], stage1[step - 1][0])
        if 0 <= step - 2 < len(tiles):
            merge_norm(tiles[step - 2], stage1.pop(step - 2)[1], *stage2.pop(step - 2))


def _out_block(x2, ya2, yb2, wz, wg, bg, wpa, wpb, wo, lng, lnb, tm, sub):
    n, d = x2.shape
    const = lambda a: pl.BlockSpec(a.shape, lambda i: (0, 0), pipeline_mode=pl.Buffered(1))
    row = lambda w: pl.BlockSpec((tm, w), lambda i: (i, 0))
    return pl.pallas_call(
        functools.partial(_out_kernel, sub=sub),
        grid=(n // tm,),
        in_specs=[row(d), row(Q_WIDTH), row(Q_WIDTH),
                  const(wz), const(wg), const(bg), const(wpa), const(wpb), const(wo), const(lng), const(lnb)],
        out_specs=row(d),
        out_shape=jax.ShapeDtypeStruct((n, d), x2.dtype),
        compiler_params=pltpu.CompilerParams(
            dimension_semantics=("parallel",), vmem_limit_bytes=VMEM_LIMIT),
        name="out_block",
    )(x2, ya2, yb2, wz, wg, bg, wpa, wpb, wo, lng, lnb)


def _rope_tables(seq):
    half = HEAD_DIM // 2
    axis_pairs = half // 2
    pos = np.arange(seq)
    freqs = ROPE_THETA ** (-np.arange(axis_pairs, dtype=np.float64) / axis_pairs)
    ang = np.concatenate([(pos // GRID_W)[:, None] * freqs, (pos % GRID_W)[:, None] * freqs], axis=-1)
    cos, sin = np.cos(ang), np.sin(ang)
    cosk = np.tile(cos, (1, KV_WIDTH // half))
    sink = np.tile(np.concatenate([-sin, sin], axis=-1), (1, KV_HEADS))
    return tuple(jnp.asarray(a, F32) for a in (cos.T, sin.T, cosk, sink))


def _layer(x, layer, w_in, b_gate, sink_a, qnorm_b, knorm_b, w_proj_a, w_proj_b, w_out, ln_g, ln_b):
    bsz, seq, d = x.shape
    tm_qkv, sub_qkv, tm_out, sub_out = 2048, MXU_TILE, 1024, MXU_TILE
    tq, kc = 2 * MXU_TILE, MXU_TILE

    wt, wk = _prep_qkv_weights(w_in, layer)
    gq = qnorm_b[layer].astype(F32)[None, :]
    gk = knorm_b[layer].astype(F32)[None, :]
    cosT, sinT, cosk, sink = _rope_tables(seq)

    qaT, ka, vaT, qbT, kb, vbT = _qkv_proj(x, wt, wk, gq, gk, cosT, sinT, cosk, sink, tm_qkv, sub_qkv)
    ya, yb, (wz, wg, wpa, wpb, wo) = _attention(
        sink_a[layer].astype(F32), qaT, ka, vaT, qbT, kb, vbT, w_in, w_proj_a, w_proj_b, w_out, layer, tq, kc)

    out = _out_block(
        x.reshape(bsz * seq, d), ya.reshape(bsz * seq, Q_WIDTH), yb.reshape(bsz * seq, Q_WIDTH),
        wz, wg, b_gate[layer].astype(F32)[None, :], wpa, wpb, wo,
        ln_g[layer].astype(F32)[None, :], ln_b[layer].astype(F32)[None, :], tm_out, sub_out)
    return out.reshape(bsz, seq, d)


def kernel(x, w_in, b_gate, sink_a, qnorm_b, knorm_b, w_proj_a, w_proj_b, w_out, ln_g, ln_b):
    h = x
    for layer in range(DEPTH):
        h = _layer(h, layer, w_in, b_gate, sink_a, qnorm_b, knorm_b, w_proj_a, w_proj_b, w_out, ln_g, ln_b)
    return h
```

```python
import functools

import jax
import jax.numpy as jnp
import numpy as np
from jax import lax
from jax.experimental import pallas as pl
from jax.experimental.pallas import tpu as pltpu

D_MODEL = 1024
HEAD_DIM = 64
HEADS = 8
KV_HEADS = 2
GROUP = HEADS // KV_HEADS
Q_WIDTH = HEADS * HEAD_DIM
KV_WIDTH = KV_HEADS * HEAD_DIM
WINDOW = 128
BLOCK = 128
GRID_W = 64
ROPE_THETA = 10000.0
QK_EPS = 1e-6
LN_EPS = 1e-5
DEPTH = 1
DN_ALPHA = (2.0 * DEPTH) ** 0.25
SCALE = HEAD_DIM ** -0.5
LOG2E = 1.4426950408889634

_SPLITS = (Q_WIDTH, KV_WIDTH, KV_WIDTH, Q_WIDTH, Q_WIDTH, KV_WIDTH, KV_WIDTH, Q_WIDTH, D_MODEL, D_MODEL)
_OFF = tuple(int(v) for v in np.cumsum((0,) + _SPLITS))
(QA, KA, VA, ZA, QB, KB, VB, ZB, GA, GB) = range(10)

MXU_TILE = 256
F32_SUBLANES = 8
BF16_SUBLANES = 16
VMEM_BYTES = 64 * 1024 * 1024
VMEM_LIMIT = VMEM_BYTES * 7 // 8

V_AUG = HEAD_DIM + BF16_SUBLANES

BF16 = jnp.bfloat16
F32 = jnp.float32


def _cols(w, *ids):
    return jnp.concatenate([w[:, _OFF[i]:_OFF[i + 1]] for i in ids], axis=1)


def _prep_kernel(w_ref, wt_out, wk_out):
    w = w_ref[0]
    wt_out[...] = _cols(w, QA, VA, QB, VB).T.astype(BF16)
    wk_out[...] = _cols(w, KA, KB).astype(BF16)


def _prep_qkv_weights(w_in, layer, steps=2):
    d = w_in.shape[1]
    t_width = 2 * (Q_WIDTH + KV_WIDTH)
    return pl.pallas_call(
        _prep_kernel,
        grid=(steps,),
        in_specs=[pl.BlockSpec((1, d // steps, _OFF[ZB]), lambda i: (layer, i, 0))],
        out_specs=[pl.BlockSpec((t_width, d // steps), lambda i: (0, i)),
                   pl.BlockSpec((d // steps, 2 * KV_WIDTH), lambda i: (i, 0))],
        out_shape=[jax.ShapeDtypeStruct((t_width, d), BF16), jax.ShapeDtypeStruct((d, 2 * KV_WIDTH), BF16)],
        compiler_params=pltpu.CompilerParams(
            dimension_semantics=("parallel",), vmem_limit_bytes=VMEM_LIMIT),
        name="prep_qkv_weights",
    )(w_in)


def _qkv_kernel(x_ref, wt_ref, wk_ref, gq_ref, gk_ref, cosT_ref, sinT_ref, cosk_ref, sink_ref,
                qaT_ref, ka_ref, vaT_ref, qbT_ref, kb_ref, vbT_ref, *, sub):
    half = HEAD_DIM // 2
    ones = jnp.ones((V_AUG - HEAD_DIM, sub), BF16)
    eye = (lax.broadcasted_iota(jnp.int32, (HEAD_DIM, HEAD_DIM), 0)
           == lax.broadcasted_iota(jnp.int32, (HEAD_DIM, HEAD_DIM), 1))
    gq = jnp.sum(jnp.where(eye, gq_ref[...], 0.0), axis=1, keepdims=True)[None]
    gk = jnp.concatenate([gk_ref[...]] * KV_HEADS, axis=1)

    for lo in range(0, x_ref.shape[1], sub):
        tok = slice(lo, lo + sub)
        xb = x_ref[0, tok, :].astype(BF16)
        nt_dot = lambda w: lax.dot_general(w, xb, (((1,), (1,)), ((), ())), preferred_element_type=F32)
        hw = Q_WIDTH + KV_WIDTH
        t_b = nt_dot(wt_ref[hw:2 * hw, :])
        kk = jnp.dot(xb, wk_ref[...], preferred_element_type=F32)
        t = jnp.concatenate([nt_dot(wt_ref[0:hw, :]), t_b], axis=0)

        def store_v(vT_ref, vT):
            for g in range(KV_HEADS):
                vT_ref[0, g, 0:HEAD_DIM, tok] = vT[g * HEAD_DIM:(g + 1) * HEAD_DIM].astype(BF16)
                vT_ref[0, g, HEAD_DIM:V_AUG, tok] = ones

        r0 = 0
        qaT_ref[0, :, tok] = (t[r0:r0 + Q_WIDTH] * (SCALE * LOG2E)).astype(BF16)
        r0 += Q_WIDTH
        store_v(vaT_ref, t[r0:r0 + KV_WIDTH])
        r0 += KV_WIDTH
        qb = t[r0:r0 + Q_WIDTH].reshape(HEADS, HEAD_DIM, sub)
        r0 += Q_WIDTH
        store_v(vbT_ref, t[r0:r0 + KV_WIDTH])

        ms = jnp.mean(qb * qb, axis=1, keepdims=True)
        qn = qb * lax.rsqrt(ms + QK_EPS) * gq
        q1, q2 = qn[:, :half], qn[:, half:]
        cosT, sinT = cosT_ref[:, tok][None], sinT_ref[:, tok][None]
        qr = jnp.concatenate([q1 * cosT - q2 * sinT, q2 * cosT + q1 * sinT], axis=1)
        qbT_ref[0, :, tok] = (qr * (SCALE * LOG2E)).reshape(Q_WIDTH, sub).astype(BF16)

        ka_ref[0, tok, :] = kk[:, :KV_WIDTH].astype(BF16)

        kb = kk[:, KV_WIDTH:]
        lane = lax.broadcasted_iota(jnp.int32, kb.shape, 1)
        k2 = kb * kb
        first = lane < HEAD_DIM
        ms0 = jnp.sum(jnp.where(first, k2, 0.0), axis=1, keepdims=True) * (1.0 / HEAD_DIM)
        ms1 = jnp.sum(jnp.where(first, 0.0, k2), axis=1, keepdims=True) * (1.0 / HEAD_DIM)
        kn = kb * lax.rsqrt(jnp.where(first, ms0, ms1) + QK_EPS) * gk
        swapped = jnp.where(lane % HEAD_DIM < half,
                            pltpu.roll(kn, KV_WIDTH - half, axis=1),
                            pltpu.roll(kn, half, axis=1))
        kb_ref[0, tok, :] = (kn * cosk_ref[tok, :] + swapped * sink_ref[tok, :]).astype(BF16)


def _qkv_proj(x, wt, wk, gq, gk, cosT, sinT, cosk, sink, tm, sub):
    bsz, seq, d = x.shape
    nt = seq // tm
    const = lambda shape: pl.BlockSpec(shape, lambda b, i: (0,) * len(shape))
    qT_spec = pl.BlockSpec((1, Q_WIDTH, tm), lambda b, i: (b, 0, i))
    k_spec = pl.BlockSpec((1, tm, KV_WIDTH), lambda b, i: (b, i, 0))
    vT_spec = pl.BlockSpec((1, KV_HEADS, V_AUG, tm), lambda b, i: (b, 0, 0, i))
    qT_shape = jax.ShapeDtypeStruct((bsz, Q_WIDTH, seq), BF16)
    k_shape = jax.ShapeDtypeStruct((bsz, seq, KV_WIDTH), BF16)
    vT_shape = jax.ShapeDtypeStruct((bsz, KV_HEADS, V_AUG, seq), BF16)
    return pl.pallas_call(
        functools.partial(_qkv_kernel, sub=sub),
        grid=(bsz, nt),
        in_specs=[
            pl.BlockSpec((1, tm, d), lambda b, i: (b, i, 0)),
            const(wt.shape), const(wk.shape), const(gq.shape), const(gk.shape),
            pl.BlockSpec((HEAD_DIM // 2, tm), lambda b, i: (0, i)),
            pl.BlockSpec((HEAD_DIM // 2, tm), lambda b, i: (0, i)),
            pl.BlockSpec((tm, KV_WIDTH), lambda b, i: (i, 0)),
            pl.BlockSpec((tm, KV_WIDTH), lambda b, i: (i, 0)),
        ],
        out_specs=[qT_spec, k_spec, vT_spec, qT_spec, k_spec, vT_spec],
        out_shape=[qT_shape, k_shape, vT_shape, qT_shape, k_shape, vT_shape],
        compiler_params=pltpu.CompilerParams(
            dimension_semantics=("parallel", "parallel"), vmem_limit_bytes=VMEM_LIMIT),
        name="qkv_proj",
    )(x, wt, wk, gq, gk, cosT, sinT, cosk, sink)


def _padded_q(qT_h, g):
    zeros = jnp.zeros_like(qT_h)
    parts = [zeros] * KV_HEADS
    parts[g] = qT_h
    return jnp.concatenate(parts, axis=0)


def _aligned(v, m):
    return v if isinstance(v, int) else pl.multiple_of(v, m)


def _fold_max(s):
    rows, n = s.shape
    while rows % (2 * F32_SUBLANES) == 0:
        rows //= 2
        s = jnp.maximum(s[:rows], s[rows:])
    return s if rows == F32_SUBLANES else jnp.max(s.reshape(rows // F32_SUBLANES, F32_SUBLANES, n), axis=0)


def _normalise(oT, extra=None):
    denom = oT[HEAD_DIM:HEAD_DIM + 1]
    if extra is not None:
        denom = denom + extra
    return oT[:HEAD_DIM] * (1.0 / denom)


def _alibi_table():
    r = np.arange(5 * BLOCK)[:, None]
    c = np.arange(BLOCK)[None, :]
    dist = np.abs(r - 2 * BLOCK - c).astype(np.float64)
    slopes = 2.0 ** (-8.0 * (np.arange(HEADS) + 1.0) / HEADS) * LOG2E
    bias = np.where(dist <= WINDOW, -slopes[:, None, None] * dist, -np.inf)
    return jnp.asarray(bias, F32)


def _attn_kernel(sink_ref, bias_ref, qaT_ref, ka_ref, vaT_ref, qbT_ref, kb_ref, vbT_ref,
                 w_ref, wpa_ref, wpb_ref, wo_ref,
                 ya_ref, yb_ref, wz_out, wg_out, wpa_out, wpb_out, wo_out, sa_ref, sb_ref, *, tq, kc):
    seq = kb_ref.shape[1]
    gw = GROUP * HEAD_DIM
    n_tiles = seq // tq
    chunks = range(0, seq, kc)
    nblk = seq // BLOCK
    span = 3 * BLOCK
    gn = GROUP * BLOCK
    a_steps = HEADS // KV_HEADS
    assert nblk == a_steps * n_tiles

    def sink_row(g):
        return jnp.concatenate(
            [jnp.full((1, BLOCK), sink_ref[g * GROUP + j] * LOG2E, F32) for j in range(GROUP)], axis=1)

    def window(n):
        if isinstance(n, int):
            q_lo, ws = n * BLOCK, min(max((n - 1) * BLOCK, 0), seq - span)
            return q_lo, ws, 2 * BLOCK - (q_lo - ws)
        q_lo = pl.multiple_of(n * BLOCK, BLOCK)
        ws = pl.multiple_of(jnp.clip((n - 1) * BLOCK, 0, seq - span), BLOCK)
        return q_lo, ws, pl.multiple_of(2 * BLOCK - (q_lo - ws), BLOCK)

    def a_scores(n, slot, g):
        q_lo, ws, b_lo = window(n)
        qT = qaT_ref[0, g * gw:(g + 1) * gw, pl.ds(q_lo, BLOCK)]
        w = jnp.concatenate(
            [_padded_q(qT[j * HEAD_DIM:(j + 1) * HEAD_DIM], g) for j in range(GROUP)], axis=1)
        sT = jnp.dot(ka_ref[0, pl.ds(ws, span), :], w, preferred_element_type=F32)
        ms = []
        for j in range(GROUP):
            s = sT[:, j * BLOCK:(j + 1) * BLOCK] + bias_ref[g * GROUP + j, pl.ds(b_lo, span), :]
            sa_ref[slot, :, g * gn + j * BLOCK:g * gn + (j + 1) * BLOCK] = s
            ms.append(jnp.max(_fold_max(s), axis=0, keepdims=True))
        return jnp.maximum(jnp.concatenate(ms, axis=1), sink_row(g))

    def a_values(n, slot, g, m):
        _, ws, _ = window(n)
        pT = jnp.exp2(sa_ref[slot, :, g * gn:(g + 1) * gn] - m).astype(BF16)
        oT = jnp.dot(vaT_ref[0, g, :, pl.ds(ws, span)], pT, preferred_element_type=F32)
        o = _normalise(oT, jnp.exp2(sink_row(g) - m))
        return [o[:, j * BLOCK:(j + 1) * BLOCK] for j in range(GROUP)]

    def a_item(n, slot, g, m_prev, outs, do_scores=True):
        m = a_scores(n, slot, g) if do_scores else None
        outs = outs + a_values(n - 1, 1 - slot, g, m_prev)
        if g == KV_HEADS - 1:
            q_lo, _, _ = window(n - 1)
            ya_ref[0, pl.ds(q_lo, BLOCK), :] = jnp.concatenate(outs, axis=0).T.astype(ya_ref.dtype)
            outs = []
        return m, outs

    def padded_q(t, h):
        qT_h = qbT_ref[0, h * HEAD_DIM:(h + 1) * HEAD_DIM, pl.ds(_aligned(t * tq, tq), tq)]
        return _padded_q(qT_h, h // GROUP)

    def scores_chunk(slot, w, k_lo, m8):
        sT = jnp.dot(kb_ref[0, k_lo:k_lo + kc, :], w, preferred_element_type=F32)
        sb_ref[slot, k_lo:k_lo + kc, :] = sT
        m8_c = _fold_max(sT)
        return m8_c if m8 is None else jnp.maximum(m8, m8_c)

    def values_chunk(g, slot, m, k_lo, acc):
        pT = jnp.exp2(sb_ref[slot, k_lo:k_lo + kc, :] - m).astype(BF16)
        oT = jnp.dot(vbT_ref[0, g, :, k_lo:k_lo + kc], pT, preferred_element_type=F32)
        return oT if acc is None else acc + oT

    def cast_weight_rows(t):
        def rows(ref):
            n = ref.shape[-2] // n_tiles
            return pl.ds(_aligned(t * n, n), n)

        w = w_ref[0, rows(w_ref), :]
        wz_out[rows(wz_out), :] = _cols(w, ZA, ZB).astype(BF16)
        wg_out[rows(wg_out), :] = _cols(w, GA, GB).astype(BF16)
        wpa_out[rows(wpa_out), :] = wpa_ref[0, rows(wpa_ref), :].astype(BF16)
        wpb_out[rows(wpb_out), :] = wpb_ref[0, rows(wpb_ref), :].astype(BF16)
        wo_out[rows(wo_out), :] = wo_ref[0, rows(wo_ref), :].astype(BF16)

    def fill():
        w, m8 = padded_q(0, 0), None
        for k_lo in chunks:
            m8 = scores_chunk(0, w, k_lo, m8)
        return (jnp.max(m8, axis=0, keepdims=True),) + tuple(a_scores(0, 0, g) for g in range(KV_HEADS))

    def tile(t, carry, last=False):
        m, ma = carry[0], list(carry[1:])
        cast_weight_rows(t)
        outs, a_outs, ma_next = [], [], [None] * KV_HEADS
        for h in range(HEADS):
            nxt = (t, h + 1) if h + 1 < HEADS else (None if last else (t + 1, 0))
            w = padded_q(*nxt) if nxt else None
            m8 = acc = None
            for k_lo in chunks:
                if nxt:
                    m8 = scores_chunk((h + 1) % 2, w, k_lo, m8)
                acc = values_chunk(h // GROUP, h % 2, m, k_lo, acc)
            outs.append(_normalise(acc))
            m = jnp.max(m8, axis=0, keepdims=True) if nxt else None
            if h % GROUP == GROUP - 1:
                g = h // GROUP
                yb_ref[0, pl.ds(_aligned(t * tq, tq), tq), g * gw:(g + 1) * gw] = (
                    jnp.concatenate(outs, axis=0).T.astype(yb_ref.dtype))
                outs = []

            i, ga = h // KV_HEADS, h % KV_HEADS
            n = a_steps * t + i + 1
            a_has_scores = not (last and i == a_steps - 1)
            ma_next[ga], a_outs = a_item(n, (i + 1) % 2, ga, ma[ga], a_outs, a_has_scores)
            if ga == KV_HEADS - 1:
                ma, ma_next = ma_next, [None] * KV_HEADS
        return None if last else (m,) + tuple(ma)

    carry = fill()
    carry = lax.fori_loop(0, n_tiles - 1, tile, carry)
    tile(n_tiles - 1, carry, last=True)


def _attention(sink, qaT, ka, vaT, qbT, kb, vbT, w_in, w_proj_a, w_proj_b, w_out, layer, tq, kc):
    bsz, _, seq = qbT.shape
    d = w_in.shape[1]
    bias = _alibi_table()
    qT_spec = pl.BlockSpec((1, Q_WIDTH, seq), lambda b: (b, 0, 0))
    k_spec = pl.BlockSpec((1, seq, KV_WIDTH), lambda b: (b, 0, 0))
    vT_spec = pl.BlockSpec((1, KV_HEADS, V_AUG, seq), lambda b: (b, 0, 0, 0))
    y_spec = pl.BlockSpec((1, seq, Q_WIDTH), lambda b: (b, 0, 0))
    y_shape = jax.ShapeDtypeStruct((bsz, seq, Q_WIDTH), BF16)
    slab_in = lambda a: pl.BlockSpec((1, a.shape[1] // bsz, a.shape[2]), lambda b: (layer, b, 0))
    slab_out = lambda rows, width: pl.BlockSpec((rows // bsz, width), lambda b: (b, 0))
    w_shapes = [(d, 2 * Q_WIDTH), (d, 2 * D_MODEL), w_proj_a.shape[1:], w_proj_b.shape[1:], w_out.shape[1:]]
    outs = pl.pallas_call(
        functools.partial(_attn_kernel, tq=tq, kc=kc),
        grid=(bsz,),
        in_specs=[
            pl.BlockSpec(memory_space=pltpu.SMEM),
            pl.BlockSpec(bias.shape, lambda b: (0, 0, 0)),
            qT_spec, k_spec, vT_spec, qT_spec, k_spec, vT_spec,
            slab_in(w_in), slab_in(w_proj_a), slab_in(w_proj_b), slab_in(w_out),
        ],
        out_specs=[y_spec, y_spec] + [slab_out(*s) for s in w_shapes],
        out_shape=[y_shape, y_shape] + [jax.ShapeDtypeStruct(s, BF16) for s in w_shapes],
        scratch_shapes=[pltpu.VMEM((2, 3 * BLOCK, HEADS * BLOCK), F32), pltpu.VMEM((2, seq, tq), F32)],
        compiler_params=pltpu.CompilerParams(
            dimension_semantics=("parallel",), vmem_limit_bytes=VMEM_LIMIT),
        name="attention",
    )(sink, bias, qaT, ka, vaT, qbT, kb, vbT, w_in, w_proj_a, w_proj_b, w_out)
    return outs[0], outs[1], outs[2:]


def _out_kernel(x_ref, ya_ref, yb_ref, wz_ref, wg_ref, bg_ref, wpa_ref, wpb_ref, wo_ref, lng_ref, lnb_ref,
                o_ref, *, sub):
    def input_proj(rows):
        xb = x_ref[rows, :].astype(BF16)
        z = jnp.dot(xb, wz_ref[...], preferred_element_type=F32)
        gl = jnp.dot(xb, wg_ref[...], preferred_element_type=F32)
        return z, gl

    def branch_proj(rows, z):
        sz = z * jax.nn.sigmoid(z)
        ua = (ya_ref[rows, :].astype(F32) * sz[:, :Q_WIDTH]).astype(BF16)
        ub = (yb_ref[rows, :].astype(F32) * sz[:, Q_WIDTH:]).astype(BF16)
        pa = jnp.dot(ua, wpa_ref[...], preferred_element_type=F32)
        pb = jnp.dot(ub, wpb_ref[...], preferred_element_type=F32)
        return pa, pb

    def merge_norm(rows, gl, pa, pb):
        gate = jax.nn.sigmoid(gl + bg_ref[...])
        mix = (gate[:, :D_MODEL] * pa + gate[:, D_MODEL:] * pb).astype(BF16)
        y = jnp.dot(mix, wo_ref[...], preferred_element_type=F32)
        r = DN_ALPHA * x_ref[rows, :] + y
        mu = jnp.mean(r, axis=-1, keepdims=True)
        rc = r - mu
        var = jnp.mean(rc * rc, axis=-1, keepdims=True)
        o_ref[rows, :] = rc * lax.rsqrt(var + LN_EPS) * lng_ref[...] + lnb_ref[...]

    tiles = [slice(lo, lo + sub) for lo in range(0, x_ref.shape[0], sub)]
    stage1, stage2 = {}, {}
    for step in range(len(tiles) + 2):
        if step < len(tiles):
            stage1[step] = input_proj(tiles[step])
        if 0 <= step - 1 < len(tiles):
            stage2[step - 1] = branch_proj(tiles[step - 1], stage1[step - 1][0])
        if 0 <= step - 2 < len(tiles):
            merge_norm(tiles[step - 2], stage1.pop(step - 2)[1], *stage2.pop(step - 2))


def _out_block(x2, ya2, yb2, wz, wg, bg, wpa, wpb, wo, lng, lnb, tm, sub):
    n, d = x2.shape
    const = lambda a: pl.BlockSpec(a.shape, lambda i: (0, 0), pipeline_mode=pl.Buffered(1))
    row = lambda w: pl.BlockSpec((tm, w), lambda i: (i, 0))
    return pl.pallas_call(
        functools.partial(_out_kernel, sub=sub),
        grid=(n // tm,),
        in_specs=[row(d), row(Q_WIDTH), row(Q_WIDTH),
                  const(wz), const(wg), const(bg), const(wpa), const(wpb), const(wo), const(lng), const(lnb)],
        out_specs=row(d),
        out_shape=jax.ShapeDtypeStruct((n, d), x2.dtype),
        compiler_params=pltpu.CompilerParams(
            dimension_semantics=("parallel",), vmem_limit_bytes=VMEM_LIMIT),
        name="out_block",
    )(x2, ya2, yb2, wz, wg, bg, wpa, wpb, wo, lng, lnb)


def _rope_tables(seq):
    half = HEAD_DIM // 2
    axis_pairs = half // 2
    pos = np.arange(seq)
    freqs = ROPE_THETA ** (-np.arange(axis_pairs, dtype=np.float64) / axis_pairs)
    ang = np.concatenate([(pos // GRID_W)[:, None] * freqs, (pos % GRID_W)[:, None] * freqs], axis=-1)
    cos, sin = np.cos(ang), np.sin(ang)
    cosk = np.tile(cos, (1, KV_WIDTH // half))
    sink = np.tile(np.concatenate([-sin, sin], axis=-1), (1, KV_HEADS))
    return tuple(jnp.asarray(a, F32) for a in (cos.T, sin.T, cosk, sink))


def _layer(x, layer, w_in, b_gate, sink_a, qnorm_b, knorm_b, w_proj_a, w_proj_b, w_out, ln_g, ln_b):
    bsz, seq, d = x.shape
    tm_qkv, sub_qkv, tm_out, sub_out = 2048, MXU_TILE, 1024, MXU_TILE
    tq, kc = 2 * MXU_TILE, MXU_TILE

    wt, wk = _prep_qkv_weights(w_in, layer)
    gq = qnorm_b[layer].astype(F32)[None, :]
    gk = knorm_b[layer].astype(F32)[None, :]
    cosT, sinT, cosk, sink = _rope_tables(seq)

    qaT, ka, vaT, qbT, kb, vbT = _qkv_proj(x, wt, wk, gq, gk, cosT, sinT, cosk, sink, tm_qkv, sub_qkv)
    ya, yb, (wz, wg, wpa, wpb, wo) = _attention(
        sink_a[layer].astype(F32), qaT, ka, vaT, qbT, kb, vbT, w_in, w_proj_a, w_proj_b, w_out, layer, tq, kc)

    out = _out_block(
        x.reshape(bsz * seq, d), ya.reshape(bsz * seq, Q_WIDTH), yb.reshape(bsz * seq, Q_WIDTH),
        wz, wg, b_gate[layer].astype(F32)[None, :], wpa, wpb, wo,
        ln_g[layer].astype(F32)[None, :], ln_b[layer].astype(F32)[None, :], tm_out, sub_out)
    return out.reshape(bsz, seq, d)


def kernel(x, w_in, b_gate, sink_a, qnorm_b, knorm_b, w_proj_a, w_proj_b, w_out, ln_g, ln_b):
    h = x
    for layer in range(DEPTH):
        h = _layer(h, layer, w_in, b_gate, sink_a, qnorm_b, knorm_b, w_proj_a, w_proj_b, w_out, ln_g, ln_b)
    return h
```

```python
import functools

import jax
import jax.numpy as jnp
import numpy as np
from jax import lax
from jax.experimental import pallas as pl
from jax.experimental.pallas import tpu as pltpu

D_MODEL = 1024
HEAD_DIM = 64
HEADS = 8
KV_HEADS = 2
GROUP = HEADS // KV_HEADS
Q_WIDTH = HEADS * HEAD_DIM
KV_WIDTH = KV_HEADS * HEAD_DIM
WINDOW = 128
BLOCK = 128
GRID_W = 64
ROPE_THETA = 10000.0
QK_EPS = 1e-6
LN_EPS = 1e-5
DEPTH = 1
DN_ALPHA = (2.0 * DEPTH) ** 0.25
SCALE = HEAD_DIM ** -0.5
LOG2E = 1.4426950408889634

_SPLITS = (Q_WIDTH, KV_WIDTH, KV_WIDTH, Q_WIDTH, Q_WIDTH, KV_WIDTH, KV_WIDTH, Q_WIDTH, D_MODEL, D_MODEL)
_OFF = tuple(int(v) for v in np.cumsum((0,) + _SPLITS))
(QA, KA, VA, ZA, QB, KB, VB, ZB, GA, GB) = range(10)

MXU_TILE = 256
F32_SUBLANES = 8
BF16_SUBLANES = 16
VMEM_BYTES = 64 * 1024 * 1024
VMEM_LIMIT = VMEM_BYTES * 7 // 8

V_AUG = HEAD_DIM + BF16_SUBLANES

BF16 = jnp.bfloat16
F32 = jnp.float32


def _cols(w, *ids):
    return jnp.concatenate([w[:, _OFF[i]:_OFF[i + 1]] for i in ids], axis=1)


def _prep_kernel(w_ref, wt_out, wk_out):
    w = w_ref[0]
    wt_out[...] = _cols(w, QA, VA, QB, VB).T.astype(BF16)
    wk_out[...] = _cols(w, KA, KB).astype(BF16)


def _prep_qkv_weights(w_in, layer, steps=2):
    d = w_in.shape[1]
    t_width = 2 * (Q_WIDTH + KV_WIDTH)
    return pl.pallas_call(
        _prep_kernel,
        grid=(steps,),
        in_specs=[pl.BlockSpec((1, d // steps, _OFF[ZB]), lambda i: (layer, i, 0))],
        out_specs=[pl.BlockSpec((t_width, d // steps), lambda i: (0, i)),
                   pl.BlockSpec((d // steps, 2 * KV_WIDTH), lambda i: (i, 0))],
        out_shape=[jax.ShapeDtypeStruct((t_width, d), BF16), jax.ShapeDtypeStruct((d, 2 * KV_WIDTH), BF16)],
        compiler_params=pltpu.CompilerParams(
            dimension_semantics=("parallel",), vmem_limit_bytes=VMEM_LIMIT),
        name="prep_qkv_weights",
    )(w_in)


def _qkv_kernel(x_ref, wt_ref, wk_ref, gq_ref, gk_ref, cosT_ref, sinT_ref, cosk_ref, sink_ref,
                qaT_ref, ka_ref, vaT_ref, qbT_ref, kb_ref, vbT_ref, *, sub):
    half = HEAD_DIM // 2
    ones = jnp.ones((V_AUG - HEAD_DIM, sub), BF16)
    eye = (lax.broadcasted_iota(jnp.int32, (HEAD_DIM, HEAD_DIM), 0)
           == lax.broadcasted_iota(jnp.int32, (HEAD_DIM, HEAD_DIM), 1))
    gq = jnp.sum(jnp.where(eye, gq_ref[...], 0.0), axis=1, keepdims=True)[None]
    gk = jnp.concatenate([gk_ref[...]] * KV_HEADS, axis=1)

    for lo in range(0, x_ref.shape[1], sub):
        tok = slice(lo, lo + sub)
        xb = x_ref[0, tok, :].astype(BF16)
        nt_dot = lambda w: lax.dot_general(w, xb, (((1,), (1,)), ((), ())), preferred_element_type=F32)
        hw = Q_WIDTH + KV_WIDTH
        t_b = nt_dot(wt_ref[hw:2 * hw, :])
        kk = jnp.dot(xb, wk_ref[...], preferred_element_type=F32)
        t = jnp.concatenate([nt_dot(wt_ref[0:hw, :]), t_b], axis=0)

        def store_v(vT_ref, vT):
            for g in range(KV_HEADS):
                vT_ref[0, g, 0:HEAD_DIM, tok] = vT[g * HEAD_DIM:(g + 1) * HEAD_DIM].astype(BF16)
                vT_ref[0, g, HEAD_DIM:V_AUG, tok] = ones

        r0 = 0
        qaT_ref[0, :, tok] = (t[r0:r0 + Q_WIDTH] * (SCALE * LOG2E)).astype(BF16)
        r0 += Q_WIDTH
        store_v(vaT_ref, t[r0:r0 + KV_WIDTH])
        r0 += KV_WIDTH
        qb = t[r0:r0 + Q_WIDTH].reshape(HEADS, HEAD_DIM, sub)
        r0 += Q_WIDTH
        store_v(vbT_ref, t[r0:r0 + KV_WIDTH])

        ms = jnp.mean(qb * qb, axis=1, keepdims=True)
        qn = qb * lax.rsqrt(ms + QK_EPS) * gq
        q1, q2 = qn[:, :half], qn[:, half:]
        cosT, sinT = cosT_ref[:, tok][None], sinT_ref[:, tok][None]
        qr = jnp.concatenate([q1 * cosT - q2 * sinT, q2 * cosT + q1 * sinT], axis=1)
        qbT_ref[0, :, tok] = (qr * (SCALE * LOG2E)).reshape(Q_WIDTH, sub).astype(BF16)

        ka_ref[0, tok, :] = kk[:, :KV_WIDTH].astype(BF16)

        kb = kk[:, KV_WIDTH:]
        lane = lax.broadcasted_iota(jnp.int32, kb.shape, 1)
        k2 = kb * kb
        first = lane < HEAD_DIM
        ms0 = jnp.sum(jnp.where(first, k2, 0.0), axis=1, keepdims=True) * (1.0 / HEAD_DIM)
        ms1 = jnp.sum(jnp.where(first, 0.0, k2), axis=1, keepdims=True) * (1.0 / HEAD_DIM)
        kn = kb * lax.rsqrt(jnp.where(first, ms0, ms1) + QK_EPS) * gk
        swapped = jnp.where(lane % HEAD_DIM < half,
                            pltpu.roll(kn, KV_WIDTH - half, axis=1),
                            pltpu.roll(kn, half, axis=1))
        kb_ref[0, tok, :] = (kn * cosk_ref[tok, :] + swapped * sink_ref[tok, :]).astype(BF16)


def _qkv_proj(x, wt, wk, gq, gk, cosT, sinT, cosk, sink, tm, sub):
    bsz, seq, d = x.shape
    nt = seq // tm
    const = lambda shape: pl.BlockSpec(shape, lambda b, i: (0,) * len(shape))
    qT_spec = pl.BlockSpec((1, Q_WIDTH, tm), lambda b, i: (b, 0, i))
    k_spec = pl.BlockSpec((1, tm, KV_WIDTH), lambda b, i: (b, i, 0))
    vT_spec = pl.BlockSpec((1, KV_HEADS, V_AUG, tm), lambda b, i: (b, 0, 0, i))
    qT_shape = jax.ShapeDtypeStruct((bsz, Q_WIDTH, seq), BF16)
    k_shape = jax.ShapeDtypeStruct((bsz, seq, KV_WIDTH), BF16)
    vT_shape = jax.ShapeDtypeStruct((bsz, KV_HEADS, V_AUG, seq), BF16)
    return pl.pallas_call(
        functools.partial(_qkv_kernel, sub=sub),
        grid=(bsz, nt),
        in_specs=[
            pl.BlockSpec((1, tm, d), lambda b, i: (b, i, 0)),
            const(wt.shape), const(wk.shape), const(gq.shape), const(gk.shape),
            pl.BlockSpec((HEAD_DIM // 2, tm), lambda b, i: (0, i)),
            pl.BlockSpec((HEAD_DIM // 2, tm), lambda b, i: (0, i)),
            pl.BlockSpec((tm, KV_WIDTH), lambda b, i: (i, 0)),
            pl.BlockSpec((tm, KV_WIDTH), lambda b, i: (i, 0)),
        ],
        out_specs=[qT_spec, k_spec, vT_spec, qT_spec, k_spec, vT_spec],
        out_shape=[qT_shape, k_shape, vT_shape, qT_shape, k_shape, vT_shape],
        compiler_params=pltpu.CompilerParams(
            dimension_semantics=("parallel", "parallel"), vmem_limit_bytes=VMEM_LIMIT),
        name="qkv_proj",
    )(x, wt, wk, gq, gk, cosT, sinT, cosk, sink)


def _padded_q(qT_h, g):
    zeros = jnp.zeros_like(qT_h)
    parts = [zeros] * KV_HEADS
    parts[g] = qT_h
    return jnp.concatenate(parts, axis=0)


def _aligned(v, m):
    return v if isinstance(v, int) else pl.multiple_of(v, m)


def _fold_max(s):
    rows, n = s.shape
    while rows % (2 * F32_SUBLANES) == 0:
        rows //= 2
        s = jnp.maximum(s[:rows], s[rows:])
    return s if rows == F32_SUBLANES else jnp.max(s.reshape(rows // F32_SUBLANES, F32_SUBLANES, n), axis=0)


def _normalise(oT, extra=None):
    denom = oT[HEAD_DIM:HEAD_DIM + 1]
    if extra is not None:
        denom = denom + extra
    return oT[:HEAD_DIM] * (1.0 / denom)


def _alibi_table():
    r = np.arange(5 * BLOCK)[:, None]
    c = np.arange(BLOCK)[None, :]
    dist = np.abs(r - 2 * BLOCK - c).astype(np.float64)
    slopes = 2.0 ** (-8.0 * (np.arange(HEADS) + 1.0) / HEADS) * LOG2E
    bias = np.where(dist <= WINDOW, -slopes[:, None, None] * dist, -np.inf)
    return jnp.asarray(bias, F32)


def _attn_kernel(sink_ref, bias_ref, qaT_ref, ka_ref, vaT_ref, qbT_ref, kb_ref, vbT_ref,
                 w_ref, wpa_ref, wpb_ref, wo_ref,
                 ya_ref, yb_ref, wz_out, wg_out, wpa_out, wpb_out, wo_out, sa_ref, sb_ref, *, tq, kc):
    seq = kb_ref.shape[1]
    gw = GROUP * HEAD_DIM
    n_tiles = seq // tq
    chunks = range(0, seq, kc)
    nblk = seq // BLOCK
    span = 3 * BLOCK
    gn = GROUP * BLOCK
    a_steps = HEADS // KV_HEADS
    assert nblk == a_steps * n_tiles

    def sink_row(g):
        return jnp.concatenate(
            [jnp.full((1, BLOCK), sink_ref[g * GROUP + j] * LOG2E, F32) for j in range(GROUP)], axis=1)

    def window(n):
        if isinstance(n, int):
            q_lo, ws = n * BLOCK, min(max((n - 1) * BLOCK, 0), seq - span)
            return q_lo, ws, 2 * BLOCK - (q_lo - ws)
        q_lo = pl.multiple_of(n * BLOCK, BLOCK)
        ws = pl.multiple_of(jnp.clip((n - 1) * BLOCK, 0, seq - span), BLOCK)
        return q_lo, ws, pl.multiple_of(2 * BLOCK - (q_lo - ws), BLOCK)

    def a_scores(n, slot, g):
        q_lo, ws, b_lo = window(n)
        qT = qaT_ref[0, g * gw:(g + 1) * gw, pl.ds(q_lo, BLOCK)]
        w = jnp.concatenate(
            [_padded_q(qT[j * HEAD_DIM:(j + 1) * HEAD_DIM], g) for j in range(GROUP)], axis=1)
        sT = jnp.dot(ka_ref[0, pl.ds(ws, span), :], w, preferred_element_type=F32)
        ms = []
        for j in range(GROUP):
            s = sT[:, j * BLOCK:(j + 1) * BLOCK] + bias_ref[g * GROUP + j, pl.ds(b_lo, span), :]
            sa_ref[slot, :, g * gn + j * BLOCK:g * gn + (j + 1) * BLOCK] = s
            ms.append(jnp.max(_fold_max(s), axis=0, keepdims=True))
        return jnp.maximum(jnp.concatenate(ms, axis=1), sink_row(g))

    def a_values(n, slot, g, m):
        _, ws, _ = window(n)
        pT = jnp.exp2(sa_ref[slot, :, g * gn:(g + 1) * gn] - m).astype(BF16)
        oT = jnp.dot(vaT_ref[0, g, :, pl.ds(ws, span)], pT, preferred_element_type=F32)
        o = _normalise(oT, jnp.exp2(sink_row(g) - m))
        return [o[:, j * BLOCK:(j + 1) * BLOCK] for j in range(GROUP)]

    def a_item(n, slot, g, m_prev, outs, do_scores=True):
        m = a_scores(n, slot, g) if do_scores else None
        outs = outs + a_values(n - 1, 1 - slot, g, m_prev)
        if g == KV_HEADS - 1:
            q_lo, _, _ = window(n - 1)
            ya_ref[0, pl.ds(q_lo, BLOCK), :] = jnp.concatenate(outs, axis=0).T.astype(ya_ref.dtype)
            outs = []
        return m, outs

    def padded_q(t, h):
        qT_h = qbT_ref[0, h * HEAD_DIM:(h + 1) * HEAD_DIM, pl.ds(_aligned(t * tq, tq), tq)]
        return _padded_q(qT_h, h // GROUP)

    def scores_chunk(slot, w, k_lo, m8):
        sT = jnp.dot(kb_ref[0, k_lo:k_lo + kc, :], w, preferred_element_type=F32)
        sb_ref[slot, k_lo:k_lo + kc, :] = sT
        m8_c = _fold_max(sT)
        return m8_c if m8 is None else jnp.maximum(m8, m8_c)

    def values_chunk(g, slot, m, k_lo, acc):
        pT = jnp.exp2(sb_ref[slot, k_lo:k_lo + kc, :] - m).astype(BF16)
        oT = jnp.dot(vbT_ref[0, g, :, k_lo:k_lo + kc], pT, preferred_element_type=F32)
        return oT if acc is None else acc + oT

    def cast_weight_rows(t):
        def rows(ref):
            n = ref.shape[-2] // n_tiles
            return pl.ds(_aligned(t * n, n), n)

        w = w_ref[0, rows(w_ref), :]
        wz_out[rows(wz_out), :] = _cols(w, ZA, ZB).astype(BF16)
        wg_out[rows(wg_out), :] = _cols(w, GA, GB).astype(BF16)
        wpa_out[rows(wpa_out), :] = wpa_ref[0, rows(wpa_ref), :].astype(BF16)
        wpb_out[rows(wpb_out), :] = wpb_ref[0, rows(wpb_ref), :].astype(BF16)
        wo_out[rows(wo_out), :] = wo_ref[0, rows(wo_ref), :].astype(BF16)

    def fill():
        w, m8 = padded_q(0, 0), None
        for k_lo in chunks:
            m8 = scores_chunk(0, w, k_lo, m8)
        return (jnp.max(m8, axis=0, keepdims=True),) + tuple(a_scores(0, 0, g) for g in range(KV_HEADS))

    def tile(t, carry, last=False):
        m, ma = carry[0], list(carry[1:])
        cast_weight_rows(t)
        outs, a_outs, ma_next = [], [], [None] * KV_HEADS
        for h in range(HEADS):
            nxt = (t, h + 1) if h + 1 < HEADS else (None if last else (t + 1, 0))
            w = padded_q(*nxt) if nxt else None
            m8 = acc = None
            for k_lo in chunks:
                if nxt:
                    m8 = scores_chunk((h + 1) % 2, w, k_lo, m8)
                acc = values_chunk(h // GROUP, h % 2, m, k_lo, acc)
            outs.append(_normalise(acc))
            m = jnp.max(m8, axis=0, keepdims=True) if nxt else None
            if h % GROUP == GROUP - 1:
                g = h // GROUP
                yb_ref[0, pl.ds(_aligned(t * tq, tq), tq), g * gw:(g + 1) * gw] = (
                    jnp.concatenate(outs, axis=0).T.astype(yb_ref.dtype))
                outs = []

            i, ga = h // KV_HEADS, h % KV_HEADS
            n = a_steps * t + i + 1
            a_has_scores = not (last and i == a_steps - 1)
            ma_next[ga], a_outs = a_item(n, (i + 1) % 2, ga, ma[ga], a_outs, a_has_scores)
            if ga == KV_HEADS - 1:
                ma, ma_next = ma_next, [None] * KV_HEADS
        return None if last else (m,) + tuple(ma)

    carry = fill()
    carry = lax.fori_loop(0, n_tiles - 1, tile, carry)
    tile(n_tiles - 1, carry, last=True)


def _attention(sink, qaT, ka, vaT, qbT, kb, vbT, w_in, w_proj_a, w_proj_b, w_out, layer, tq, kc):
    bsz, _, seq = qbT.shape
    d = w_in.shape[1]
    bias = _alibi_table()
    qT_spec = pl.BlockSpec((1, Q_WIDTH, seq), lambda b: (b, 0, 0))
    k_spec = pl.BlockSpec((1, seq, KV_WIDTH), lambda b: (b, 0, 0))
    vT_spec = pl.BlockSpec((1, KV_HEADS, V_AUG, seq), lambda b: (b, 0, 0, 0))
    y_spec = pl.BlockSpec((1, seq, Q_WIDTH), lambda b: (b, 0, 0))
    y_shape = jax.ShapeDtypeStruct((bsz, seq, Q_WIDTH), BF16)
    slab_in = lambda a: pl.BlockSpec((1, a.shape[1] // bsz, a.shape[2]), lambda b: (layer, b, 0))
    slab_out = lambda rows, width: pl.BlockSpec((rows // bsz, width), lambda b: (b, 0))
    w_shapes = [(d, 2 * Q_WIDTH), (d, 2 * D_MODEL), w_proj_a.shape[1:], w_proj_b.shape[1:], w_out.shape[1:]]
    outs = pl.pallas_call(
        functools.partial(_attn_kernel, tq=tq, kc=kc),
        grid=(bsz,),
        in_specs=[
            pl.BlockSpec(memory_space=pltpu.SMEM),
            pl.BlockSpec(bias.shape, lambda b: (0, 0, 0)),
            qT_spec, k_spec, vT_spec, qT_spec, k_spec, vT_spec,
            slab_in(w_in), slab_in(w_proj_a), slab_in(w_proj_b), slab_in(w_out),
        ],
        out_specs=[y_spec, y_spec] + [slab_out(*s) for s in w_shapes],
        out_shape=[y_shape, y_shape] + [jax.ShapeDtypeStruct(s, BF16) for s in w_shapes],
        scratch_shapes=[pltpu.VMEM((2, 3 * BLOCK, HEADS * BLOCK), F32), pltpu.VMEM((2, seq, tq), F32)],
        compiler_params=pltpu.CompilerParams(
            dimension_semantics=("parallel",), vmem_limit_bytes=VMEM_LIMIT),
        name="attention",
    )(sink, bias, qaT, ka, vaT, qbT, kb, vbT, w_in, w_proj_a, w_proj_b, w_out)
    return outs[0], outs[1], outs[2:]


def _out_kernel(x_ref, ya_ref, yb_ref, wz_ref, wg_ref, bg_ref, wpa_ref, wpb_ref, wo_ref, lng_ref, lnb_ref,
                o_ref, *, sub):
    def input_proj(rows):
        xb = x_ref[rows, :].astype(BF16)
        z = jnp.dot(xb, wz_ref[...], preferred_element_type=F32)
        gl = jnp.concatenate([jnp.dot(xb, wg_ref[:, :D_MODEL], preferred_element_type=F32),
                              jnp.dot(xb, wg_ref[:, D_MODEL:], preferred_element_type=F32)], axis=1)
        return z, gl

    def branch_proj(rows, z):
        sz = z * jax.nn.sigmoid(z)
        ua = (ya_ref[rows, :].astype(F32) * sz[:, :Q_WIDTH]).astype(BF16)
        ub = (yb_ref[rows, :].astype(F32) * sz[:, Q_WIDTH:]).astype(BF16)
        pa = jnp.dot(ua, wpa_ref[...], preferred_element_type=F32)
        pb = jnp.dot(ub, wpb_ref[...], preferred_element_type=F32)
        return pa, pb

    def merge_norm(rows, gl, pa, pb):
        gate = jax.nn.sigmoid(gl + bg_ref[...])
        mix = (gate[:, :D_MODEL] * pa + gate[:, D_MODEL:] * pb).astype(BF16)
        y = jnp.dot(mix, wo_ref[...], preferred_element_type=F32)
        r = DN_ALPHA * x_ref[rows, :] + y
        mu = jnp.mean(r, axis=-1, keepdims=True)
        rc = r - mu
        var = jnp.mean(rc * rc, axis=-1, keepdims=True)
        o_ref[rows, :] = rc * lax.rsqrt(var + LN_EPS) * lng_ref[...] + lnb_ref[...]

    tiles = [slice(lo, lo + sub) for lo in range(0, x_ref.shape[0], sub)]
    stage1, stage2 = {}, {}
    for step in range(len(tiles) + 2):
        if step < len(tiles):
            stage1[step] = input_proj(tiles[step])
        if 0 <= step - 1 < len(tiles):
            stage2[step - 1] = branch_proj(tiles[step - 1], stage1[step - 1][0])
        if 0 <= step - 2 < len(tiles):
            merge_norm(tiles[step - 2], stage1.pop(step - 2)[1], *stage2.pop(step - 2))


def _out_block(x2, ya2, yb2, wz, wg, bg, wpa, wpb, wo, lng, lnb, tm, sub):
    n, d = x2.shape
    const = lambda a: pl.BlockSpec(a.shape, lambda i: (0, 0), pipeline_mode=pl.Buffered(1))
    row = lambda w: pl.BlockSpec((tm, w), lambda i: (i, 0))
    return pl.pallas_call(
        functools.partial(_out_kernel, sub=sub),
        grid=(n // tm,),
        in_specs=[row(d), row(Q_WIDTH), row(Q_WIDTH),
                  const(wz), const(wg), const(bg), const(wpa), const(wpb), const(wo), const(lng), const(lnb)],
        out_specs=row(d),
        out_shape=jax.ShapeDtypeStruct((n, d), x2.dtype),
        compiler_params=pltpu.CompilerParams(
            dimension_semantics=("parallel",), vmem_limit_bytes=VMEM_LIMIT),
        name="out_block",
    )(x2, ya2, yb2, wz, wg, bg, wpa, wpb, wo, lng, lnb)


def _rope_tables(seq):
    half = HEAD_DIM // 2
    axis_pairs = half // 2
    pos = np.arange(seq)
    freqs = ROPE_THETA ** (-np.arange(axis_pairs, dtype=np.float64) / axis_pairs)
    ang = np.concatenate([(pos // GRID_W)[:, None] * freqs, (pos % GRID_W)[:, None] * freqs], axis=-1)
    cos, sin = np.cos(ang), np.sin(ang)
    cosk = np.tile(cos, (1, KV_WIDTH // half))
    sink = np.tile(np.concatenate([-sin, sin], axis=-1), (1, KV_HEADS))
    return tuple(jnp.asarray(a, F32) for a in (cos.T, sin.T, cosk, sink))


def _layer(x, layer, w_in, b_gate, sink_a, qnorm_b, knorm_b, w_proj_a, w_proj_b, w_out, ln_g, ln_b):
    bsz, seq, d = x.shape
    tm_qkv, sub_qkv, tm_out, sub_out = 2048, MXU_TILE, 1024, MXU_TILE
    tq, kc = 2 * MXU_TILE, MXU_TILE

    wt, wk = _prep_qkv_weights(w_in, layer)
    gq = qnorm_b[layer].astype(F32)[None, :]
    gk = knorm_b[layer].astype(F32)[None, :]
    cosT, sinT, cosk, sink = _rope_tables(seq)

    qaT, ka, vaT, qbT, kb, vbT = _qkv_proj(x, wt, wk, gq, gk, cosT, sinT, cosk, sink, tm_qkv, sub_qkv)
    ya, yb, (wz, wg, wpa, wpb, wo) = _attention(
        sink_a[layer].astype(F32), qaT, ka, vaT, qbT, kb, vbT, w_in, w_proj_a, w_proj_b, w_out, layer, tq, kc)

    out = _out_block(
        x.reshape(bsz * seq, d), ya.reshape(bsz * seq, Q_WIDTH), yb.reshape(bsz * seq, Q_WIDTH),
        wz, wg, b_gate[layer].astype(F32)[None, :], wpa, wpb, wo,
        ln_g[layer].astype(F32)[None, :], ln_b[layer].astype(F32)[None, :], tm_out, sub_out)
    return out.reshape(bsz, seq, d)


def kernel(x, w_in, b_gate, sink_a, qnorm_b, knorm_b, w_proj_a, w_proj_b, w_out, ln_g, ln_b):
    h = x
    for layer in range(DEPTH):
        h = _layer(h, layer, w_in, b_gate, sink_a, qnorm_b, knorm_b, w_proj_a, w_proj_b, w_out, ln_g, ln_b)
    return h
```

```python
import functools

import jax
import jax.numpy as jnp
import numpy as np
from jax import lax
from jax.experimental import pallas as pl
from jax.experimental.pallas import tpu as pltpu

D_MODEL = 1024
HEAD_DIM = 64
HEADS = 8
KV_HEADS = 2
GROUP = HEADS // KV_HEADS
Q_WIDTH = HEADS * HEAD_DIM
KV_WIDTH = KV_HEADS * HEAD_DIM
WINDOW = 128
BLOCK = 128
GRID_W = 64
ROPE_THETA = 10000.0
QK_EPS = 1e-6
LN_EPS = 1e-5
DEPTH = 1
DN_ALPHA = (2.0 * DEPTH) ** 0.25
SCALE = HEAD_DIM ** -0.5
LOG2E = 1.4426950408889634

_SPLITS = (Q_WIDTH, KV_WIDTH, KV_WIDTH, Q_WIDTH, Q_WIDTH, KV_WIDTH, KV_WIDTH, Q_WIDTH, D_MODEL, D_MODEL)
_OFF = tuple(int(v) for v in np.cumsum((0,) + _SPLITS))
(QA, KA, VA, ZA, QB, KB, VB, ZB, GA, GB) = range(10)

MXU_TILE = 256
F32_SUBLANES = 8
BF16_SUBLANES = 16
VMEM_BYTES = 64 * 1024 * 1024
VMEM_LIMIT = VMEM_BYTES * 7 // 8

V_AUG = HEAD_DIM + BF16_SUBLANES

BF16 = jnp.bfloat16
F32 = jnp.float32


def _cols(w, *ids):
    return jnp.concatenate([w[:, _OFF[i]:_OFF[i + 1]] for i in ids], axis=1)


def _prep_kernel(w_ref, wt_out, wk_out):
    w = w_ref[0]
    wt_out[...] = _cols(w, QA, VA, QB, VB).T.astype(BF16)
    wk_out[...] = _cols(w, KA, KB).astype(BF16)


def _prep_qkv_weights(w_in, layer, steps=2):
    d = w_in.shape[1]
    t_width = 2 * (Q_WIDTH + KV_WIDTH)
    return pl.pallas_call(
        _prep_kernel,
        grid=(steps,),
        in_specs=[pl.BlockSpec((1, d // steps, _OFF[ZB]), lambda i: (layer, i, 0))],
        out_specs=[pl.BlockSpec((t_width, d // steps), lambda i: (0, i)),
                   pl.BlockSpec((d // steps, 2 * KV_WIDTH), lambda i: (i, 0))],
        out_shape=[jax.ShapeDtypeStruct((t_width, d), BF16), jax.ShapeDtypeStruct((d, 2 * KV_WIDTH), BF16)],
        compiler_params=pltpu.CompilerParams(
            dimension_semantics=("parallel",), vmem_limit_bytes=VMEM_LIMIT),
        name="prep_qkv_weights",
    )(w_in)


def _qkv_kernel(x_ref, wt_ref, wk_ref, gq_ref, gk_ref, cosT_ref, sinT_ref, cosk_ref, sink_ref,
                qaT_ref, ka_ref, vaT_ref, qbT_ref, kb_ref, vbT_ref, *, sub):
    half = HEAD_DIM // 2
    ones = jnp.ones((V_AUG - HEAD_DIM, sub), BF16)
    eye = (lax.broadcasted_iota(jnp.int32, (HEAD_DIM, HEAD_DIM), 0)
           == lax.broadcasted_iota(jnp.int32, (HEAD_DIM, HEAD_DIM), 1))
    gq = jnp.sum(jnp.where(eye, gq_ref[...], 0.0), axis=1, keepdims=True)[None]
    gk = jnp.concatenate([gk_ref[...]] * KV_HEADS, axis=1)

    for lo in range(0, x_ref.shape[1], sub):
        tok = slice(lo, lo + sub)
        xb = x_ref[0, tok, :].astype(BF16)
        nt_dot = lambda w: lax.dot_general(w, xb, (((1,), (1,)), ((), ())), preferred_element_type=F32)
        hw = Q_WIDTH + KV_WIDTH
        t_b = nt_dot(wt_ref[hw:2 * hw, :])
        kk = jnp.dot(xb, wk_ref[...], preferred_element_type=F32)
        t = jnp.concatenate([nt_dot(wt_ref[0:hw, :]), t_b], axis=0)

        def store_v(vT_ref, vT):
            for g in range(KV_HEADS):
                vT_ref[0, g, 0:HEAD_DIM, tok] = vT[g * HEAD_DIM:(g + 1) * HEAD_DIM].astype(BF16)
                vT_ref[0, g, HEAD_DIM:V_AUG, tok] = ones

        r0 = 0
        qaT_ref[0, :, tok] = (t[r0:r0 + Q_WIDTH] * (SCALE * LOG2E)).astype(BF16)
        r0 += Q_WIDTH
        store_v(vaT_ref, t[r0:r0 + KV_WIDTH])
        r0 += KV_WIDTH
        qb = t[r0:r0 + Q_WIDTH].reshape(HEADS, HEAD_DIM, sub)
        r0 += Q_WIDTH
        store_v(vbT_ref, t[r0:r0 + KV_WIDTH])

        ms = jnp.mean(qb * qb, axis=1, keepdims=True)
        qn = qb * lax.rsqrt(ms + QK_EPS) * gq
        q1, q2 = qn[:, :half], qn[:, half:]
        cosT, sinT = cosT_ref[:, tok][None], sinT_ref[:, tok][None]
        qr = jnp.concatenate([q1 * cosT - q2 * sinT, q2 * cosT + q1 * sinT], axis=1)
        qbT_ref[0, :, tok] = (qr * (SCALE * LOG2E)).reshape(Q_WIDTH, sub).astype(BF16)

        ka_ref[0, tok, :] = kk[:, :KV_WIDTH].astype(BF16)

        kb = kk[:, KV_WIDTH:]
        lane = lax.broadcasted_iota(jnp.int32, kb.shape, 1)
        k2 = kb * kb
        first = lane < HEAD_DIM
        ms0 = jnp.sum(jnp.where(first, k2, 0.0), axis=1, keepdims=True) * (1.0 / HEAD_DIM)
        ms1 = jnp.sum(jnp.where(first, 0.0, k2), axis=1, keepdims=True) * (1.0 / HEAD_DIM)
        kn = kb * lax.rsqrt(jnp.where(first, ms0, ms1) + QK_EPS) * gk
        swapped = jnp.where(lane % HEAD_DIM < half,
                            pltpu.roll(kn, KV_WIDTH - half, axis=1),
                            pltpu.roll(kn, half, axis=1))
        kb_ref[0, tok, :] = (kn * cosk_ref[tok, :] + swapped * sink_ref[tok, :]).astype(BF16)


def _qkv_proj(x, wt, wk, gq, gk, cosT, sinT, cosk, sink, tm, sub):
    bsz, seq, d = x.shape
    nt = seq // tm
    const = lambda shape: pl.BlockSpec(shape, lambda b, i: (0,) * len(shape))
    qT_spec = pl.BlockSpec((1, Q_WIDTH, tm), lambda b, i: (b, 0, i))
    k_spec = pl.BlockSpec((1, tm, KV_WIDTH), lambda b, i: (b, i, 0))
    vT_spec = pl.BlockSpec((1, KV_HEADS, V_AUG, tm), lambda b, i: (b, 0, 0, i))
    qT_shape = jax.ShapeDtypeStruct((bsz, Q_WIDTH, seq), BF16)
    k_shape = jax.ShapeDtypeStruct((bsz, seq, KV_WIDTH), BF16)
    vT_shape = jax.ShapeDtypeStruct((bsz, KV_HEADS, V_AUG, seq), BF16)
    return pl.pallas_call(
        functools.partial(_qkv_kernel, sub=sub),
        grid=(bsz, nt),
        in_specs=[
            pl.BlockSpec((1, tm, d), lambda b, i: (b, i, 0)),
            const(wt.shape), const(wk.shape), const(gq.shape), const(gk.shape),
            pl.BlockSpec((HEAD_DIM // 2, tm), lambda b, i: (0, i)),
            pl.BlockSpec((HEAD_DIM // 2, tm), lambda b, i: (0, i)),
            pl.BlockSpec((tm, KV_WIDTH), lambda b, i: (i, 0)),
            pl.BlockSpec((tm, KV_WIDTH), lambda b, i: (i, 0)),
        ],
        out_specs=[qT_spec, k_spec, vT_spec, qT_spec, k_spec, vT_spec],
        out_shape=[qT_shape, k_shape, vT_shape, qT_shape, k_shape, vT_shape],
        compiler_params=pltpu.CompilerParams(
            dimension_semantics=("parallel", "parallel"), vmem_limit_bytes=VMEM_LIMIT),
        name="qkv_proj",
    )(x, wt, wk, gq, gk, cosT, sinT, cosk, sink)


def _padded_q(qT_h, g):
    zeros = jnp.zeros_like(qT_h)
    parts = [zeros] * KV_HEADS
    parts[g] = qT_h
    return jnp.concatenate(parts, axis=0)


def _aligned(v, m):
    return v if isinstance(v, int) else pl.multiple_of(v, m)


def _fold_max(s):
    rows, n = s.shape
    while rows % (2 * F32_SUBLANES) == 0:
        rows //= 2
        s = jnp.maximum(s[:rows], s[rows:])
    return s if rows == F32_SUBLANES else jnp.max(s.reshape(rows // F32_SUBLANES, F32_SUBLANES, n), axis=0)


def _normalise(oT, extra=None):
    denom = oT[HEAD_DIM:HEAD_DIM + 1]
    if extra is not None:
        denom = denom + extra
    return oT[:HEAD_DIM] * (1.0 / denom)


def _alibi_table():
    r = np.arange(5 * BLOCK)[:, None]
    c = np.arange(BLOCK)[None, :]
    dist = np.abs(r - 2 * BLOCK - c).astype(np.float64)
    slopes = 2.0 ** (-8.0 * (np.arange(HEADS) + 1.0) / HEADS) * LOG2E
    bias = np.where(dist <= WINDOW, -slopes[:, None, None] * dist, -np.inf)
    return jnp.asarray(bias, F32)


def _attn_kernel(sink_ref, bias_ref, qaT_ref, ka_ref, vaT_ref, qbT_ref, kb_ref, vbT_ref,
                 w_ref, wpa_ref, wpb_ref, wo_ref,
                 ya_ref, yb_ref, wz_out, wg_out, wpa_out, wpb_out, wo_out, sa_ref, sb_ref, *, tq, kc):
    seq = kb_ref.shape[1]
    gw = GROUP * HEAD_DIM
    n_tiles = seq // tq
    chunks = range(0, seq, kc)
    nblk = seq // BLOCK
    span = 3 * BLOCK
    gn = GROUP * BLOCK
    a_steps = HEADS // KV_HEADS
    assert nblk == a_steps * n_tiles

    def sink_row(g):
        return jnp.concatenate(
            [jnp.full((1, BLOCK), sink_ref[g * GROUP + j] * LOG2E, F32) for j in range(GROUP)], axis=1)

    def window(n):
        if isinstance(n, int):
            q_lo, ws = n * BLOCK, min(max((n - 1) * BLOCK, 0), seq - span)
            return q_lo, ws, 2 * BLOCK - (q_lo - ws)
        q_lo = pl.multiple_of(n * BLOCK, BLOCK)
        ws = pl.multiple_of(jnp.clip((n - 1) * BLOCK, 0, seq - span), BLOCK)
        return q_lo, ws, pl.multiple_of(2 * BLOCK - (q_lo - ws), BLOCK)

    def a_scores(n, slot, g):
        q_lo, ws, b_lo = window(n)
        qT = qaT_ref[0, g * gw:(g + 1) * gw, pl.ds(q_lo, BLOCK)]
        w = jnp.concatenate(
            [_padded_q(qT[j * HEAD_DIM:(j + 1) * HEAD_DIM], g) for j in range(GROUP)], axis=1)
        sT = jnp.dot(ka_ref[0, pl.ds(ws, span), :], w, preferred_element_type=F32)
        ms = []
        for j in range(GROUP):
            s = sT[:, j * BLOCK:(j + 1) * BLOCK] + bias_ref[g * GROUP + j, pl.ds(b_lo, span), :]
            sa_ref[slot, :, g * gn + j * BLOCK:g * gn + (j + 1) * BLOCK] = s
            ms.append(jnp.max(_fold_max(s), axis=0, keepdims=True))
        return jnp.maximum(jnp.concatenate(ms, axis=1), sink_row(g))

    def a_values(n, slot, g, m):
        _, ws, _ = window(n)
        pT = jnp.exp2(sa_ref[slot, :, g * gn:(g + 1) * gn] - m).astype(BF16)
        oT = jnp.dot(vaT_ref[0, g, :, pl.ds(ws, span)], pT, preferred_element_type=F32)
        o = _normalise(oT, jnp.exp2(sink_row(g) - m))
        return [o[:, j * BLOCK:(j + 1) * BLOCK] for j in range(GROUP)]

    def a_item(n, slot, g, m_prev, outs, do_scores=True):
        m = a_scores(n, slot, g) if do_scores else None
        outs = outs + a_values(n - 1, 1 - slot, g, m_prev)
        if g == KV_HEADS - 1:
            q_lo, _, _ = window(n - 1)
            ya_ref[0, pl.ds(q_lo, BLOCK), :] = jnp.concatenate(outs, axis=0).T.astype(ya_ref.dtype)
            outs = []
        return m, outs

    def padded_q(t, h):
        qT_h = qbT_ref[0, h * HEAD_DIM:(h + 1) * HEAD_DIM, pl.ds(_aligned(t * tq, tq), tq)]
        return _padded_q(qT_h, h // GROUP)

    def scores_chunk(slot, w, k_lo, m8):
        sT = jnp.dot(kb_ref[0, k_lo:k_lo + kc, :], w, preferred_element_type=F32)
        sb_ref[slot, k_lo:k_lo + kc, :] = sT
        m8_c = _fold_max(sT)
        return m8_c if m8 is None else jnp.maximum(m8, m8_c)

    def values_chunk(g, slot, m, k_lo, acc):
        pT = jnp.exp2(sb_ref[slot, k_lo:k_lo + kc, :] - m).astype(BF16)
        oT = jnp.dot(vbT_ref[0, g, :, k_lo:k_lo + kc], pT, preferred_element_type=F32)
        return oT if acc is None else acc + oT

    def cast_weight_rows(t):
        def rows(ref):
            n = ref.shape[-2] // n_tiles
            return pl.ds(_aligned(t * n, n), n)

        w = w_ref[0, rows(w_ref), :]
        wz_out[rows(wz_out), :] = _cols(w, ZA, ZB).astype(BF16)
        wg_out[rows(wg_out), :] = _cols(w, GA, GB).astype(BF16)
        wpa_out[rows(wpa_out), :] = wpa_ref[0, rows(wpa_ref), :].astype(BF16)
        wpb_out[rows(wpb_out), :] = wpb_ref[0, rows(wpb_ref), :].astype(BF16)
        wo_out[rows(wo_out), :] = wo_ref[0, rows(wo_ref), :].astype(BF16)

    def fill():
        w, m8 = padded_q(0, 0), None
        for k_lo in chunks:
            m8 = scores_chunk(0, w, k_lo, m8)
        return (jnp.max(m8, axis=0, keepdims=True),) + tuple(a_scores(0, 0, g) for g in range(KV_HEADS))

    def tile(t, carry, last=False):
        m, ma = carry[0], list(carry[1:])
        cast_weight_rows(t)
        outs, a_outs, ma_next = [], [], [None] * KV_HEADS
        for h in range(HEADS):
            nxt = (t, h + 1) if h + 1 < HEADS else (None if last else (t + 1, 0))
            w = padded_q(*nxt) if nxt else None
            m8 = acc = None
            for k_lo in chunks:
                if nxt:
                    m8 = scores_chunk((h + 1) % 2, w, k_lo, m8)
                acc = values_chunk(h // GROUP, h % 2, m, k_lo, acc)
            outs.append(_normalise(acc))
            m = jnp.max(m8, axis=0, keepdims=True) if nxt else None
            if h % GROUP == GROUP - 1:
                g = h // GROUP
                yb_ref[0, pl.ds(_aligned(t * tq, tq), tq), g * gw:(g + 1) * gw] = (
                    jnp.concatenate(outs, axis=0).T.astype(yb_ref.dtype))
                outs = []

            i, ga = h // KV_HEADS, h % KV_HEADS
            n = a_steps * t + i + 1
            a_has_scores = not (last and i == a_steps - 1)
            ma_next[ga], a_outs = a_item(n, (i + 1) % 2, ga, ma[ga], a_outs, a_has_scores)
            if ga == KV_HEADS - 1:
                ma, ma_next = ma_next, [None] * KV_HEADS
        return None if last else (m,) + tuple(ma)

    carry = fill()
    carry = lax.fori_loop(0, n_tiles - 1, tile, carry)
    tile(n_tiles - 1, carry, last=True)


def _attention(sink, qaT, ka, vaT, qbT, kb, vbT, w_in, w_proj_a, w_proj_b, w_out, layer, tq, kc):
    bsz, _, seq = qbT.shape
    d = w_in.shape[1]
    bias = _alibi_table()
    qT_spec = pl.BlockSpec((1, Q_WIDTH, seq), lambda b: (b, 0, 0))
    k_spec = pl.BlockSpec((1, seq, KV_WIDTH), lambda b: (b, 0, 0))
    vT_spec = pl.BlockSpec((1, KV_HEADS, V_AUG, seq), lambda b: (b, 0, 0, 0))
    y_spec = pl.BlockSpec((1, seq, Q_WIDTH), lambda b: (b, 0, 0))
    y_shape = jax.ShapeDtypeStruct((bsz, seq, Q_WIDTH), BF16)
    slab_in = lambda a: pl.BlockSpec((1, a.shape[1] // bsz, a.shape[2]), lambda b: (layer, b, 0))
    slab_out = lambda rows, width: pl.BlockSpec((rows // bsz, width), lambda b: (b, 0))
    w_shapes = [(d, 2 * Q_WIDTH), (d, 2 * D_MODEL), w_proj_a.shape[1:], w_proj_b.shape[1:], w_out.shape[1:]]
    outs = pl.pallas_call(
        functools.partial(_attn_kernel, tq=tq, kc=kc),
        grid=(bsz,),
        in_specs=[
            pl.BlockSpec(memory_space=pltpu.SMEM),
            pl.BlockSpec(bias.shape, lambda b: (0, 0, 0)),
            qT_spec, k_spec, vT_spec, qT_spec, k_spec, vT_spec,
            slab_in(w_in), slab_in(w_proj_a), slab_in(w_proj_b), slab_in(w_out),
        ],
        out_specs=[y_spec, y_spec] + [slab_out(*s) for s in w_shapes],
        out_shape=[y_shape, y_shape] + [jax.ShapeDtypeStruct(s, BF16) for s in w_shapes],
        scratch_shapes=[pltpu.VMEM((2, 3 * BLOCK, HEADS * BLOCK), F32), pltpu.VMEM((2, seq, tq), F32)],
        compiler_params=pltpu.CompilerParams(
            dimension_semantics=("parallel",), vmem_limit_bytes=VMEM_LIMIT),
        name="attention",
    )(sink, bias, qaT, ka, vaT, qbT, kb, vbT, w_in, w_proj_a, w_proj_b, w_out)
    return outs[0], outs[1], outs[2:]


def _out_kernel(x_ref, ya_ref, yb_ref, wz_ref, wg_ref, bg_ref, wpa_ref, wpb_ref, wo_ref, lng_ref, lnb_ref,
                o_ref, *, sub):
    def input_proj(rows):
        xb = x_ref[rows, :].astype(BF16)
        z = jnp.dot(xb, wz_ref[...], preferred_element_type=F32)
        gl = jnp.concatenate([jnp.dot(xb, wg_ref[:, :D_MODEL], preferred_element_type=F32),
                              jnp.dot(xb, wg_ref[:, D_MODEL:], preferred_element_type=F32)], axis=1)
        return z, gl

    def branch_proj(rows, z):
        sz = z * jax.nn.sigmoid(z)
        ua = (ya_ref[rows, :].astype(F32) * sz[:, :Q_WIDTH]).astype(BF16)
        ub = (yb_ref[rows, :].astype(F32) * sz[:, Q_WIDTH:]).astype(BF16)
        pa = jnp.dot(ua, wpa_ref[...], preferred_element_type=F32)
        pb = jnp.dot(ub, wpb_ref[...], preferred_element_type=F32)
        return pa, pb

    def merge_norm(rows, gl, pa, pb):
        y = None
        for c in range(0, D_MODEL, D_MODEL // 2):
            ch = slice(c, c + D_MODEL // 2)
            ga = jax.nn.sigmoid(gl[:, ch] + bg_ref[:, ch])
            gb = jax.nn.sigmoid(gl[:, D_MODEL + c:D_MODEL + c + D_MODEL // 2]
                                + bg_ref[:, D_MODEL + c:D_MODEL + c + D_MODEL // 2])
            mix = (ga * pa[:, ch] + gb * pb[:, ch]).astype(BF16)
            part = jnp.dot(mix, wo_ref[ch, :], preferred_element_type=F32)
            y = part if y is None else y + part
        r = DN_ALPHA * x_ref[rows, :] + y
        mu = jnp.mean(r, axis=-1, keepdims=True)
        rc = r - mu
        var = jnp.mean(rc * rc, axis=-1, keepdims=True)
        o_ref[rows, :] = rc * lax.rsqrt(var + LN_EPS) * lng_ref[...] + lnb_ref[...]

    tiles = [slice(lo, lo + sub) for lo in range(0, x_ref.shape[0], sub)]
    stage1, stage2 = {}, {}
    for step in range(len(tiles) + 2):
        if step < len(tiles):
            stage1[step] = input_proj(tiles[step])
        if 0 <= step - 1 < len(tiles):
            stage2[step - 1] = branch_proj(tiles[step - 1], stage1[step - 1][0])
        if 0 <= step - 2 < len(tiles):
            merge_norm(tiles[step - 2], stage1.pop(step - 2)[1], *stage2.pop(step - 2))


def _out_block(x2, ya2, yb2, wz, wg, bg, wpa, wpb, wo, lng, lnb, tm, sub):
    n, d = x2.shape
    const = lambda a: pl.BlockSpec(a.shape, lambda i: (0, 0), pipeline_mode=pl.Buffered(1))
    row = lambda w: pl.BlockSpec((tm, w), lambda i: (i, 0))
    return pl.pallas_call(
        functools.partial(_out_kernel, sub=sub),
        grid=(n // tm,),
        in_specs=[row(d), row(Q_WIDTH), row(Q_WIDTH),
                  const(wz), const(wg), const(bg), const(wpa), const(wpb), const(wo), const(lng), const(lnb)],
        out_specs=row(d),
        out_shape=jax.ShapeDtypeStruct((n, d), x2.dtype),
        compiler_params=pltpu.CompilerParams(
            dimension_semantics=("parallel",), vmem_limit_bytes=VMEM_LIMIT),
        name="out_block",
    )(x2, ya2, yb2, wz, wg, bg, wpa, wpb, wo, lng, lnb)


def _rope_tables(seq):
    half = HEAD_DIM // 2
    axis_pairs = half // 2
    pos = np.arange(seq)
    freqs = ROPE_THETA ** (-np.arange(axis_pairs, dtype=np.float64) / axis_pairs)
    ang = np.concatenate([(pos // GRID_W)[:, None] * freqs, (pos % GRID_W)[:, None] * freqs], axis=-1)
    cos, sin = np.cos(ang), np.sin(ang)
    cosk = np.tile(cos, (1, KV_WIDTH // half))
    sink = np.tile(np.concatenate([-sin, sin], axis=-1), (1, KV_HEADS))
    return tuple(jnp.asarray(a, F32) for a in (cos.T, sin.T, cosk, sink))


def _layer(x, layer, w_in, b_gate, sink_a, qnorm_b, knorm_b, w_proj_a, w_proj_b, w_out, ln_g, ln_b):
    bsz, seq, d = x.shape
    tm_qkv, sub_qkv, tm_out, sub_out = 2048, MXU_TILE, 1024, MXU_TILE
    tq, kc = 2 * MXU_TILE, MXU_TILE

    wt, wk = _prep_qkv_weights(w_in, layer)
    gq = qnorm_b[layer].astype(F32)[None, :]
    gk = knorm_b[layer].astype(F32)[None, :]
    cosT, sinT, cosk, sink = _rope_tables(seq)

    qaT, ka, vaT, qbT, kb, vbT = _qkv_proj(x, wt, wk, gq, gk, cosT, sinT, cosk, sink, tm_qkv, sub_qkv)
    ya, yb, (wz, wg, wpa, wpb, wo) = _attention(
        sink_a[layer].astype(F32), qaT, ka, vaT, qbT, kb, vbT, w_in, w_proj_a, w_proj_b, w_out, layer, tq, kc)

    out = _out_block(
        x.reshape(bsz * seq, d), ya.reshape(bsz * seq, Q_WIDTH), yb.reshape(bsz * seq, Q_WIDTH),
        wz, wg, b_gate[layer].astype(F32)[None, :], wpa, wpb, wo,
        ln_g[layer].astype(F32)[None, :], ln_b[layer].astype(F32)[None, :], tm_out, sub_out)
    return out.reshape(bsz, seq, d)


def kernel(x, w_in, b_gate, sink_a, qnorm_b, knorm_b, w_proj_a, w_proj_b, w_out, ln_g, ln_b):
    h = x
    for layer in range(DEPTH):
        h = _layer(h, layer, w_in, b_gate, sink_a, qnorm_b, knorm_b, w_proj_a, w_proj_b, w_out, ln_g, ln_b)
    return h
```
